```python
import math
import jax, jax.numpy as jnp
from jax import lax
import numpy as np

D_MODEL = 2048
BATCH = 1
SEQ = 16384
DEPTH = 2

DEEPNORM_ALPHA = (2 * DEPTH) ** 0.25
DEEPNORM_BETA = (8 * DEPTH) ** -0.25
LN_EPS = 1e-5
N_BRANCHES = 3

RWKV_HEADS = 8
RWKV_HEAD = 64
RWKV_WIDTH = RWKV_HEADS * RWKV_HEAD
RWKV_DECAY_LORA = 96
RWKV_AAA_LORA = 96
RWKV_GATE_LORA = 256
RWKV_GN_EPS = 64e-5
RWKV_SPLITS = (RWKV_WIDTH, RWKV_WIDTH, RWKV_WIDTH, RWKV_DECAY_LORA, RWKV_AAA_LORA, RWKV_GATE_LORA)
RWKV_IN = 3 * RWKV_WIDTH + RWKV_DECAY_LORA + RWKV_AAA_LORA + RWKV_GATE_LORA

DIFF_HEADS = 8
DIFF_HEAD = 64
DIFF_QK = DIFF_HEADS * 2 * DIFF_HEAD
DIFF_V = DIFF_HEADS * 2 * DIFF_HEAD
DIFF_SPLITS = (DIFF_QK, DIFF_QK, DIFF_V)
DIFF_IN = 2 * DIFF_QK + DIFF_V
ROPE_THETA = 500000.0
ROPE_DIM = DIFF_HEAD // 4
Q_BLOCK = 128

GLA_HEADS = 4
GLA_KEY = 64
GLA_VALUE = 128
GLA_K_WIDTH = GLA_HEADS * GLA_KEY
GLA_V_WIDTH = GLA_HEADS * GLA_VALUE
GLA_GATE_LORA = 16
GLA_TAU = 16.0
GLA_CHUNK = 64
GLA_SPLITS = (GLA_K_WIDTH, GLA_K_WIDTH, GLA_V_WIDTH, GLA_GATE_LORA, GLA_V_WIDTH)
GLA_IN = 2 * GLA_K_WIDTH + 2 * GLA_V_WIDTH + GLA_GATE_LORA

D_IN = RWKV_IN + DIFF_IN + GLA_IN + N_BRANCHES * D_MODEL

MOE_GROUPS = 4
MOE_EXPERTS = 8
MOE_TOP_K = 2
MOE_HIDDEN = 256

kernel_name = 'hybrid_rwkv7_diffattn_gla_hmoe_deepnorm'


def split_cols(z, widths):
    offsets = np.cumsum(widths)[:-1].tolist()
    return jnp.split(z, offsets, axis=-1)


def layer_norm(x, g, b):
    xf = x.astype(jnp.float32)
    mu = jnp.mean(xf, -1, keepdims=True)
    var = jnp.mean(jnp.square(xf - mu), -1, keepdims=True)
    return ((xf - mu) * lax.rsqrt(var + LN_EPS)).astype(x.dtype) * g + b


def token_shift(z):
    return jnp.pad(z, ((0, 0), (1, 0), (0, 0)))[:, :-1]


def partial_rope(x, pos):
    half = ROPE_DIM // 2
    inv_freq = ROPE_THETA ** (-jnp.arange(half, dtype=jnp.float32) / half)
    ang = pos.astype(jnp.float32)[:, None] * inv_freq[None, :]
    cos = jnp.cos(ang)[None, :, None, None, :].astype(x.dtype)
    sin = jnp.sin(ang)[None, :, None, None, :].astype(x.dtype)
    x1, x2, rest = x[..., :half], x[..., half:ROPE_DIM], x[..., ROPE_DIM:]
    return jnp.concatenate([x1 * cos - x2 * sin, x2 * cos + x1 * sin, rest], axis=-1)


def rwkv7_mixer(r, k, v, w_lo, a_lo, g_lo, w2, w0, a2, a0, g2, kk_scale, ka_scale, r_k, gn_g, gn_b):
    B, T, _ = r.shape
    f32 = jnp.float32
    H, N = RWKV_HEADS, RWKV_HEAD
    w = -jax.nn.softplus(-(w0 + jnp.tanh(w_lo) @ w2)) - 0.5
    decay = jnp.exp(-jnp.exp(w.astype(f32)))
    a = jax.nn.sigmoid(a0 + a_lo @ a2)
    g = jax.nn.sigmoid(g_lo) @ g2

    def heads(z):
        return z.reshape(B, T, H, N).astype(f32)

    r, k, v, a, decay = heads(r), heads(k), heads(v), heads(a), heads(decay)
    kk = k * kk_scale.reshape(H, N)
    kk = kk / jnp.maximum(jnp.sqrt(jnp.sum(kk * kk, -1, keepdims=True)), 1e-12)
    k = k * (1.0 + (a - 1.0) * ka_scale.reshape(H, N))

    def step(S, inp):
        r_t, w_t, k_t, v_t, kk_t, a_t = inp
        sa = jnp.einsum('bhvk,bhk->bhv', S, -kk_t)
        S = (S * w_t[:, :, None, :] + sa[..., None] * (kk_t * a_t)[:, :, None, :]
             + v_t[..., None] * k_t[:, :, None, :])
        return S, jnp.einsum('bhvk,bhk->bhv', S, r_t)

    seqs = tuple(jnp.moveaxis(z, 1, 0) for z in (r, decay, k, v, kk, a))
    _, y = lax.scan(step, jnp.zeros((B, H, N, N), f32), seqs)
    y = jnp.moveaxis(y, 0, 1)
    mu = jnp.mean(y, -1, keepdims=True)
    var = jnp.mean(jnp.square(y - mu), -1, keepdims=True)
    y = ((y - mu) * lax.rsqrt(var + RWKV_GN_EPS)).reshape(B, T, RWKV_WIDTH) * gn_g + gn_b
    bonus = jnp.sum(r * k * r_k, -1, keepdims=True) * v
    y = (y + bonus.reshape(B, T, RWKV_WIDTH)) * g
    return y.astype(w_lo.dtype)


def diff_attention(q, k, v, lq1, lk1, lq2, lk2, subln_g, lambda_init):
    B, T, _ = q.shape
    f32 = jnp.float32
    pos = jnp.arange(T)
    q = partial_rope(q.reshape(B, T, DIFF_HEADS, 2, DIFF_HEAD), pos)
    k = partial_rope(k.reshape(B, T, DIFF_HEADS, 2, DIFF_HEAD), pos)
    v = v.reshape(B, T, DIFF_HEADS, 2 * DIFF_HEAD)
    lam = (jnp.exp(jnp.sum(lq1 * lk1).astype(f32)) - jnp.exp(jnp.sum(lq2 * lk2).astype(f32))
           + lambda_init)
    scale = DIFF_HEAD ** -0.5

    def block(i):
        start = i * Q_BLOCK
        qb = lax.dynamic_slice_in_dim(q, start, Q_BLOCK, axis=1)
        s = jnp.einsum('bqhmd,bkhmd->bhmqk', qb, k).astype(f32) * scale
        q_pos = start + jnp.arange(Q_BLOCK)
        s = jnp.where(pos[None, :] <= q_pos[:, None], s, -jnp.inf)
        p = jax.nn.softmax(s, axis=-1)
        attn = p[:, :, 0] - lam * p[:, :, 1]
        return jnp.einsum('bhqk,bkhe->bqhe', attn.astype(v.dtype), v)

    o = lax.map(block, jnp.arange(T // Q_BLOCK))
    o = jnp.moveaxis(o, 0, 1).reshape(B, T, DIFF_HEADS, 2 * DIFF_HEAD).astype(f32)
    o = o * lax.rsqrt(jnp.mean(o * o, -1, keepdims=True) + LN_EPS) * (1.0 - lambda_init)
    o = o.astype(q.dtype) * subln_g
    return o.reshape(B, T, DIFF_V)


def gla_mixer(q, k, v, g_lo, r, g2, gb, gn_g):
    B, T, _ = q.shape
    f32 = jnp.float32
    H, dk, dv, C = GLA_HEADS, GLA_KEY, GLA_VALUE, GLA_CHUNK
    nc = T // C
    log_a = jax.nn.log_sigmoid((g_lo @ g2 + gb).astype(f32)) / GLA_TAU

    def chunks(z, d):
        return z.astype(f32).reshape(B, nc, C, H, d).transpose(1, 0, 3, 2, 4)

    qc = chunks(q * dk ** -0.5, dk)
    kc, vc, gc = chunks(k, dk), chunks(v, dv), chunks(log_a, dk)
    causal = jnp.tril(jnp.ones((C, C), dtype=bool))

    def step(S, inp):
        q_c, k_c, v_c, g_c = inp
        b = jnp.cumsum(g_c, axis=-2)
        rel = jnp.exp(jnp.where(causal[:, :, None],
                                b[..., :, None, :] - b[..., None, :, :], -jnp.inf))
        att = jnp.einsum('bhik,bhjk,bhijk->bhij', q_c, k_c, rel)
        o = att @ v_c + jnp.einsum('bhik,bhkv->bhiv', q_c * jnp.exp(b), S)
        b_last = b[..., -1:, :]
        S = (jnp.exp(b_last)[..., 0, :, None] * S
             + jnp.einsum('bhjk,bhjv->bhkv', k_c * jnp.exp(b_last - b), v_c))
        return S, o

    _, o = lax.scan(step, jnp.zeros((B, H, dk, dv), f32), (qc, kc, vc, gc))
    o = o.transpose(1, 0, 3, 2, 4).reshape(B, T, H, dv)
    o = o * lax.rsqrt(jnp.mean(o * o, -1, keepdims=True) + LN_EPS)
    o = o.reshape(B, T, GLA_V_WIDTH) * gn_g
    return (o * jax.nn.silu(r)).astype(r.dtype)


def hybrid_mixer(x, w_in, rwkv_mu, rwkv_w2, rwkv_w0, rwkv_a2, rwkv_a0, rwkv_g2, rwkv_kk_scale,
                 rwkv_ka_scale, rwkv_rk, rwkv_gn_g, rwkv_gn_b, diff_lq1, diff_lk1, diff_lq2,
                 diff_lk2, diff_subln_g, gla_g2, gla_gb, gla_gn_g, proj_a, proj_b, proj_c, w_o,
                 lambda_init):
    proj = x @ w_in
    a_cols, b_cols, c_cols, gate_cols = split_cols(
        proj, (RWKV_IN, DIFF_IN, GLA_IN, N_BRANCHES * D_MODEL))
    a_cols = a_cols + (token_shift(a_cols) - a_cols) * rwkv_mu
    ar, ak, av, aw, aa, ag = split_cols(a_cols, RWKV_SPLITS)
    o_a = rwkv7_mixer(ar, ak, av, aw, aa, ag, rwkv_w2, rwkv_w0, rwkv_a2, rwkv_a0, rwkv_g2,
                      rwkv_kk_scale, rwkv_ka_scale, rwkv_rk, rwkv_gn_g, rwkv_gn_b)
    bq, bk, bv = split_cols(b_cols, DIFF_SPLITS)
    o_b = diff_attention(bq, bk, bv, diff_lq1, diff_lk1, diff_lq2, diff_lk2, diff_subln_g,
                         lambda_init)
    cq, ck, cv, cg, cr = split_cols(c_cols, GLA_SPLITS)
    o_c = gla_mixer(cq, ck, cv, cg, cr, gla_g2, gla_gb, gla_gn_g)
    g_a, g_b, g_c = split_cols(gate_cols, (D_MODEL,) * N_BRANCHES)
    merged = (jax.nn.sigmoid(g_a) * (o_a @ proj_a) + jax.nn.sigmoid(g_b) * (o_b @ proj_b)
              + jax.nn.sigmoid(g_c) * (o_c @ proj_c))
    return merged @ w_o


def hierarchical_moe(x, wr_group, br_group, wr_expert, br_expert, w_gate, w_up, w_down):
    B, T, D = x.shape
    f32 = jnp.float32
    xt = x.reshape(B * T, D)
    group_prob = jax.nn.softmax((xt @ wr_group + br_group).astype(f32), axis=-1)
    g_p, g_idx = lax.top_k(group_prob, 1)
    group_onehot = jax.nn.one_hot(g_idx[:, 0], MOE_GROUPS, dtype=f32)
    exp_logits = (xt @ wr_expert + br_expert).astype(f32).reshape(-1, MOE_GROUPS, MOE_EXPERTS)
    sel_logits = jnp.einsum('ng,nge->ne', group_onehot, exp_logits)
    e_prob = jax.nn.softmax(sel_logits, axis=-1)
    e_p, e_idx = lax.top_k(e_prob, MOE_TOP_K)
    e_p = e_p / jnp.sum(e_p, -1, keepdims=True)
    expert_w = jnp.sum(jax.nn.one_hot(e_idx, MOE_EXPERTS, dtype=f32) * e_p[..., None], axis=1)
    combine = (group_onehot[:, :, None] * (g_p * expert_w)[:, None, :]).astype(x.dtype)
    y = jnp.zeros_like(xt)
    for grp in range(MOE_GROUPS):
        h = (jax.nn.silu(jnp.einsum('nd,edf->nef', xt, w_gate[grp]))
             * jnp.einsum('nd,edf->nef', xt, w_up[grp]))
        y = y + jnp.einsum('nef,efd->nd', h * combine[:, grp, :, None], w_down[grp])
    return y.reshape(B, T, D)


def setup_inputs(seed: int = 0) -> dict:
    key = jax.random.key(seed)
    ks = iter(jax.random.split(key, 48))
    L, D = DEPTH, D_MODEL
    G, E, F = MOE_GROUPS, MOE_EXPERTS, MOE_HIDDEN

    def nrm(shape, scale):
        return jax.random.normal(next(ks), shape, jnp.float32) * scale

    def unif(shape, lo, hi):
        return jax.random.uniform(next(ks), shape, jnp.float32, minval=lo, maxval=hi)

    return {
        'x': nrm((BATCH, SEQ, D), 1.0),
        'w_in': nrm((L, D, D_IN), D ** -0.5),
        'rwkv_mu': unif((L, RWKV_IN), 0.0, 1.0),
        'rwkv_w2': nrm((L, RWKV_DECAY_LORA, RWKV_WIDTH), RWKV_DECAY_LORA ** -0.5),
        'rwkv_w0': unif((L, RWKV_WIDTH), -6.0, -1.0),
        'rwkv_a2': nrm((L, RWKV_AAA_LORA, RWKV_WIDTH), RWKV_AAA_LORA ** -0.5),
        'rwkv_a0': nrm((L, RWKV_WIDTH), 0.1),
        'rwkv_g2': nrm((L, RWKV_GATE_LORA, RWKV_WIDTH), RWKV_GATE_LORA ** -0.5),
        'rwkv_kk_scale': 0.85 + nrm((L, RWKV_WIDTH), 0.02),
        'rwkv_ka_scale': 1.0 + nrm((L, RWKV_WIDTH), 0.02),
        'rwkv_rk': nrm((L, RWKV_HEADS, RWKV_HEAD), 0.1),
        'rwkv_gn_g': 1.0 + nrm((L, RWKV_WIDTH), 0.02),
        'rwkv_gn_b': nrm((L, RWKV_WIDTH), 0.02),
        'diff_lq1': nrm((L, DIFF_HEAD), 0.1),
        'diff_lk1': nrm((L, DIFF_HEAD), 0.1),
        'diff_lq2': nrm((L, DIFF_HEAD), 0.1),
        'diff_lk2': nrm((L, DIFF_HEAD), 0.1),
        'diff_subln_g': 1.0 + nrm((L, 2 * DIFF_HEAD), 0.02),
        'gla_g2': nrm((L, GLA_GATE_LORA, GLA_K_WIDTH), GLA_GATE_LORA ** -0.5),
        'gla_gb': nrm((L, GLA_K_WIDTH), 0.1),
        'gla_gn_g': 1.0 + nrm((L, GLA_V_WIDTH), 0.02),
        'proj_a': nrm((L, RWKV_WIDTH, D), RWKV_WIDTH ** -0.5),
        'proj_b': nrm((L, DIFF_V, D), DIFF_V ** -0.5),
        'proj_c': nrm((L, GLA_V_WIDTH, D), GLA_V_WIDTH ** -0.5),
        'w_o': nrm((L, D, D), D ** -0.5 * DEEPNORM_BETA),
        'ln1_g': 1.0 + nrm((L, D), 0.02),
        'ln1_b': nrm((L, D), 0.02),
        'router_group_w': nrm((L, D, G), D ** -0.5),
        'router_group_b': nrm((L, G), 0.01),
        'router_expert_w': nrm((L, D, G * E), D ** -0.5),
        'router_expert_b': nrm((L, G * E), 0.01),
        'moe_w_gate': nrm((L, G, E, D, F), D ** -0.5),
        'moe_w_up': nrm((L, G, E, D, F), D ** -0.5),
        'moe_w_down': nrm((L, G, E, F, D), F ** -0.5 * DEEPNORM_BETA),
        'ln2_g': 1.0 + nrm((L, D), 0.02),
        'ln2_b': nrm((L, D), 0.02),
    }


def reference(x, w_in, rwkv_mu, rwkv_w2, rwkv_w0, rwkv_a2, rwkv_a0, rwkv_g2, rwkv_kk_scale,
              rwkv_ka_scale, rwkv_rk, rwkv_gn_g, rwkv_gn_b, diff_lq1, diff_lk1, diff_lq2, diff_lk2,
              diff_subln_g, gla_g2, gla_gb, gla_gn_g, proj_a, proj_b, proj_c, w_o, ln1_g, ln1_b,
              router_group_w, router_group_b, router_expert_w, router_expert_b, moe_w_gate,
              moe_w_up, moe_w_down, ln2_g, ln2_b):
    for l in range(DEPTH):
        lambda_init = 0.8 - 0.6 * math.exp(-0.3 * l)
        h = hybrid_mixer(x, w_in[l], rwkv_mu[l], rwkv_w2[l], rwkv_w0[l], rwkv_a2[l], rwkv_a0[l],
                         rwkv_g2[l], rwkv_kk_scale[l], rwkv_ka_scale[l], rwkv_rk[l], rwkv_gn_g[l],
                         rwkv_gn_b[l], diff_lq1[l], diff_lk1[l], diff_lq2[l], diff_lk2[l],
                         diff_subln_g[l], gla_g2[l], gla_gb[l], gla_gn_g[l], proj_a[l], proj_b[l],
                         proj_c[l], w_o[l], lambda_init)
        x = layer_norm(DEEPNORM_ALPHA * x + h, ln1_g[l], ln1_b[l])
        h = hierarchical_moe(x, router_group_w[l], router_group_b[l], router_expert_w[l],
                             router_expert_b[l], moe_w_gate[l], moe_w_up[l], moe_w_down[l])
        x = layer_norm(DEEPNORM_ALPHA * x + h, ln2_g[l], ln2_b[l])
    return x
```

```python
import functools
import math

import numpy as np
import jax
import jax.numpy as jnp
from jax import lax
from jax.experimental import pallas as pl
from jax.experimental.pallas import tpu as pltpu

F32 = jnp.float32
BF16 = jnp.bfloat16
HIGHEST = lax.Precision.HIGHEST

D_MODEL = 2048
DEPTH = 2
DEEPNORM_ALPHA = (2 * DEPTH) ** 0.25
LN_EPS = 1e-5

RWKV_HEADS = 8
RWKV_HEAD = 64
RWKV_WIDTH = RWKV_HEADS * RWKV_HEAD
RWKV_DECAY_LORA = 96
RWKV_AAA_LORA = 96
RWKV_GATE_LORA = 256
RWKV_GN_EPS = 64e-5
RWKV_IN = 3 * RWKV_WIDTH + RWKV_DECAY_LORA + RWKV_AAA_LORA + RWKV_GATE_LORA
RWKV_CHUNK = 64
LORA_PAD = 128
RWKV_IN_PAD = 3 * RWKV_WIDTH + 2 * LORA_PAD + RWKV_GATE_LORA

DIFF_HEADS = 8
DIFF_HEAD = 64
DIFF_QK = DIFF_HEADS * 2 * DIFF_HEAD
DIFF_V = DIFF_HEADS * 2 * DIFF_HEAD
DIFF_IN = 2 * DIFF_QK + DIFF_V
ROPE_THETA = 500000.0
ROPE_DIM = DIFF_HEAD // 4
ROPE_HALF = ROPE_DIM // 2

GLA_HEADS = 4
GLA_KEY = 64
GLA_VALUE = 128
GLA_K_WIDTH = GLA_HEADS * GLA_KEY
GLA_V_WIDTH = GLA_HEADS * GLA_VALUE
GLA_GATE_LORA = 16
GLA_TAU = 16.0
GLA_SUB = 16
GLA_IN = 2 * GLA_K_WIDTH + 2 * GLA_V_WIDTH + GLA_GATE_LORA
GLA_IN_PAD = 2 * GLA_K_WIDTH + 2 * GLA_V_WIDTH + 128

N_BRANCHES = 3
MOE_GROUPS = 4
MOE_EXPERTS = 8
MOE_HIDDEN = 256
N_EXPERTS = MOE_GROUPS * MOE_EXPERTS
LANE = 128
V7X_VMEM_LIMIT = 56 * 1024 * 1024


def _cparams(sem):
    return pltpu.CompilerParams(dimension_semantics=sem, vmem_limit_bytes=V7X_VMEM_LIMIT)


def _nt(a, b, **kw):
    return lax.dot_general(a, b, (((1,), (1,)), ((), ())), preferred_element_type=F32, **kw)


def _tn(a, b, **kw):
    return lax.dot_general(a, b, (((0,), (0,)), ((), ())), preferred_element_type=F32, **kw)


def _dot(a, b, **kw):
    return jnp.dot(a, b, preferred_element_type=F32, **kw)


def _mm_kernel(a_ref, b_ref, o_ref):
    o_ref[...] = _dot(a_ref[...], b_ref[...]).astype(o_ref.dtype)


def _matmul(a, b, out_dtype, tm, tn):
    M, K = a.shape
    N = b.shape[1]
    tm, tn = min(tm, M), min(tn, N)
    return pl.pallas_call(
        _mm_kernel,
        grid=(N // tn, M // tm),
        in_specs=[pl.BlockSpec((tm, K), lambda j, i: (i, 0)),
                  pl.BlockSpec((K, tn), lambda j, i: (0, j))],
        out_specs=pl.BlockSpec((tm, tn), lambda j, i: (i, j)),
        out_shape=jax.ShapeDtypeStruct((M, N), out_dtype),
        compiler_params=_cparams(("arbitrary", "arbitrary")),
        name="proj_mm",
    )(a, b)


def _mm_rope_kernel(a_ref, b_ref, c_ref, sa_ref, sb_ref, o_ref):
    acc = _dot(a_ref[...], b_ref[...])
    c, sa, sb = c_ref[...], sa_ref[...], sb_ref[...]
    for h in range(acc.shape[1] // LANE):
        blk = acc[:, h * LANE:(h + 1) * LANE]
        out = blk * c + pltpu.roll(blk, LANE - ROPE_HALF, 1) * sa + pltpu.roll(blk, ROPE_HALF, 1) * sb
        o_ref[:, h * LANE:(h + 1) * LANE] = out.astype(o_ref.dtype)


def _matmul_rope(a, b, tabs, tm, tn):
    M, K = a.shape
    N = b.shape[1]
    tm, tn = min(tm, M), min(tn, N)
    tab_spec = pl.BlockSpec((tm, LANE), lambda j, i: (i, 0))
    return pl.pallas_call(
        _mm_rope_kernel,
        grid=(N // tn, M // tm),
        in_specs=[pl.BlockSpec((tm, K), lambda j, i: (i, 0)),
                  pl.BlockSpec((K, tn), lambda j, i: (0, j)),
                  tab_spec, tab_spec, tab_spec],
        out_specs=pl.BlockSpec((tm, tn), lambda j, i: (i, j)),
        out_shape=jax.ShapeDtypeStruct((M, N), BF16),
        compiler_params=_cparams(("arbitrary", "arbitrary")),
        name="proj_qk_rope",
    )(a, b, *tabs)


def _mm_vt_kernel(wt_ref, x_ref, o_ref):
    o_ref[0] = _nt(wt_ref[...], x_ref[...]).astype(o_ref.dtype)


def _matmul_vt(wt, x, tk):
    N, K = wt.shape
    T = x.shape[0]
    return pl.pallas_call(
        _mm_vt_kernel,
        grid=(T // tk,),
        in_specs=[pl.BlockSpec((N, K), lambda i: (0, 0)),
                  pl.BlockSpec((tk, K), lambda i: (i, 0))],
        out_specs=pl.BlockSpec((1, N, tk), lambda i: (i, 0, 0)),
        out_shape=jax.ShapeDtypeStruct((T // tk, N, tk), BF16),
        compiler_params=_cparams(("arbitrary",)),
        name="proj_vt",
    )(wt, x)


def _rope_tables(T):
    pos = jnp.arange(T, dtype=F32)[:, None]
    inv_freq = ROPE_THETA ** (-jnp.arange(ROPE_HALF, dtype=F32) / ROPE_HALF)
    lane = np.arange(LANE) % DIFF_HEAD
    ang = pos * inv_freq[None, :]
    cos8, sin8 = jnp.cos(ang), jnp.sin(ang)
    idx = jnp.asarray(lane % ROPE_HALF)
    cos_l, sin_l = cos8[:, idx], sin8[:, idx]
    first = jnp.asarray(lane < ROPE_HALF)[None, :]
    second = jnp.asarray((lane >= ROPE_HALF) & (lane < ROPE_DIM))[None, :]
    c = jnp.where(first | second, cos_l, 1.0)
    sa = jnp.where(first, -sin_l, 0.0)
    sb = jnp.where(second, sin_l, 0.0)
    return c, sa, sb


def _rwkv_kernel(a_ref, mu_ref, w2_ref, w0_ref, a2_ref, a0_ref, g2_ref, kks_ref, kas_ref, rk_ref,
                 gng_ref, gnb_ref, seg_ref, tri_ref, o_ref,
                 prev_ref, h_ref, at_ref, rt_ref, bt_ref, kt_ref, bp_ref, kp_ref, v_ref, gc_ref, y_ref):
    TB = a_ref.shape[0]
    C, N, W = RWKV_CHUNK, RWKV_HEAD, RWKV_WIDTH

    @pl.when(pl.program_id(0) == 0)
    def _():
        prev_ref[...] = jnp.zeros_like(prev_ref)
        h_ref[...] = jnp.zeros_like(h_ref)

    a = a_ref[...]
    row = lax.broadcasted_iota(jnp.int32, a.shape, 0)
    shifted = jnp.where(row == 0, prev_ref[...], pltpu.roll(a, 1, 0))
    prev_ref[...] = a[TB - 1:TB, :]
    xs = a + (shifted - a) * mu_ref[...]

    r = xs[:, 0:W]
    k = xs[:, W:2 * W]
    v = xs[:, 2 * W:3 * W]
    w_lo = xs[:, 3 * W:3 * W + LORA_PAD]
    a_lo = xs[:, 3 * W + LORA_PAD:3 * W + 2 * LORA_PAD]
    g_lo = xs[:, 3 * W + 2 * LORA_PAD:]

    z = w0_ref[...] + _dot(jnp.tanh(w_lo).astype(BF16), w2_ref[...])
    w = -jax.nn.softplus(-z) - 0.5
    lw = -jnp.exp(w)
    aa = jax.nn.sigmoid(a0_ref[...] + _dot(a_lo.astype(BF16), a2_ref[...]))
    g = _dot(jax.nn.sigmoid(g_lo).astype(BF16), g2_ref[...])

    seg = seg_ref[...]
    kk = k * kks_ref[...]
    ss = _dot((kk * kk).astype(BF16), seg)
    kk = kk * lax.rsqrt(jnp.maximum(ss, 1e-24))
    km = k * (1.0 + (aa - 1.0) * kas_ref[...])
    beta = kk * aa

    tri = tri_ref[...]
    cum = _dot(tri, lw, precision=HIGHEST)
    rowc = lax.broadcasted_iota(jnp.int32, (TB, TB), 0) // C
    colc = lax.broadcasted_iota(jnp.int32, (TB, TB), 1) // C
    allc = jnp.where(rowc == colc, 1.0, 0.0).astype(F32)
    cum_c = _dot(allc, lw, precision=HIGHEST)

    e_in = jnp.exp(cum)
    e_neg = jnp.exp(-cum)
    e_end = jnp.exp(cum_c - cum)
    at_ref[...] = (-kk * jnp.exp(cum - lw)).astype(BF16)
    rt_ref[...] = (r * e_in).astype(BF16)
    bt_ref[...] = (beta * e_neg).astype(BF16)
    kt_ref[...] = (km * e_neg).astype(BF16)
    bp_ref[...] = (beta * e_end).astype(BF16)
    kp_ref[...] = (km * e_end).astype(BF16)
    v_ref[...] = v.astype(BF16)
    gc_ref[...] = jnp.exp(cum_c)

    ri = lax.broadcasted_iota(jnp.int32, (C, C), 0)
    ci = lax.broadcasted_iota(jnp.int32, (C, C), 1)
    strict = ci < ri
    incl = ci <= ri
    eye = ci == ri

    def chunk(c, carry):
        rows = pl.ds(pl.multiple_of(c * C, C), C)
        for h in range(RWKV_HEADS):
            L = slice(h * N, (h + 1) * N)
            at, rt = at_ref[rows, L], rt_ref[rows, L]
            bt, kt = bt_ref[rows, L], kt_ref[rows, L]
            bp, kp = bp_ref[rows, L], kp_ref[rows, L]
            vh = v_ref[rows, L]
            gam = gc_ref[rows, L][0:1, :]
            a_ab = jnp.where(strict, _nt(at, bt), 0.0)
            a_ak = jnp.where(strict, _nt(at, kt), 0.0).astype(BF16)
            a_rb = jnp.where(incl, _nt(rt, bt), 0.0).astype(BF16)
            a_rk = jnp.where(incl, _nt(rt, kt), 0.0).astype(BF16)
            t_inv = jnp.where(eye, 1.0, 0.0) + a_ab
            ap = a_ab
            for _ in range(5):
                apb = ap.astype(BF16)
                ap = _dot(apb, apb)
                t_inv = t_inv + _dot(t_inv.astype(BF16), ap.astype(BF16))
            tb = t_inv.astype(BF16)
            p = _dot(tb, at)
            u0 = _dot(tb, _dot(a_ak, vh).astype(BF16))
            pb, u0b = p.astype(BF16), u0.astype(BF16)
            q = rt.astype(F32) + _dot(a_rb, pb)
            y0 = _dot(a_rb, u0b) + _dot(a_rk, vh)
            gmat = jnp.where(eye, gam, 0.0) + _tn(bp, pb)
            hadd = _tn(bp, u0b) + _tn(kp, vh)
            h0 = h_ref[h]
            y_ref[rows, L] = _dot(q, h0, precision=HIGHEST) + y0
            h_ref[h] = _dot(gmat, h0, precision=HIGHEST) + hadd
        return carry

    lax.fori_loop(0, TB // C, chunk, 0)

    y = y_ref[...]
    segm = seg_ref[...]
    mu = _dot(y.astype(BF16), segm) * (1.0 / N)
    yc = y - mu
    var = _dot((yc * yc).astype(BF16), segm) * (1.0 / N)
    yn = yc * lax.rsqrt(var + RWKV_GN_EPS) * gng_ref[...] + gnb_ref[...]
    bonus = _dot((r * km * rk_ref[...]).astype(BF16), segm) * v
    o_ref[...] = ((yn + bonus) * g).astype(o_ref.dtype)


def _rwkv(a_proj, mu, w2, w0, a2, a0, g2, kks, kas, rk, gng, gnb, tb):
    T = a_proj.shape[0]
    tb = min(tb, T)
    W = RWKV_WIDTH
    hid = np.arange(W) // RWKV_HEAD
    seg = jnp.asarray(hid[:, None] == hid[None, :], dtype=BF16)
    t_idx = np.arange(tb)
    tri = jnp.asarray((t_idx[:, None] // RWKV_CHUNK == t_idx[None, :] // RWKV_CHUNK)
                      & (t_idx[None, :] <= t_idx[:, None]), dtype=F32)
    full = lambda shp: pl.BlockSpec(shp, lambda i: (0,) * len(shp))
    row = lambda n: full((1, n))
    bf = lambda: pltpu.VMEM((tb, W), BF16)
    return pl.pallas_call(
        _rwkv_kernel,
        grid=(T // tb,),
        in_specs=[pl.BlockSpec((tb, RWKV_IN_PAD), lambda i: (i, 0)),
                  row(RWKV_IN_PAD), full((LORA_PAD, W)), row(W), full((LORA_PAD, W)), row(W),
                  full((RWKV_GATE_LORA, W)), row(W), row(W), row(W), row(W), row(W),
                  full((W, W)), full((tb, tb))],
        out_specs=pl.BlockSpec((tb, W), lambda i: (i, 0)),
        out_shape=jax.ShapeDtypeStruct((T, W), BF16),
        scratch_shapes=[pltpu.VMEM((1, RWKV_IN_PAD), F32),
                        pltpu.VMEM((RWKV_HEADS, RWKV_HEAD, RWKV_HEAD), F32),
                        bf(), bf(), bf(), bf(), bf(), bf(), bf(),
                        pltpu.VMEM((tb, W), F32), pltpu.VMEM((tb, W), F32)],
        compiler_params=_cparams(("arbitrary",)),
        name="rwkv7",
    )(a_proj, mu, w2, w0, a2, a0, g2, kks, kas, rk, gng, gnb, seg, tri)


def _diff_kernel(q_ref, k_ref, vt_ref, lq1_ref, lk1_ref, lq2_ref, lk2_ref, sg_ref, o_ref,
                 m_ref, l_ref, acc_ref, *, lambda_init):
    tq = q_ref.shape[0]
    tk = vt_ref.shape[2]
    i = pl.program_id(1)
    q = q_ref[...]
    lane = lax.broadcasted_iota(jnp.int32, q.shape, 1)
    qs = (jnp.where(lane < DIFF_HEAD, q, jnp.zeros_like(q)),
          jnp.where(lane >= DIFF_HEAD, q, jnp.zeros_like(q)))
    m_ref[...] = jnp.full(m_ref.shape, -jnp.inf, F32)
    l_ref[...] = jnp.zeros_like(l_ref)
    acc_ref[...] = jnp.zeros_like(acc_ref)

    def block(j, masked):
        kj = k_ref[pl.ds(pl.multiple_of(j * tk, tk), tk), :]
        vj = vt_ref[j]
        if masked:
            kpos = lax.broadcasted_iota(jnp.int32, (tk, tq), 0)
            qpos = lax.broadcasted_iota(jnp.int32, (tk, tq), 1)
            keep = kpos <= qpos
        for mi in range(2):
            s = _nt(kj, qs[mi])
            if masked:
                s = jnp.where(keep, s, -jnp.inf)
            m_old = m_ref[mi]
            m_new = jnp.maximum(m_old, jnp.max(s, axis=0, keepdims=True))
            alpha = jnp.exp(m_old - m_new)
            p = jnp.exp(s - m_new)
            l_ref[mi] = alpha * l_ref[mi] + jnp.sum(p, axis=0, keepdims=True)
            acc_ref[mi] = alpha * acc_ref[mi] + _dot(vj, p.astype(BF16))
            m_ref[mi] = m_new

    def body(j, carry):
        block(j, False)
        return carry

    lax.fori_loop(0, i, body, 0)
    block(i, True)

    lam = (jnp.exp(jnp.sum(lq1_ref[...] * lk1_ref[...])) - jnp.exp(jnp.sum(lq2_ref[...] * lk2_ref[...]))
           + lambda_init)
    o = acc_ref[0] / l_ref[0] - lam * (acc_ref[1] / l_ref[1])
    o = o * lax.rsqrt(jnp.mean(o * o, axis=0, keepdims=True) + LN_EPS) * (1.0 - lambda_init)
    o = o * sg_ref[...]
    o_ref[...] = o.T.astype(o_ref.dtype)


def _diff_attention(qk, vt, lq1, lk1, lq2, lk2, subln_g, lambda_init, tq):
    T = qk.shape[0]
    tk = vt.shape[2]
    assert tq == tk
    H, E = DIFF_HEADS, 2 * DIFF_HEAD
    vec = lambda: pl.BlockSpec((1, DIFF_HEAD), lambda h, i: (0, 0))
    return pl.pallas_call(
        functools.partial(_diff_kernel, lambda_init=lambda_init),
        grid=(H, T // tq),
        in_specs=[pl.BlockSpec((tq, E), lambda h, i: (i, h)),
                  pl.BlockSpec((T, E), lambda h, i: (0, H + h)),
                  pl.BlockSpec((T // tk, E, tk), lambda h, i: (0, h, 0)),
                  vec(), vec(), vec(), vec(),
                  pl.BlockSpec((E, 1), lambda h, i: (0, 0))],
        out_specs=pl.BlockSpec((tq, E), lambda h, i: (i, h)),
        out_shape=jax.ShapeDtypeStruct((T, DIFF_V), BF16),
        scratch_shapes=[pltpu.VMEM((2, 1, tq), F32), pltpu.VMEM((2, 1, tq), F32),
                        pltpu.VMEM((2, E, tq), F32)],
        compiler_params=_cparams(("arbitrary", "arbitrary")),
        name="diff_attn",
    )(qk, qk, vt, lq1, lk1, lq2, lk2, subln_g)


def _gla_kernel(c_ref, g2_ref, gb_ref, gng_ref, tri_ref, segv_ref, o_ref,
                st_ref, q_ref, k_ref, b_ref, oacc_ref):
    TB = c_ref.shape[0]
    KW, VW, S = GLA_K_WIDTH, GLA_V_WIDTH, GLA_SUB
    DK, DV = GLA_KEY, GLA_VALUE

    @pl.when(pl.program_id(0) == 0)
    def _():
        st_ref[...] = jnp.zeros_like(st_ref)

    g_lo = c_ref[:, 2 * KW + 2 * VW:]
    gate = _dot(g_lo, g2_ref[...], precision=HIGHEST) + gb_ref[...]
    log_a = jax.nn.log_sigmoid(gate) * (1.0 / GLA_TAU)
    b_ref[...] = _dot(tri_ref[...], log_a, precision=HIGHEST)
    q_ref[...] = c_ref[:, 0:KW]
    k_ref[...] = c_ref[:, KW:2 * KW]

    ri = lax.broadcasted_iota(jnp.int32, (S, 1), 0)
    segv = segv_ref[...]

    def sub(s, carry):
        rows = pl.ds(pl.multiple_of(s * S, S), S)
        q = q_ref[rows, :]
        k = k_ref[rows, :]
        b = b_ref[rows, :]
        v = c_ref[rows, 2 * KW:2 * KW + VW]
        b_last = b[S - 1:S, :]
        qb = (q * jnp.exp(b)).astype(BF16)
        kb = (k * jnp.exp(b_last - b)).astype(BF16)
        dec = jnp.exp(b_last)
        vb = v.astype(BF16)
        acc = jnp.zeros((S, VW), F32)
        for j in range(S):
            e = jnp.exp(jnp.minimum(b - b[j:j + 1, :], 0.0))
            t = (q * e * k[j:j + 1, :]).astype(BF16)
            att = _dot(t, segv)
            acc = acc + jnp.where(ri >= j, att, 0.0) * v[j:j + 1, :]
        for h in range(GLA_HEADS):
            lk = slice(h * DK, (h + 1) * DK)
            lv = slice(h * DV, (h + 1) * DV)
            st = st_ref[h]
            inter = _nt(qb[:, lk], st.astype(BF16))
            oacc_ref[rows, lv] = acc[:, lv] + inter
            st_ref[h] = st * dec[:, lk] + _tn(vb[:, lv], kb[:, lk])
        return carry

    lax.fori_loop(0, TB // S, sub, 0)

    o = oacc_ref[...]
    rgate = c_ref[:, 2 * KW + VW:2 * KW + 2 * VW]
    for h in range(GLA_HEADS):
        lv = slice(h * DV, (h + 1) * DV)
        oh = o[:, lv]
        oh = oh * lax.rsqrt(jnp.mean(oh * oh, axis=1, keepdims=True) + LN_EPS) * gng_ref[:, lv]
        o_ref[:, lv] = (oh * jax.nn.silu(rgate[:, lv])).astype(o_ref.dtype)


def _gla(c_proj, g2, gb, gng, tb):
    T = c_proj.shape[0]
    tb = min(tb, T)
    KW, VW = GLA_K_WIDTH, GLA_V_WIDTH
    t_idx = np.arange(tb)
    tri = jnp.asarray((t_idx[:, None] // GLA_SUB == t_idx[None, :] // GLA_SUB)
                      & (t_idx[None, :] <= t_idx[:, None]), dtype=F32)
    segv = jnp.asarray((np.arange(KW) // GLA_KEY)[:, None] == (np.arange(VW) // GLA_VALUE)[None, :], dtype=BF16)
    full = lambda shp: pl.BlockSpec(shp, lambda i: (0,) * len(shp))
    return pl.pallas_call(
        _gla_kernel,
        grid=(T // tb,),
        in_specs=[pl.BlockSpec((tb, GLA_IN_PAD), lambda i: (i, 0)),
                  full((LANE, KW)), full((1, KW)), full((1, VW)), full((tb, tb)), full((KW, VW))],
        out_specs=pl.BlockSpec((tb, VW), lambda i: (i, 0)),
        out_shape=jax.ShapeDtypeStruct((T, VW), BF16),
        scratch_shapes=[pltpu.VMEM((GLA_HEADS, GLA_VALUE, GLA_KEY), F32),
                        pltpu.VMEM((tb, KW), F32), pltpu.VMEM((tb, KW), F32), pltpu.VMEM((tb, KW), F32),
                        pltpu.VMEM((tb, VW), F32)],
        compiler_params=_cparams(("arbitrary",)),
        name="gla",
    )(c_proj, g2, gb, gng, tri, segv)


def _merge_kernel(oa_ref, ob_ref, oc_ref, g_ref, pa_ref, pb_ref, pc_ref, o_ref):
    D = D_MODEL
    m = jax.nn.sigmoid(g_ref[:, 0:D]) * _dot(oa_ref[...], pa_ref[...])
    m = m + jax.nn.sigmoid(g_ref[:, D:2 * D]) * _dot(ob_ref[...], pb_ref[...])
    m = m + jax.nn.sigmoid(g_ref[:, 2 * D:3 * D]) * _dot(oc_ref[...], pc_ref[...])
    o_ref[...] = m.astype(o_ref.dtype)


def _merge(oa, ob, oc, gates, pa, pb, pc, tm):
    T = oa.shape[0]
    tm = min(tm, T)
    D = D_MODEL
    rowblk = lambda n: pl.BlockSpec((tm, n), lambda i: (i, 0))
    full = lambda shp: pl.BlockSpec(shp, lambda i: (0, 0))
    return pl.pallas_call(
        _merge_kernel,
        grid=(T // tm,),
        in_specs=[rowblk(RWKV_WIDTH), rowblk(DIFF_V), rowblk(GLA_V_WIDTH), rowblk(3 * D),
                  full((RWKV_WIDTH, D)), full((DIFF_V, D)), full((GLA_V_WIDTH, D))],
        out_specs=rowblk(D),
        out_shape=jax.ShapeDtypeStruct((T, D), BF16),
        compiler_params=_cparams(("arbitrary",)),
        name="merge",
    )(oa, ob, oc, gates, pa, pb, pc)


def _layer_norm(z, g, b):
    mu = jnp.mean(z, axis=1, keepdims=True)
    zc = z - mu
    var = jnp.mean(zc * zc, axis=1, keepdims=True)
    return zc * lax.rsqrt(var + LN_EPS) * g + b


def _mm_ln_kernel(a_ref, w_ref, res_ref, g_ref, b_ref, o_ref, ob_ref):
    z = DEEPNORM_ALPHA * res_ref[...] + _dot(a_ref[...], w_ref[...])
    out = _layer_norm(z, g_ref[...], b_ref[...])
    o_ref[...] = out
    ob_ref[...] = out.astype(BF16)


def _mm_ln(a, w, res, g, b, tm):
    T, K = a.shape
    D = D_MODEL
    tm = min(tm, T)
    rowblk = lambda n: pl.BlockSpec((tm, n), lambda i: (i, 0))
    full = lambda shp: pl.BlockSpec(shp, lambda i: (0, 0))
    return pl.pallas_call(
        _mm_ln_kernel,
        grid=(T // tm,),
        in_specs=[rowblk(K), full((K, D)), rowblk(D), full((1, D)), full((1, D))],
        out_specs=[rowblk(D), rowblk(D)],
        out_shape=[jax.ShapeDtypeStruct((T, D), F32), jax.ShapeDtypeStruct((T, D), BF16)],
        compiler_params=_cparams(("arbitrary",)),
        name="wo_ln",
    )(a, w, res, g, b)


def _router_kernel(x_ref, w_ref, b_ref, o_ref):
    G, E = MOE_GROUPS, MOE_EXPERTS
    logits = _dot(x_ref[...], w_ref[...], precision=HIGHEST) + b_ref[...]
    lane = lax.broadcasted_iota(jnp.int32, logits.shape, 1)
    neg = -jnp.inf
    big = jnp.int32(1 << 20)
    is_g = lane < G
    lg = jnp.where(is_g, logits, neg)
    gmax = jnp.max(lg, axis=1, keepdims=True)
    gidx = jnp.min(jnp.where(is_g & (lg == gmax), lane, big), axis=1, keepdims=True)
    g_p = 1.0 / jnp.sum(jnp.exp(lg - gmax), axis=1, keepdims=True)
    lo = G + gidx * E
    in_grp = (lane >= lo) & (lane < lo + E)
    le = jnp.where(in_grp, logits, neg)
    m1 = jnp.max(le, axis=1, keepdims=True)
    i1 = jnp.min(jnp.where(in_grp & (le == m1), lane, big), axis=1, keepdims=True)
    le2 = jnp.where(lane == i1, neg, le)
    m2 = jnp.max(le2, axis=1, keepdims=True)
    i2 = jnp.min(jnp.where(in_grp & (le2 == m2), lane, big), axis=1, keepdims=True)
    e2 = jnp.exp(m2 - m1)
    p1 = 1.0 / (1.0 + e2)
    p2 = e2 / (1.0 + e2)
    comb = jnp.where(lane == i1, g_p * p1, 0.0) + jnp.where(lane == i2, g_p * p2, 0.0)
    o_ref[...] = comb


def _router(x1, wr, br, tm):
    T, D = x1.shape
    tm = min(tm, T)
    return pl.pallas_call(
        _router_kernel,
        grid=(T // tm,),
        in_specs=[pl.BlockSpec((tm, D), lambda i: (i, 0)),
                  pl.BlockSpec((D, LANE), lambda i: (0, 0)),
                  pl.BlockSpec((1, LANE), lambda i: (0, 0))],
        out_specs=pl.BlockSpec((tm, LANE), lambda i: (i, 0)),
        out_shape=jax.ShapeDtypeStruct((T, LANE), F32),
        compiler_params=_cparams(("arbitrary",)),
        name="router",
    )(x1, wr, br)


def _moe_kernel(xb_ref, x_ref, comb_ref, wg_ref, wu_ref, wd_ref, g_ref, b_ref, o_ref, ob_ref, acc_ref):
    e = pl.program_id(1)

    @pl.when(e == 0)
    def _():
        acc_ref[...] = jnp.zeros_like(acc_ref)

    xb = xb_ref[...]
    hg = _dot(xb, wg_ref[0])
    hu = _dot(xb, wu_ref[0])
    comb = comb_ref[...]
    lane = lax.broadcasted_iota(jnp.int32, comb.shape, 1)
    cw = jnp.sum(jnp.where(lane == e + MOE_GROUPS, comb, 0.0), axis=1, keepdims=True)
    h = (jax.nn.silu(hg) * hu * cw).astype(BF16)
    acc_ref[...] += _dot(h, wd_ref[0])

    @pl.when(e == N_EXPERTS - 1)
    def _():
        z = DEEPNORM_ALPHA * x_ref[...] + acc_ref[...]
        out = _layer_norm(z, g_ref[...], b_ref[...])
        o_ref[...] = out
        ob_ref[...] = out.astype(BF16)


def _moe(xb, x, comb, wg, wu, wd, g, b, tm):
    T, D = x.shape
    F = MOE_HIDDEN
    tm = min(tm, T)
    rowblk = lambda n: pl.BlockSpec((tm, n), lambda i, e: (i, 0))
    return pl.pallas_call(
        _moe_kernel,
        grid=(T // tm, N_EXPERTS),
        in_specs=[rowblk(D), rowblk(D), rowblk(LANE),
                  pl.BlockSpec((1, D, F), lambda i, e: (e, 0, 0)),
                  pl.BlockSpec((1, D, F), lambda i, e: (e, 0, 0)),
                  pl.BlockSpec((1, F, D), lambda i, e: (e, 0, 0)),
                  pl.BlockSpec((1, D), lambda i, e: (0, 0)),
                  pl.BlockSpec((1, D), lambda i, e: (0, 0))],
        out_specs=[rowblk(D), rowblk(D)],
        out_shape=[jax.ShapeDtypeStruct((T, D), F32), jax.ShapeDtypeStruct((T, D), BF16)],
        scratch_shapes=[pltpu.VMEM((tm, D), F32)],
        compiler_params=_cparams(("arbitrary", "arbitrary")),
        name="moe_experts",
    )(xb, x, comb, wg, wu, wd, g, b)


def _pad_cols(w, n):
    return jnp.pad(w, ((0, 0), (0, n - w.shape[1])))


def _layer(x, xb, l, p, rope_tabs):
    T = x.shape[0]
    W = RWKV_WIDTH
    KW, VW = GLA_K_WIDTH, GLA_V_WIDTH
    lambda_init = 0.8 - 0.6 * math.exp(-0.3 * l)

    w_in = p['w_in'][l]
    o_b = RWKV_IN
    o_c = o_b + DIFF_IN
    o_g = o_c + GLA_IN
    lo1 = 3 * W + RWKV_DECAY_LORA
    lo2 = lo1 + RWKV_AAA_LORA
    w_a = jnp.concatenate([_pad_cols(w_in[:, :lo1], 3 * W + LORA_PAD), _pad_cols(w_in[:, lo1:lo2], LORA_PAD),
                           w_in[:, lo2:o_b]], axis=1).astype(BF16)
    mu = p['rwkv_mu'][l]
    mu_a = jnp.concatenate([jnp.pad(mu[:lo1], (0, LORA_PAD - RWKV_DECAY_LORA)),
                            jnp.pad(mu[lo1:lo2], (0, LORA_PAD - RWKV_AAA_LORA)), mu[lo2:]])[None, :]
    qscale = DIFF_HEAD ** -0.5
    w_qk = jnp.concatenate([w_in[:, o_b:o_b + DIFF_QK] * qscale, w_in[:, o_b + DIFF_QK:o_b + 2 * DIFF_QK]],
                           axis=1).astype(BF16)
    w_vt = w_in[:, o_b + 2 * DIFF_QK:o_c].T.astype(BF16)
    c0 = o_c
    w_c = jnp.concatenate([w_in[:, c0:c0 + KW] * (GLA_KEY ** -0.5), w_in[:, c0 + KW:c0 + 2 * KW + VW],
                           w_in[:, c0 + 2 * KW + VW + GLA_GATE_LORA:o_g],
                           _pad_cols(w_in[:, c0 + 2 * KW + VW:c0 + 2 * KW + VW + GLA_GATE_LORA], LANE)],
                          axis=1).astype(BF16)
    w_g = w_in[:, o_g:].astype(BF16)

    proj_a = _matmul(xb, w_a, F32, 1024, 1024)
    qk = _matmul_rope(xb, w_qk, rope_tabs, 1024, 1024)
    tk = min(512, T)
    vt = _matmul_vt(w_vt, xb, tk)
    proj_c = _matmul(xb, w_c, F32, 512, GLA_IN_PAD)
    gates = _matmul(xb, w_g, F32, 1024, 1024)

    padr = lambda w: jnp.pad(w, ((0, LORA_PAD - w.shape[0]), (0, 0)))
    r2 = lambda v: v.reshape(1, -1)
    o_a = _rwkv(proj_a, mu_a, padr(p['rwkv_w2'][l]).astype(BF16), r2(p['rwkv_w0'][l]),
                padr(p['rwkv_a2'][l]).astype(BF16), r2(p['rwkv_a0'][l]), p['rwkv_g2'][l].astype(BF16),
                r2(p['rwkv_kk_scale'][l]), r2(p['rwkv_ka_scale'][l]), r2(p['rwkv_rk'][l]),
                r2(p['rwkv_gn_g'][l]), r2(p['rwkv_gn_b'][l]), 256)
    o_bb = _diff_attention(qk, vt, r2(p['diff_lq1'][l]), r2(p['diff_lk1'][l]), r2(p['diff_lq2'][l]),
                           r2(p['diff_lk2'][l]), p['diff_subln_g'][l].reshape(-1, 1), lambda_init, tk)
    g2p = jnp.pad(p['gla_g2'][l], ((0, LANE - GLA_GATE_LORA), (0, 0)))
    o_c = _gla(proj_c, g2p, r2(p['gla_gb'][l]), r2(p['gla_gn_g'][l]), 256)

    merged = _merge(o_a, o_bb, o_c, gates, p['proj_a'][l].astype(BF16), p['proj_b'][l].astype(BF16),
                    p['proj_c'][l].astype(BF16), 256)
    x1, x1b = _mm_ln(merged, p['w_o'][l].astype(BF16), x, r2(p['ln1_g'][l]), r2(p['ln1_b'][l]), 256)

    wr = _pad_cols(jnp.concatenate([p['router_group_w'][l], p['router_expert_w'][l]], axis=1), LANE)
    br = _pad_cols(jnp.concatenate([p['router_group_b'][l], p['router_expert_b'][l]])[None, :], LANE)
    comb = _router(x1, wr, br, 512)
    F = MOE_HIDDEN
    wg = p['moe_w_gate'][l].reshape(N_EXPERTS, D_MODEL, F).astype(BF16)
    wu = p['moe_w_up'][l].reshape(N_EXPERTS, D_MODEL, F).astype(BF16)
    wd = p['moe_w_down'][l].reshape(N_EXPERTS, F, D_MODEL).astype(BF16)
    return _moe(x1b, x1, comb, wg, wu, wd, r2(p['ln2_g'][l]), r2(p['ln2_b'][l]), 512)


def kernel(x, w_in, rwkv_mu, rwkv_w2, rwkv_w0, rwkv_a2, rwkv_a0, rwkv_g2, rwkv_kk_scale, rwkv_ka_scale, rwkv_rk, rwkv_gn_g, rwkv_gn_b, diff_lq1, diff_lk1, diff_lq2, diff_lk2, diff_subln_g, gla_g2, gla_gb, gla_gn_g, proj_a, proj_b, proj_c, w_o, ln1_g, ln1_b, router_group_w, router_group_b, router_expert_w, router_expert_b, moe_w_gate, moe_w_up, moe_w_down, ln2_g, ln2_b):
    p = dict(w_in=w_in, rwkv_mu=rwkv_mu, rwkv_w2=rwkv_w2, rwkv_w0=rwkv_w0, rwkv_a2=rwkv_a2, rwkv_a0=rwkv_a0,
             rwkv_g2=rwkv_g2, rwkv_kk_scale=rwkv_kk_scale, rwkv_ka_scale=rwkv_ka_scale, rwkv_rk=rwkv_rk,
             rwkv_gn_g=rwkv_gn_g, rwkv_gn_b=rwkv_gn_b, diff_lq1=diff_lq1, diff_lk1=diff_lk1, diff_lq2=diff_lq2,
             diff_lk2=diff_lk2, diff_subln_g=diff_subln_g, gla_g2=gla_g2, gla_gb=gla_gb, gla_gn_g=gla_gn_g,
             proj_a=proj_a, proj_b=proj_b, proj_c=proj_c, w_o=w_o, ln1_g=ln1_g, ln1_b=ln1_b,
             router_group_w=router_group_w, router_group_b=router_group_b, router_expert_w=router_expert_w,
             router_expert_b=router_expert_b, moe_w_gate=moe_w_gate, moe_w_up=moe_w_up, moe_w_down=moe_w_down,
             ln2_g=ln2_g, ln2_b=ln2_b)
    B, T, D = x.shape
    assert B == 1 and D == D_MODEL
    xf = x.reshape(T, D)
    xb = xf.astype(BF16)
    tabs = _rope_tables(T)
    for l in range(DEPTH):
        xf, xb = _layer(xf, xb, l, p, tabs)
    return xf.reshape(B, T, D)
```

```python
import functools
import math

import numpy as np
import jax
import jax.numpy as jnp
from jax import lax
from jax.experimental import pallas as pl
from jax.experimental.pallas import tpu as pltpu

F32 = jnp.float32
BF16 = jnp.bfloat16
HIGHEST = lax.Precision.HIGHEST

D_MODEL = 2048
DEPTH = 2
DEEPNORM_ALPHA = (2 * DEPTH) ** 0.25
LN_EPS = 1e-5

RWKV_HEADS = 8
RWKV_HEAD = 64
RWKV_WIDTH = RWKV_HEADS * RWKV_HEAD
RWKV_DECAY_LORA = 96
RWKV_AAA_LORA = 96
RWKV_GATE_LORA = 256
RWKV_GN_EPS = 64e-5
RWKV_IN = 3 * RWKV_WIDTH + RWKV_DECAY_LORA + RWKV_AAA_LORA + RWKV_GATE_LORA
RWKV_CHUNK = 64
LORA_PAD = 128
RWKV_IN_PAD = 3 * RWKV_WIDTH + 2 * LORA_PAD + RWKV_GATE_LORA

DIFF_HEADS = 8
DIFF_HEAD = 64
DIFF_QK = DIFF_HEADS * 2 * DIFF_HEAD
DIFF_V = DIFF_HEADS * 2 * DIFF_HEAD
DIFF_IN = 2 * DIFF_QK + DIFF_V
ROPE_THETA = 500000.0
ROPE_DIM = DIFF_HEAD // 4
ROPE_HALF = ROPE_DIM // 2

GLA_HEADS = 4
GLA_KEY = 64
GLA_VALUE = 128
GLA_K_WIDTH = GLA_HEADS * GLA_KEY
GLA_V_WIDTH = GLA_HEADS * GLA_VALUE
GLA_GATE_LORA = 16
GLA_TAU = 16.0
GLA_SUB = 16
GLA_IN = 2 * GLA_K_WIDTH + 2 * GLA_V_WIDTH + GLA_GATE_LORA
GLA_IN_PAD = 2 * GLA_K_WIDTH + 2 * GLA_V_WIDTH + 128

N_BRANCHES = 3
MOE_GROUPS = 4
MOE_EXPERTS = 8
MOE_HIDDEN = 256
N_EXPERTS = MOE_GROUPS * MOE_EXPERTS
LANE = 128
V7X_VMEM_LIMIT = 56 * 1024 * 1024


def _cparams(sem):
    return pltpu.CompilerParams(dimension_semantics=sem, vmem_limit_bytes=V7X_VMEM_LIMIT)


def _nt(a, b, **kw):
    return lax.dot_general(a, b, (((1,), (1,)), ((), ())), preferred_element_type=F32, **kw)


def _tn(a, b, **kw):
    return lax.dot_general(a, b, (((0,), (0,)), ((), ())), preferred_element_type=F32, **kw)


def _dot(a, b, **kw):
    return jnp.dot(a, b, preferred_element_type=F32, **kw)


def _mm_kernel(a_ref, b_ref, o_ref):
    o_ref[...] = _dot(a_ref[...], b_ref[...]).astype(o_ref.dtype)


def _matmul(a, b, out_dtype, tm, tn):
    M, K = a.shape
    N = b.shape[1]
    tm, tn = min(tm, M), min(tn, N)
    return pl.pallas_call(
        _mm_kernel,
        grid=(N // tn, M // tm),
        in_specs=[pl.BlockSpec((tm, K), lambda j, i: (i, 0)),
                  pl.BlockSpec((K, tn), lambda j, i: (0, j))],
        out_specs=pl.BlockSpec((tm, tn), lambda j, i: (i, j)),
        out_shape=jax.ShapeDtypeStruct((M, N), out_dtype),
        compiler_params=_cparams(("arbitrary", "arbitrary")),
        name="proj_mm",
    )(a, b)


def _mm_rope_kernel(a_ref, b_ref, c_ref, sa_ref, sb_ref, o_ref):
    acc = _dot(a_ref[...], b_ref[...])
    c, sa, sb = c_ref[...], sa_ref[...], sb_ref[...]
    for h in range(acc.shape[1] // LANE):
        blk = acc[:, h * LANE:(h + 1) * LANE]
        out = blk * c + pltpu.roll(blk, LANE - ROPE_HALF, 1) * sa + pltpu.roll(blk, ROPE_HALF, 1) * sb
        o_ref[:, h * LANE:(h + 1) * LANE] = out.astype(o_ref.dtype)


def _matmul_rope(a, b, tabs, tm, tn):
    M, K = a.shape
    N = b.shape[1]
    tm, tn = min(tm, M), min(tn, N)
    tab_spec = pl.BlockSpec((tm, LANE), lambda j, i: (i, 0))
    return pl.pallas_call(
        _mm_rope_kernel,
        grid=(N // tn, M // tm),
        in_specs=[pl.BlockSpec((tm, K), lambda j, i: (i, 0)),
                  pl.BlockSpec((K, tn), lambda j, i: (0, j)),
                  tab_spec, tab_spec, tab_spec],
        out_specs=pl.BlockSpec((tm, tn), lambda j, i: (i, j)),
        out_shape=jax.ShapeDtypeStruct((M, N), BF16),
        compiler_params=_cparams(("arbitrary", "arbitrary")),
        name="proj_qk_rope",
    )(a, b, *tabs)


def _mm_vt_kernel(wt_ref, x_ref, o_ref):
    o_ref[0] = _nt(wt_ref[...], x_ref[...]).astype(o_ref.dtype)


def _matmul_vt(wt, x, tk):
    N, K = wt.shape
    T = x.shape[0]
    return pl.pallas_call(
        _mm_vt_kernel,
        grid=(T // tk,),
        in_specs=[pl.BlockSpec((N, K), lambda i: (0, 0)),
                  pl.BlockSpec((tk, K), lambda i: (i, 0))],
        out_specs=pl.BlockSpec((1, N, tk), lambda i: (i, 0, 0)),
        out_shape=jax.ShapeDtypeStruct((T // tk, N, tk), BF16),
        compiler_params=_cparams(("arbitrary",)),
        name="proj_vt",
    )(wt, x)


def _rope_tables(T):
    pos = jnp.arange(T, dtype=F32)[:, None]
    inv_freq = ROPE_THETA ** (-jnp.arange(ROPE_HALF, dtype=F32) / ROPE_HALF)
    lane = np.arange(LANE) % DIFF_HEAD
    ang = pos * inv_freq[None, :]
    cos8, sin8 = jnp.cos(ang), jnp.sin(ang)
    idx = jnp.asarray(lane % ROPE_HALF)
    cos_l, sin_l = cos8[:, idx], sin8[:, idx]
    first = jnp.asarray(lane < ROPE_HALF)[None, :]
    second = jnp.asarray((lane >= ROPE_HALF) & (lane < ROPE_DIM))[None, :]
    c = jnp.where(first | second, cos_l, 1.0)
    sa = jnp.where(first, -sin_l, 0.0)
    sb = jnp.where(second, sin_l, 0.0)
    return c, sa, sb


def _rwkv_kernel(a_ref, mu_ref, w2_ref, w0_ref, a2_ref, a0_ref, g2_ref, kks_ref, kas_ref, rk_ref,
                 gng_ref, gnb_ref, seg_ref, tri_ref, o_ref,
                 prev_ref, h_ref, at_ref, rt_ref, bt_ref, kt_ref, bp_ref, kp_ref, v_ref, gc_ref, y_ref):
    TB = a_ref.shape[0]
    C, N, W = RWKV_CHUNK, RWKV_HEAD, RWKV_WIDTH

    @pl.when(pl.program_id(0) == 0)
    def _():
        prev_ref[...] = jnp.zeros_like(prev_ref)
        h_ref[...] = jnp.zeros_like(h_ref)

    a = a_ref[...]
    row = lax.broadcasted_iota(jnp.int32, a.shape, 0)
    shifted = jnp.where(row == 0, prev_ref[...], pltpu.roll(a, 1, 0))
    prev_ref[...] = a[TB - 1:TB, :]
    xs = a + (shifted - a) * mu_ref[...]

    r = xs[:, 0:W]
    k = xs[:, W:2 * W]
    v = xs[:, 2 * W:3 * W]
    w_lo = xs[:, 3 * W:3 * W + LORA_PAD]
    a_lo = xs[:, 3 * W + LORA_PAD:3 * W + 2 * LORA_PAD]
    g_lo = xs[:, 3 * W + 2 * LORA_PAD:]

    z = w0_ref[...] + _dot(jnp.tanh(w_lo).astype(BF16), w2_ref[...])
    w = -jax.nn.softplus(-z) - 0.5
    lw = -jnp.exp(w)
    aa = jax.nn.sigmoid(a0_ref[...] + _dot(a_lo.astype(BF16), a2_ref[...]))
    g = _dot(jax.nn.sigmoid(g_lo).astype(BF16), g2_ref[...])

    seg = seg_ref[...]
    kk = k * kks_ref[...]
    ss = _dot((kk * kk).astype(BF16), seg)
    kk = kk * lax.rsqrt(jnp.maximum(ss, 1e-24))
    km = k * (1.0 + (aa - 1.0) * kas_ref[...])
    beta = kk * aa

    tri = tri_ref[...]
    cum = _dot(tri, lw, precision=HIGHEST)
    rowc = lax.broadcasted_iota(jnp.int32, (TB, TB), 0) // C
    colc = lax.broadcasted_iota(jnp.int32, (TB, TB), 1) // C
    allc = jnp.where(rowc == colc, 1.0, 0.0).astype(F32)
    cum_c = _dot(allc, lw, precision=HIGHEST)

    e_in = jnp.exp(cum)
    e_neg = jnp.exp(-cum)
    e_end = jnp.exp(cum_c - cum)
    at_ref[...] = (-kk * jnp.exp(cum - lw)).astype(BF16)
    rt_ref[...] = (r * e_in).astype(BF16)
    bt_ref[...] = (beta * e_neg).astype(BF16)
    kt_ref[...] = (km * e_neg).astype(BF16)
    bp_ref[...] = (beta * e_end).astype(BF16)
    kp_ref[...] = (km * e_end).astype(BF16)
    v_ref[...] = v.astype(BF16)
    gc_ref[...] = jnp.exp(cum_c)

    ri = lax.broadcasted_iota(jnp.int32, (C, C), 0)
    ci = lax.broadcasted_iota(jnp.int32, (C, C), 1)
    strict = ci < ri
    incl = ci <= ri
    eye = ci == ri

    def chunk(c, carry):
        rows = pl.ds(pl.multiple_of(c * C, C), C)
        HS = range(RWKV_HEADS)
        sl = [slice(h * N, (h + 1) * N) for h in HS]
        cat = lambda a, b: jnp.concatenate([a, b], axis=0)
        at = [at_ref[rows, L] for L in sl]
        rt = [rt_ref[rows, L] for L in sl]
        bt = [bt_ref[rows, L] for L in sl]
        kt = [kt_ref[rows, L] for L in sl]
        bp = [bp_ref[rows, L] for L in sl]
        kp = [kp_ref[rows, L] for L in sl]
        vh = [v_ref[rows, L] for L in sl]
        ar = [cat(at[h], rt[h]) for h in HS]
        xb = [_nt(ar[h], bt[h]) for h in HS]
        xk = [_nt(ar[h], kt[h]) for h in HS]
        a_ab = [jnp.where(strict, xb[h][:C], 0.0) for h in HS]
        a_rb = [jnp.where(incl, xb[h][C:], 0.0).astype(BF16) for h in HS]
        a_ak = [jnp.where(strict, xk[h][:C], 0.0).astype(BF16) for h in HS]
        a_rk = [jnp.where(incl, xk[h][C:], 0.0).astype(BF16) for h in HS]
        ident = jnp.where(eye, 1.0, 0.0)
        t_inv = [ident + a_ab[h] for h in HS]
        ap = a_ab
        for _ in range(5):
            apb = [ap[h].astype(BF16) for h in HS]
            ap = [_dot(apb[h], apb[h]) for h in HS]
            t_inv = [t_inv[h] + _dot(t_inv[h].astype(BF16), ap[h].astype(BF16)) for h in HS]
        tb = [t_inv[h].astype(BF16) for h in HS]
        wv = [_dot(a_ak[h], vh[h]).astype(BF16) for h in HS]
        pb = [_dot(tb[h], at[h]).astype(BF16) for h in HS]
        u0b = [_dot(tb[h], wv[h]).astype(BF16) for h in HS]
        qm = [rt[h].astype(F32) + _dot(a_rb[h], pb[h]) for h in HS]
        y0 = [_dot(a_rb[h], u0b[h]) + _dot(a_rk[h], vh[h]) for h in HS]
        gm = [jnp.where(eye, gc_ref[rows, sl[h]][0:1, :], 0.0) + _tn(bp[h], pb[h]) for h in HS]
        hadd = [_tn(cat(bp[h], kp[h]), cat(u0b[h], vh[h])) for h in HS]
        for h in HS:
            z = _dot(cat(qm[h], gm[h]).astype(BF16), h_ref[h].astype(BF16))
            y_ref[rows, sl[h]] = z[:C] + y0[h]
            h_ref[h] = z[C:] + hadd[h]
        return carry

    lax.fori_loop(0, TB // C, chunk, 0)

    y = y_ref[...]
    segm = seg_ref[...]
    mu = _dot(y.astype(BF16), segm) * (1.0 / N)
    yc = y - mu
    var = _dot((yc * yc).astype(BF16), segm) * (1.0 / N)
    yn = yc * lax.rsqrt(var + RWKV_GN_EPS) * gng_ref[...] + gnb_ref[...]
    bonus = _dot((r * km * rk_ref[...]).astype(BF16), segm) * v
    o_ref[...] = ((yn + bonus) * g).astype(o_ref.dtype)


def _rwkv(a_proj, mu, w2, w0, a2, a0, g2, kks, kas, rk, gng, gnb, tb):
    T = a_proj.shape[0]
    tb = min(tb, T)
    W = RWKV_WIDTH
    hid = np.arange(W) // RWKV_HEAD
    seg = jnp.asarray(hid[:, None] == hid[None, :], dtype=BF16)
    t_idx = np.arange(tb)
    tri = jnp.asarray((t_idx[:, None] // RWKV_CHUNK == t_idx[None, :] // RWKV_CHUNK)
                      & (t_idx[None, :] <= t_idx[:, None]), dtype=F32)
    full = lambda shp: pl.BlockSpec(shp, lambda i: (0,) * len(shp))
    row = lambda n: full((1, n))
    bf = lambda: pltpu.VMEM((tb, W), BF16)
    return pl.pallas_call(
        _rwkv_kernel,
        grid=(T // tb,),
        in_specs=[pl.BlockSpec((tb, RWKV_IN_PAD), lambda i: (i, 0)),
                  row(RWKV_IN_PAD), full((LORA_PAD, W)), row(W), full((LORA_PAD, W)), row(W),
                  full((RWKV_GATE_LORA, W)), row(W), row(W), row(W), row(W), row(W),
                  full((W, W)), full((tb, tb))],
        out_specs=pl.BlockSpec((tb, W), lambda i: (i, 0)),
        out_shape=jax.ShapeDtypeStruct((T, W), BF16),
        scratch_shapes=[pltpu.VMEM((1, RWKV_IN_PAD), F32),
                        pltpu.VMEM((RWKV_HEADS, RWKV_HEAD, RWKV_HEAD), F32),
                        bf(), bf(), bf(), bf(), bf(), bf(), bf(),
                        pltpu.VMEM((tb, W), F32), pltpu.VMEM((tb, W), F32)],
        compiler_params=_cparams(("arbitrary",)),
        name="rwkv7",
    )(a_proj, mu, w2, w0, a2, a0, g2, kks, kas, rk, gng, gnb, seg, tri)


def _diff_kernel(q_ref, k_ref, vt_ref, lq1_ref, lk1_ref, lq2_ref, lk2_ref, sg_ref, o_ref,
                 m_ref, l_ref, acc_ref, s_ref, p_ref, a_ref, *, lambda_init):
    tq = q_ref.shape[0]
    nkb, _, tk = vt_ref.shape
    i = pl.program_id(1)
    q = q_ref[...]
    lane = lax.broadcasted_iota(jnp.int32, q.shape, 1)
    qm = (jnp.where(lane < DIFF_HEAD, q, jnp.zeros_like(q)),
          jnp.where(lane >= DIFF_HEAD, q, jnp.zeros_like(q)))
    m_ref[...] = jnp.full(m_ref.shape, -jnp.inf, F32)
    l_ref[...] = jnp.zeros_like(l_ref)
    acc_ref[...] = jnp.zeros_like(acc_ref)
    p_ref[1] = jnp.zeros(p_ref.shape[1:], BF16)
    a_ref[1] = jnp.ones(a_ref.shape[1:], F32)

    def scores(j, slot):
        jc = jnp.minimum(j, nkb - 1)
        kj = k_ref[pl.ds(pl.multiple_of(jc * tk, tk), tk), :]
        for mi in range(2):
            s_ref[slot, mi] = _nt(kj, qm[mi])

    def values(j, slot):
        vj = vt_ref[jnp.clip(j, 0, nkb - 1)]
        for mi in range(2):
            acc_ref[mi] = a_ref[slot, mi] * acc_ref[mi] + _dot(vj, p_ref[slot, mi])

    def softmax(j, slot, masked):
        for mi in range(2):
            s = s_ref[slot, mi]
            if masked:
                d = (lax.broadcasted_iota(jnp.int32, (tk, tq), 0)
                     - lax.broadcasted_iota(jnp.int32, (tk, tq), 1))
                s = jnp.where(d <= i * tq - j * tk, s, -jnp.inf)
            m_old = m_ref[mi]
            m_new = jnp.maximum(m_old, jnp.max(s, axis=0, keepdims=True))
            alpha = jnp.exp2(m_old - m_new)
            p = jnp.exp2(s - m_new)
            l_ref[mi] = alpha * l_ref[mi] + jnp.sum(p, axis=0, keepdims=True)
            m_ref[mi] = m_new
            a_ref[slot, mi] = alpha
            p_ref[slot, mi] = p.astype(BF16)

    def half(j, cur, masked):
        nxt = 1 - cur
        scores(j + 1, nxt)
        values(j - 1, nxt)
        softmax(j, cur, masked)

    def pair(jj, carry):
        half(2 * jj, 0, False)
        half(2 * jj + 1, 1, False)
        return carry

    scores(0, 0)
    n_pairs = (i + 2) // 2
    lax.fori_loop(0, n_pairs - 1, pair, 0)
    j_last = 2 * (n_pairs - 1)
    half(j_last, 0, True)
    half(j_last + 1, 1, True)
    values(j_last + 1, 1)

    lam = (jnp.exp(jnp.sum(lq1_ref[...] * lk1_ref[...])) - jnp.exp(jnp.sum(lq2_ref[...] * lk2_ref[...]))
           + lambda_init)
    o = acc_ref[0] / l_ref[0] - lam * (acc_ref[1] / l_ref[1])
    o = o * lax.rsqrt(jnp.mean(o * o, axis=0, keepdims=True) + LN_EPS) * (1.0 - lambda_init)
    o = o * sg_ref[...]
    o_ref[...] = o.T.astype(o_ref.dtype)


def _diff_attention(qk, vt, lq1, lk1, lq2, lk2, subln_g, lambda_init, tq):
    T = qk.shape[0]
    tk = vt.shape[2]
    assert tq == tk
    H, E = DIFF_HEADS, 2 * DIFF_HEAD
    vec = lambda: pl.BlockSpec((1, DIFF_HEAD), lambda h, i: (0, 0))
    return pl.pallas_call(
        functools.partial(_diff_kernel, lambda_init=lambda_init),
        grid=(H, T // tq),
        in_specs=[pl.BlockSpec((tq, E), lambda h, i: (i, h)),
                  pl.BlockSpec((T, E), lambda h, i: (0, H + h)),
                  pl.BlockSpec((T // tk, E, tk), lambda h, i: (0, h, 0)),
                  vec(), vec(), vec(), vec(),
                  pl.BlockSpec((E, 1), lambda h, i: (0, 0))],
        out_specs=pl.BlockSpec((tq, E), lambda h, i: (i, h)),
        out_shape=jax.ShapeDtypeStruct((T, DIFF_V), BF16),
        scratch_shapes=[pltpu.VMEM((2, 1, tq), F32), pltpu.VMEM((2, 1, tq), F32),
                        pltpu.VMEM((2, E, tq), F32), pltpu.VMEM((2, 2, tk, tq), F32),
                        pltpu.VMEM((2, 2, tk, tq), BF16), pltpu.VMEM((2, 2, 1, tq), F32)],
        compiler_params=_cparams(("arbitrary", "arbitrary")),
        name="diff_attn",
    )(qk, qk, vt, lq1, lk1, lq2, lk2, subln_g)


def _gla_kernel(c_ref, g2_ref, gb_ref, gng_ref, tri_ref, segv_ref, o_ref,
                st_ref, q_ref, k_ref, b_ref, oacc_ref):
    TB = c_ref.shape[0]
    KW, VW, S = GLA_K_WIDTH, GLA_V_WIDTH, GLA_SUB
    DK, DV = GLA_KEY, GLA_VALUE

    @pl.when(pl.program_id(0) == 0)
    def _():
        st_ref[...] = jnp.zeros_like(st_ref)

    g_lo = c_ref[:, 2 * KW + 2 * VW:]
    gate = _dot(g_lo, g2_ref[...], precision=HIGHEST) + gb_ref[...]
    log_a = jax.nn.log_sigmoid(gate) * (1.0 / GLA_TAU)
    b_ref[...] = _dot(tri_ref[...], log_a, precision=HIGHEST)
    q_ref[...] = c_ref[:, 0:KW]
    k_ref[...] = c_ref[:, KW:2 * KW]

    ri = lax.broadcasted_iota(jnp.int32, (S, 1), 0)
    segv = segv_ref[...]

    def sub(s, carry):
        rows = pl.ds(pl.multiple_of(s * S, S), S)
        q = q_ref[rows, :]
        k = k_ref[rows, :]
        b = b_ref[rows, :]
        v = c_ref[rows, 2 * KW:2 * KW + VW]
        b_last = b[S - 1:S, :]
        qb = (q * jnp.exp(b)).astype(BF16)
        kb = (k * jnp.exp(b_last - b)).astype(BF16)
        dec = jnp.exp(b_last)
        vb = v.astype(BF16)
        acc = jnp.zeros((S, VW), F32)
        for j in range(S):
            e = jnp.exp(jnp.minimum(b - b[j:j + 1, :], 0.0))
            t = (q * e * k[j:j + 1, :]).astype(BF16)
            att = _dot(t, segv)
            acc = acc + jnp.where(ri >= j, att, 0.0) * v[j:j + 1, :]
        for h in range(GLA_HEADS):
            lk = slice(h * DK, (h + 1) * DK)
            lv = slice(h * DV, (h + 1) * DV)
            st = st_ref[h]
            inter = _nt(qb[:, lk], st.astype(BF16))
            oacc_ref[rows, lv] = acc[:, lv] + inter
            st_ref[h] = st * dec[:, lk] + _tn(vb[:, lv], kb[:, lk])
        return carry

    lax.fori_loop(0, TB // S, sub, 0)

    o = oacc_ref[...]
    rgate = c_ref[:, 2 * KW + VW:2 * KW + 2 * VW]
    for h in range(GLA_HEADS):
        lv = slice(h * DV, (h + 1) * DV)
        oh = o[:, lv]
        oh = oh * lax.rsqrt(jnp.mean(oh * oh, axis=1, keepdims=True) + LN_EPS) * gng_ref[:, lv]
        o_ref[:, lv] = (oh * jax.nn.silu(rgate[:, lv])).astype(o_ref.dtype)


def _gla(c_proj, g2, gb, gng, tb):
    T = c_proj.shape[0]
    tb = min(tb, T)
    KW, VW = GLA_K_WIDTH, GLA_V_WIDTH
    t_idx = np.arange(tb)
    tri = jnp.asarray((t_idx[:, None] // GLA_SUB == t_idx[None, :] // GLA_SUB)
                      & (t_idx[None, :] <= t_idx[:, None]), dtype=F32)
    segv = jnp.asarray((np.arange(KW) // GLA_KEY)[:, None] == (np.arange(VW) // GLA_VALUE)[None, :], dtype=BF16)
    full = lambda shp: pl.BlockSpec(shp, lambda i: (0,) * len(shp))
    return pl.pallas_call(
        _gla_kernel,
        grid=(T // tb,),
        in_specs=[pl.BlockSpec((tb, GLA_IN_PAD), lambda i: (i, 0)),
                  full((LANE, KW)), full((1, KW)), full((1, VW)), full((tb, tb)), full((KW, VW))],
        out_specs=pl.BlockSpec((tb, VW), lambda i: (i, 0)),
        out_shape=jax.ShapeDtypeStruct((T, VW), BF16),
        scratch_shapes=[pltpu.VMEM((GLA_HEADS, GLA_VALUE, GLA_KEY), F32),
                        pltpu.VMEM((tb, KW), F32), pltpu.VMEM((tb, KW), F32), pltpu.VMEM((tb, KW), F32),
                        pltpu.VMEM((tb, VW), F32)],
        compiler_params=_cparams(("arbitrary",)),
        name="gla",
    )(c_proj, g2, gb, gng, tri, segv)


def _merge_kernel(oa_ref, ob_ref, oc_ref, g_ref, pa_ref, pb_ref, pc_ref, o_ref):
    D = D_MODEL
    m = jax.nn.sigmoid(g_ref[:, 0:D]) * _dot(oa_ref[...], pa_ref[...])
    m = m + jax.nn.sigmoid(g_ref[:, D:2 * D]) * _dot(ob_ref[...], pb_ref[...])
    m = m + jax.nn.sigmoid(g_ref[:, 2 * D:3 * D]) * _dot(oc_ref[...], pc_ref[...])
    o_ref[...] = m.astype(o_ref.dtype)


def _merge(oa, ob, oc, gates, pa, pb, pc, tm):
    T = oa.shape[0]
    tm = min(tm, T)
    D = D_MODEL
    rowblk = lambda n: pl.BlockSpec((tm, n), lambda i: (i, 0))
    full = lambda shp: pl.BlockSpec(shp, lambda i: (0, 0))
    return pl.pallas_call(
        _merge_kernel,
        grid=(T // tm,),
        in_specs=[rowblk(RWKV_WIDTH), rowblk(DIFF_V), rowblk(GLA_V_WIDTH), rowblk(3 * D),
                  full((RWKV_WIDTH, D)), full((DIFF_V, D)), full((GLA_V_WIDTH, D))],
        out_specs=rowblk(D),
        out_shape=jax.ShapeDtypeStruct((T, D), BF16),
        compiler_params=_cparams(("arbitrary",)),
        name="merge",
    )(oa, ob, oc, gates, pa, pb, pc)


def _layer_norm(z, g, b):
    mu = jnp.mean(z, axis=1, keepdims=True)
    zc = z - mu
    var = jnp.mean(zc * zc, axis=1, keepdims=True)
    return zc * lax.rsqrt(var + LN_EPS) * g + b


def _mm_ln_kernel(a_ref, w_ref, res_ref, g_ref, b_ref, o_ref, ob_ref):
    z = DEEPNORM_ALPHA * res_ref[...] + _dot(a_ref[...], w_ref[...])
    out = _layer_norm(z, g_ref[...], b_ref[...])
    o_ref[...] = out
    ob_ref[...] = out.astype(BF16)


def _mm_ln(a, w, res, g, b, tm):
    T, K = a.shape
    D = D_MODEL
    tm = min(tm, T)
    rowblk = lambda n: pl.BlockSpec((tm, n), lambda i: (i, 0))
    full = lambda shp: pl.BlockSpec(shp, lambda i: (0, 0))
    return pl.pallas_call(
        _mm_ln_kernel,
        grid=(T // tm,),
        in_specs=[rowblk(K), full((K, D)), rowblk(D), full((1, D)), full((1, D))],
        out_specs=[rowblk(D), rowblk(D)],
        out_shape=[jax.ShapeDtypeStruct((T, D), F32), jax.ShapeDtypeStruct((T, D), BF16)],
        compiler_params=_cparams(("arbitrary",)),
        name="wo_ln",
    )(a, w, res, g, b)


def _router_kernel(x_ref, w_ref, b_ref, o_ref):
    G, E = MOE_GROUPS, MOE_EXPERTS
    logits = _dot(x_ref[...], w_ref[...], precision=HIGHEST) + b_ref[...]
    lane = lax.broadcasted_iota(jnp.int32, logits.shape, 1)
    neg = -jnp.inf
    big = jnp.int32(1 << 20)
    is_g = lane < G
    lg = jnp.where(is_g, logits, neg)
    gmax = jnp.max(lg, axis=1, keepdims=True)
    gidx = jnp.min(jnp.where(is_g & (lg == gmax), lane, big), axis=1, keepdims=True)
    g_p = 1.0 / jnp.sum(jnp.exp(lg - gmax), axis=1, keepdims=True)
    lo = G + gidx * E
    in_grp = (lane >= lo) & (lane < lo + E)
    le = jnp.where(in_grp, logits, neg)
    m1 = jnp.max(le, axis=1, keepdims=True)
    i1 = jnp.min(jnp.where(in_grp & (le == m1), lane, big), axis=1, keepdims=True)
    le2 = jnp.where(lane == i1, neg, le)
    m2 = jnp.max(le2, axis=1, keepdims=True)
    i2 = jnp.min(jnp.where(in_grp & (le2 == m2), lane, big), axis=1, keepdims=True)
    e2 = jnp.exp(m2 - m1)
    p1 = 1.0 / (1.0 + e2)
    p2 = e2 / (1.0 + e2)
    comb = jnp.where(lane == i1, g_p * p1, 0.0) + jnp.where(lane == i2, g_p * p2, 0.0)
    o_ref[...] = comb


def _router(x1, wr, br, tm):
    T, D = x1.shape
    tm = min(tm, T)
    return pl.pallas_call(
        _router_kernel,
        grid=(T // tm,),
        in_specs=[pl.BlockSpec((tm, D), lambda i: (i, 0)),
                  pl.BlockSpec((D, LANE), lambda i: (0, 0)),
                  pl.BlockSpec((1, LANE), lambda i: (0, 0))],
        out_specs=pl.BlockSpec((tm, LANE), lambda i: (i, 0)),
        out_shape=jax.ShapeDtypeStruct((T, LANE), F32),
        compiler_params=_cparams(("arbitrary",)),
        name="router",
    )(x1, wr, br)


def _moe_kernel(xb_ref, x_ref, comb_ref, wg_ref, wu_ref, wd_ref, g_ref, b_ref, o_ref, ob_ref, acc_ref):
    e = pl.program_id(1)

    @pl.when(e == 0)
    def _():
        acc_ref[...] = jnp.zeros_like(acc_ref)

    xb = xb_ref[...]
    hg = _dot(xb, wg_ref[0])
    hu = _dot(xb, wu_ref[0])
    comb = comb_ref[...]
    lane = lax.broadcasted_iota(jnp.int32, comb.shape, 1)
    cw = jnp.sum(jnp.where(lane == e + MOE_GROUPS, comb, 0.0), axis=1, keepdims=True)
    h = (jax.nn.silu(hg) * hu * cw).astype(BF16)
    acc_ref[...] += _dot(h, wd_ref[0])

    @pl.when(e == N_EXPERTS - 1)
    def _():
        z = DEEPNORM_ALPHA * x_ref[...] + acc_ref[...]
        out = _layer_norm(z, g_ref[...], b_ref[...])
        o_ref[...] = out
        ob_ref[...] = out.astype(BF16)


def _moe(xb, x, comb, wg, wu, wd, g, b, tm):
    T, D = x.shape
    F = MOE_HIDDEN
    tm = min(tm, T)
    rowblk = lambda n: pl.BlockSpec((tm, n), lambda i, e: (i, 0))
    return pl.pallas_call(
        _moe_kernel,
        grid=(T // tm, N_EXPERTS),
        in_specs=[rowblk(D), rowblk(D), rowblk(LANE),
                  pl.BlockSpec((1, D, F), lambda i, e: (e, 0, 0)),
                  pl.BlockSpec((1, D, F), lambda i, e: (e, 0, 0)),
                  pl.BlockSpec((1, F, D), lambda i, e: (e, 0, 0)),
                  pl.BlockSpec((1, D), lambda i, e: (0, 0)),
                  pl.BlockSpec((1, D), lambda i, e: (0, 0))],
        out_specs=[rowblk(D), rowblk(D)],
        out_shape=[jax.ShapeDtypeStruct((T, D), F32), jax.ShapeDtypeStruct((T, D), BF16)],
        scratch_shapes=[pltpu.VMEM((tm, D), F32)],
        compiler_params=_cparams(("arbitrary", "arbitrary")),
        name="moe_experts",
    )(xb, x, comb, wg, wu, wd, g, b)


def _pad_cols(w, n):
    return jnp.pad(w, ((0, 0), (0, n - w.shape[1])))


def _layer(x, xb, l, p, rope_tabs):
    T = x.shape[0]
    W = RWKV_WIDTH
    KW, VW = GLA_K_WIDTH, GLA_V_WIDTH
    lambda_init = 0.8 - 0.6 * math.exp(-0.3 * l)

    w_in = p['w_in'][l]
    o_b = RWKV_IN
    o_c = o_b + DIFF_IN
    o_g = o_c + GLA_IN
    lo1 = 3 * W + RWKV_DECAY_LORA
    lo2 = lo1 + RWKV_AAA_LORA
    w_a = jnp.concatenate([_pad_cols(w_in[:, :lo1], 3 * W + LORA_PAD), _pad_cols(w_in[:, lo1:lo2], LORA_PAD),
                           w_in[:, lo2:o_b]], axis=1).astype(BF16)
    mu = p['rwkv_mu'][l]
    mu_a = jnp.concatenate([jnp.pad(mu[:lo1], (0, LORA_PAD - RWKV_DECAY_LORA)),
                            jnp.pad(mu[lo1:lo2], (0, LORA_PAD - RWKV_AAA_LORA)), mu[lo2:]])[None, :]
    qscale = DIFF_HEAD ** -0.5 * math.log2(math.e)
    w_qk = jnp.concatenate([w_in[:, o_b:o_b + DIFF_QK] * qscale, w_in[:, o_b + DIFF_QK:o_b + 2 * DIFF_QK]],
                           axis=1).astype(BF16)
    w_vt = w_in[:, o_b + 2 * DIFF_QK:o_c].T.astype(BF16)
    c0 = o_c
    w_c = jnp.concatenate([w_in[:, c0:c0 + KW] * (GLA_KEY ** -0.5), w_in[:, c0 + KW:c0 + 2 * KW + VW],
                           w_in[:, c0 + 2 * KW + VW + GLA_GATE_LORA:o_g],
                           _pad_cols(w_in[:, c0 + 2 * KW + VW:c0 + 2 * KW + VW + GLA_GATE_LORA], LANE)],
                          axis=1).astype(BF16)
    w_g = w_in[:, o_g:].astype(BF16)

    proj_a = _matmul(xb, w_a, F32, 1024, 1024)
    qk = _matmul_rope(xb, w_qk, rope_tabs, 1024, 1024)
    tk = min(512, T)
    vt = _matmul_vt(w_vt, xb, tk)
    proj_c = _matmul(xb, w_c, F32, 512, GLA_IN_PAD)
    gates = _matmul(xb, w_g, F32, 1024, 1024)

    padr = lambda w: jnp.pad(w, ((0, LORA_PAD - w.shape[0]), (0, 0)))
    r2 = lambda v: v.reshape(1, -1)
    o_a = _rwkv(proj_a, mu_a, padr(p['rwkv_w2'][l]).astype(BF16), r2(p['rwkv_w0'][l]),
                padr(p['rwkv_a2'][l]).astype(BF16), r2(p['rwkv_a0'][l]), p['rwkv_g2'][l].astype(BF16),
                r2(p['rwkv_kk_scale'][l]), r2(p['rwkv_ka_scale'][l]), r2(p['rwkv_rk'][l]),
                r2(p['rwkv_gn_g'][l]), r2(p['rwkv_gn_b'][l]), 256)
    o_bb = _diff_attention(qk, vt, r2(p['diff_lq1'][l]), r2(p['diff_lk1'][l]), r2(p['diff_lq2'][l]),
                           r2(p['diff_lk2'][l]), p['diff_subln_g'][l].reshape(-1, 1), lambda_init, tk)
    g2p = jnp.pad(p['gla_g2'][l], ((0, LANE - GLA_GATE_LORA), (0, 0)))
    o_c = _gla(proj_c, g2p, r2(p['gla_gb'][l]), r2(p['gla_gn_g'][l]), 256)

    merged = _merge(o_a, o_bb, o_c, gates, p['proj_a'][l].astype(BF16), p['proj_b'][l].astype(BF16),
                    p['proj_c'][l].astype(BF16), 256)
    x1, x1b = _mm_ln(merged, p['w_o'][l].astype(BF16), x, r2(p['ln1_g'][l]), r2(p['ln1_b'][l]), 256)

    wr = _pad_cols(jnp.concatenate([p['router_group_w'][l], p['router_expert_w'][l]], axis=1), LANE)
    br = _pad_cols(jnp.concatenate([p['router_group_b'][l], p['router_expert_b'][l]])[None, :], LANE)
    comb = _router(x1, wr, br, 512)
    F = MOE_HIDDEN
    wg = p['moe_w_gate'][l].reshape(N_EXPERTS, D_MODEL, F).astype(BF16)
    wu = p['moe_w_up'][l].reshape(N_EXPERTS, D_MODEL, F).astype(BF16)
    wd = p['moe_w_down'][l].reshape(N_EXPERTS, F, D_MODEL).astype(BF16)
    return _moe(x1b, x1, comb, wg, wu, wd, r2(p['ln2_g'][l]), r2(p['ln2_b'][l]), 512)


def kernel(x, w_in, rwkv_mu, rwkv_w2, rwkv_w0, rwkv_a2, rwkv_a0, rwkv_g2, rwkv_kk_scale, rwkv_ka_scale, rwkv_rk, rwkv_gn_g, rwkv_gn_b, diff_lq1, diff_lk1, diff_lq2, diff_lk2, diff_subln_g, gla_g2, gla_gb, gla_gn_g, proj_a, proj_b, proj_c, w_o, ln1_g, ln1_b, router_group_w, router_group_b, router_expert_w, router_expert_b, moe_w_gate, moe_w_up, moe_w_down, ln2_g, ln2_b):
    p = dict(w_in=w_in, rwkv_mu=rwkv_mu, rwkv_w2=rwkv_w2, rwkv_w0=rwkv_w0, rwkv_a2=rwkv_a2, rwkv_a0=rwkv_a0,
             rwkv_g2=rwkv_g2, rwkv_kk_scale=rwkv_kk_scale, rwkv_ka_scale=rwkv_ka_scale, rwkv_rk=rwkv_rk,
             rwkv_gn_g=rwkv_gn_g, rwkv_gn_b=rwkv_gn_b, diff_lq1=diff_lq1, diff_lk1=diff_lk1, diff_lq2=diff_lq2,
             diff_lk2=diff_lk2, diff_subln_g=diff_subln_g, gla_g2=gla_g2, gla_gb=gla_gb, gla_gn_g=gla_gn_g,
             proj_a=proj_a, proj_b=proj_b, proj_c=proj_c, w_o=w_o, ln1_g=ln1_g, ln1_b=ln1_b,
             router_group_w=router_group_w, router_group_b=router_group_b, router_expert_w=router_expert_w,
             router_expert_b=router_expert_b, moe_w_gate=moe_w_gate, moe_w_up=moe_w_up, moe_w_down=moe_w_down,
             ln2_g=ln2_g, ln2_b=ln2_b)
    B, T, D = x.shape
    assert B == 1 and D == D_MODEL
    xf = x.reshape(T, D)
    xb = xf.astype(BF16)
    tabs = _rope_tables(T)
    for l in range(DEPTH):
        xf, xb = _layer(xf, xb, l, p, tabs)
    return xf.reshape(B, T, D)
```

```python
import functools
import math

import numpy as np
import jax
import jax.numpy as jnp
from jax import lax
from jax.experimental import pallas as pl
from jax.experimental.pallas import tpu as pltpu

F32 = jnp.float32
BF16 = jnp.bfloat16
HIGHEST = lax.Precision.HIGHEST

D_MODEL = 2048
DEPTH = 2
DEEPNORM_ALPHA = (2 * DEPTH) ** 0.25
LN_EPS = 1e-5

RWKV_HEADS = 8
RWKV_HEAD = 64
RWKV_WIDTH = RWKV_HEADS * RWKV_HEAD
RWKV_DECAY_LORA = 96
RWKV_AAA_LORA = 96
RWKV_GATE_LORA = 256
RWKV_GN_EPS = 64e-5
RWKV_IN = 3 * RWKV_WIDTH + RWKV_DECAY_LORA + RWKV_AAA_LORA + RWKV_GATE_LORA
RWKV_CHUNK = 64
LORA_PAD = 128
RWKV_IN_PAD = 3 * RWKV_WIDTH + 2 * LORA_PAD + RWKV_GATE_LORA

DIFF_HEADS = 8
DIFF_HEAD = 64
DIFF_QK = DIFF_HEADS * 2 * DIFF_HEAD
DIFF_V = DIFF_HEADS * 2 * DIFF_HEAD
DIFF_IN = 2 * DIFF_QK + DIFF_V
ROPE_THETA = 500000.0
ROPE_DIM = DIFF_HEAD // 4
ROPE_HALF = ROPE_DIM // 2

GLA_HEADS = 4
GLA_KEY = 64
GLA_VALUE = 128
GLA_K_WIDTH = GLA_HEADS * GLA_KEY
GLA_V_WIDTH = GLA_HEADS * GLA_VALUE
GLA_GATE_LORA = 16
GLA_TAU = 16.0
GLA_SUB = 16
GLA_IN = 2 * GLA_K_WIDTH + 2 * GLA_V_WIDTH + GLA_GATE_LORA
GLA_IN_PAD = 2 * GLA_K_WIDTH + 2 * GLA_V_WIDTH + 128

N_BRANCHES = 3
MOE_GROUPS = 4
MOE_EXPERTS = 8
MOE_HIDDEN = 256
N_EXPERTS = MOE_GROUPS * MOE_EXPERTS
LANE = 128
V7X_VMEM_LIMIT = 56 * 1024 * 1024


def _cparams(sem):
    return pltpu.CompilerParams(dimension_semantics=sem, vmem_limit_bytes=V7X_VMEM_LIMIT)


def _nt(a, b, **kw):
    return lax.dot_general(a, b, (((1,), (1,)), ((), ())), preferred_element_type=F32, **kw)


def _tn(a, b, **kw):
    return lax.dot_general(a, b, (((0,), (0,)), ((), ())), preferred_element_type=F32, **kw)


def _dot(a, b, **kw):
    return jnp.dot(a, b, preferred_element_type=F32, **kw)


def _mm_kernel(a_ref, b_ref, o_ref):
    o_ref[...] = _dot(a_ref[...], b_ref[...]).astype(o_ref.dtype)


def _matmul(a, b, out_dtype, tm, tn):
    M, K = a.shape
    N = b.shape[1]
    tm, tn = min(tm, M), min(tn, N)
    return pl.pallas_call(
        _mm_kernel,
        grid=(N // tn, M // tm),
        in_specs=[pl.BlockSpec((tm, K), lambda j, i: (i, 0)),
                  pl.BlockSpec((K, tn), lambda j, i: (0, j))],
        out_specs=pl.BlockSpec((tm, tn), lambda j, i: (i, j)),
        out_shape=jax.ShapeDtypeStruct((M, N), out_dtype),
        compiler_params=_cparams(("arbitrary", "arbitrary")),
        name="proj_mm",
    )(a, b)


def _mm_rope_kernel(a_ref, b_ref, c_ref, sa_ref, sb_ref, o_ref):
    acc = _dot(a_ref[...], b_ref[...])
    c, sa, sb = c_ref[...], sa_ref[...], sb_ref[...]
    for h in range(acc.shape[1] // LANE):
        blk = acc[:, h * LANE:(h + 1) * LANE]
        out = blk * c + pltpu.roll(blk, LANE - ROPE_HALF, 1) * sa + pltpu.roll(blk, ROPE_HALF, 1) * sb
        o_ref[:, h * LANE:(h + 1) * LANE] = out.astype(o_ref.dtype)


def _matmul_rope(a, b, tabs, tm, tn):
    M, K = a.shape
    N = b.shape[1]
    tm, tn = min(tm, M), min(tn, N)
    tab_spec = pl.BlockSpec((tm, LANE), lambda j, i: (i, 0))
    return pl.pallas_call(
        _mm_rope_kernel,
        grid=(N // tn, M // tm),
        in_specs=[pl.BlockSpec((tm, K), lambda j, i: (i, 0)),
                  pl.BlockSpec((K, tn), lambda j, i: (0, j)),
                  tab_spec, tab_spec, tab_spec],
        out_specs=pl.BlockSpec((tm, tn), lambda j, i: (i, j)),
        out_shape=jax.ShapeDtypeStruct((M, N), BF16),
        compiler_params=_cparams(("arbitrary", "arbitrary")),
        name="proj_qk_rope",
    )(a, b, *tabs)


def _mm_vt_kernel(wt_ref, x_ref, o_ref):
    o_ref[0] = _nt(wt_ref[...], x_ref[...]).astype(o_ref.dtype)


def _matmul_vt(wt, x, tk):
    N, K = wt.shape
    T = x.shape[0]
    return pl.pallas_call(
        _mm_vt_kernel,
        grid=(T // tk,),
        in_specs=[pl.BlockSpec((N, K), lambda i: (0, 0)),
                  pl.BlockSpec((tk, K), lambda i: (i, 0))],
        out_specs=pl.BlockSpec((1, N, tk), lambda i: (i, 0, 0)),
        out_shape=jax.ShapeDtypeStruct((T // tk, N, tk), BF16),
        compiler_params=_cparams(("arbitrary",)),
        name="proj_vt",
    )(wt, x)


def _rope_tables(T):
    pos = jnp.arange(T, dtype=F32)[:, None]
    inv_freq = ROPE_THETA ** (-jnp.arange(ROPE_HALF, dtype=F32) / ROPE_HALF)
    lane = np.arange(LANE) % DIFF_HEAD
    ang = pos * inv_freq[None, :]
    cos8, sin8 = jnp.cos(ang), jnp.sin(ang)
    idx = jnp.asarray(lane % ROPE_HALF)
    cos_l, sin_l = cos8[:, idx], sin8[:, idx]
    first = jnp.asarray(lane < ROPE_HALF)[None, :]
    second = jnp.asarray((lane >= ROPE_HALF) & (lane < ROPE_DIM))[None, :]
    c = jnp.where(first | second, cos_l, 1.0)
    sa = jnp.where(first, -sin_l, 0.0)
    sb = jnp.where(second, sin_l, 0.0)
    return c, sa, sb


def _rwkv_kernel(a_ref, mu_ref, w2_ref, w0_ref, a2_ref, a0_ref, g2_ref, kks_ref, kas_ref, rk_ref,
                 gng_ref, gnb_ref, seg_ref, tri_ref, o_ref,
                 prev_ref, h_ref, at_ref, rt_ref, bt_ref, kt_ref, bp_ref, kp_ref, v_ref, gc_ref, y_ref):
    TB = a_ref.shape[0]
    C, N, W = RWKV_CHUNK, RWKV_HEAD, RWKV_WIDTH

    @pl.when(pl.program_id(0) == 0)
    def _():
        prev_ref[...] = jnp.zeros_like(prev_ref)
        h_ref[...] = jnp.zeros_like(h_ref)

    a = a_ref[...]
    row = lax.broadcasted_iota(jnp.int32, a.shape, 0)
    shifted = jnp.where(row == 0, prev_ref[...], pltpu.roll(a, 1, 0))
    prev_ref[...] = a[TB - 1:TB, :]
    xs = a + (shifted - a) * mu_ref[...]

    r = xs[:, 0:W]
    k = xs[:, W:2 * W]
    v = xs[:, 2 * W:3 * W]
    w_lo = xs[:, 3 * W:3 * W + LORA_PAD]
    a_lo = xs[:, 3 * W + LORA_PAD:3 * W + 2 * LORA_PAD]
    g_lo = xs[:, 3 * W + 2 * LORA_PAD:]

    z = w0_ref[...] + _dot(jnp.tanh(w_lo).astype(BF16), w2_ref[...])
    w = -jax.nn.softplus(-z) - 0.5
    lw = -jnp.exp(w)
    aa = jax.nn.sigmoid(a0_ref[...] + _dot(a_lo.astype(BF16), a2_ref[...]))
    g = _dot(jax.nn.sigmoid(g_lo).astype(BF16), g2_ref[...])

    seg = seg_ref[...]
    kk = k * kks_ref[...]
    ss = _dot((kk * kk).astype(BF16), seg)
    kk = kk * lax.rsqrt(jnp.maximum(ss, 1e-24))
    km = k * (1.0 + (aa - 1.0) * kas_ref[...])
    beta = kk * aa

    tri = tri_ref[...]
    cum = _dot(tri, lw, precision=HIGHEST)
    rowc = lax.broadcasted_iota(jnp.int32, (TB, TB), 0) // C
    colc = lax.broadcasted_iota(jnp.int32, (TB, TB), 1) // C
    allc = jnp.where(rowc == colc, 1.0, 0.0).astype(F32)
    cum_c = _dot(allc, lw, precision=HIGHEST)

    e_in = jnp.exp(cum)
    e_neg = jnp.exp(-cum)
    e_end = jnp.exp(cum_c - cum)
    at_ref[...] = (-kk * jnp.exp(cum - lw)).astype(BF16)
    rt_ref[...] = (r * e_in).astype(BF16)
    bt_ref[...] = (beta * e_neg).astype(BF16)
    kt_ref[...] = (km * e_neg).astype(BF16)
    bp_ref[...] = (beta * e_end).astype(BF16)
    kp_ref[...] = (km * e_end).astype(BF16)
    v_ref[...] = v.astype(BF16)
    gc_ref[...] = jnp.exp(cum_c)

    ri = lax.broadcasted_iota(jnp.int32, (C, C), 0)
    ci = lax.broadcasted_iota(jnp.int32, (C, C), 1)
    strict = ci < ri
    incl = ci <= ri
    eye = ci == ri

    def chunk(c, carry):
        rows = pl.ds(pl.multiple_of(c * C, C), C)
        HS = range(RWKV_HEADS)
        sl = [slice(h * N, (h + 1) * N) for h in HS]
        cat = lambda a, b: jnp.concatenate([a, b], axis=0)
        at = [at_ref[rows, L] for L in sl]
        rt = [rt_ref[rows, L] for L in sl]
        bt = [bt_ref[rows, L] for L in sl]
        kt = [kt_ref[rows, L] for L in sl]
        bp = [bp_ref[rows, L] for L in sl]
        kp = [kp_ref[rows, L] for L in sl]
        vh = [v_ref[rows, L] for L in sl]
        ar = [cat(at[h], rt[h]) for h in HS]
        xb = [_nt(ar[h], bt[h]) for h in HS]
        xk = [_nt(ar[h], kt[h]) for h in HS]
        a_ab = [jnp.where(strict, xb[h][:C], 0.0) for h in HS]
        a_rb = [jnp.where(incl, xb[h][C:], 0.0).astype(BF16) for h in HS]
        a_ak = [jnp.where(strict, xk[h][:C], 0.0).astype(BF16) for h in HS]
        a_rk = [jnp.where(incl, xk[h][C:], 0.0).astype(BF16) for h in HS]
        ident = jnp.where(eye, 1.0, 0.0)
        t_inv = [ident + a_ab[h] for h in HS]
        ap = a_ab
        for _ in range(5):
            apb = [ap[h].astype(BF16) for h in HS]
            ap = [_dot(apb[h], apb[h]) for h in HS]
            t_inv = [t_inv[h] + _dot(t_inv[h].astype(BF16), ap[h].astype(BF16)) for h in HS]
        tb = [t_inv[h].astype(BF16) for h in HS]
        wv = [_dot(a_ak[h], vh[h]).astype(BF16) for h in HS]
        pb = [_dot(tb[h], at[h]).astype(BF16) for h in HS]
        u0b = [_dot(tb[h], wv[h]).astype(BF16) for h in HS]
        qm = [rt[h].astype(F32) + _dot(a_rb[h], pb[h]) for h in HS]
        y0 = [_dot(a_rb[h], u0b[h]) + _dot(a_rk[h], vh[h]) for h in HS]
        gm = [jnp.where(eye, gc_ref[rows, sl[h]][0:1, :], 0.0) + _tn(bp[h], pb[h]) for h in HS]
        hadd = [_tn(cat(bp[h], kp[h]), cat(u0b[h], vh[h])) for h in HS]
        for h in HS:
            z = _dot(cat(qm[h], gm[h]).astype(BF16), h_ref[h].astype(BF16))
            y_ref[rows, sl[h]] = z[:C] + y0[h]
            h_ref[h] = z[C:] + hadd[h]
        return carry

    lax.fori_loop(0, TB // C, chunk, 0)

    y = y_ref[...]
    segm = seg_ref[...]
    mu = _dot(y.astype(BF16), segm) * (1.0 / N)
    yc = y - mu
    var = _dot((yc * yc).astype(BF16), segm) * (1.0 / N)
    yn = yc * lax.rsqrt(var + RWKV_GN_EPS) * gng_ref[...] + gnb_ref[...]
    bonus = _dot((r * km * rk_ref[...]).astype(BF16), segm) * v
    o_ref[...] = ((yn + bonus) * g).astype(o_ref.dtype)


def _rwkv(a_proj, mu, w2, w0, a2, a0, g2, kks, kas, rk, gng, gnb, tb):
    T = a_proj.shape[0]
    tb = min(tb, T)
    W = RWKV_WIDTH
    hid = np.arange(W) // RWKV_HEAD
    seg = jnp.asarray(hid[:, None] == hid[None, :], dtype=BF16)
    t_idx = np.arange(tb)
    tri = jnp.asarray((t_idx[:, None] // RWKV_CHUNK == t_idx[None, :] // RWKV_CHUNK)
                      & (t_idx[None, :] <= t_idx[:, None]), dtype=F32)
    full = lambda shp: pl.BlockSpec(shp, lambda i: (0,) * len(shp))
    row = lambda n: full((1, n))
    bf = lambda: pltpu.VMEM((tb, W), BF16)
    return pl.pallas_call(
        _rwkv_kernel,
        grid=(T // tb,),
        in_specs=[pl.BlockSpec((tb, RWKV_IN_PAD), lambda i: (i, 0)),
                  row(RWKV_IN_PAD), full((LORA_PAD, W)), row(W), full((LORA_PAD, W)), row(W),
                  full((RWKV_GATE_LORA, W)), row(W), row(W), row(W), row(W), row(W),
                  full((W, W)), full((tb, tb))],
        out_specs=pl.BlockSpec((tb, W), lambda i: (i, 0)),
        out_shape=jax.ShapeDtypeStruct((T, W), BF16),
        scratch_shapes=[pltpu.VMEM((1, RWKV_IN_PAD), F32),
                        pltpu.VMEM((RWKV_HEADS, RWKV_HEAD, RWKV_HEAD), F32),
                        bf(), bf(), bf(), bf(), bf(), bf(), bf(),
                        pltpu.VMEM((tb, W), F32), pltpu.VMEM((tb, W), F32)],
        compiler_params=_cparams(("arbitrary",)),
        name="rwkv7",
    )(a_proj, mu, w2, w0, a2, a0, g2, kks, kas, rk, gng, gnb, seg, tri)


def _diff_kernel(q_ref, k_ref, vt_ref, lq1_ref, lk1_ref, lq2_ref, lk2_ref, sg_ref, o_ref,
                 m_ref, l_ref, acc_ref, s_ref, p_ref, a_ref, bm_ref, *, lambda_init):
    tq = q_ref.shape[0]
    nkb, _, tk = vt_ref.shape
    i = pl.program_id(1)
    q = q_ref[...]
    lane = lax.broadcasted_iota(jnp.int32, q.shape, 1)
    qm = (jnp.where(lane < DIFF_HEAD, q, jnp.zeros_like(q)),
          jnp.where(lane >= DIFF_HEAD, q, jnp.zeros_like(q)))
    m_ref[...] = jnp.full(m_ref.shape, -jnp.inf, F32)
    l_ref[...] = jnp.zeros_like(l_ref)
    acc_ref[...] = jnp.zeros_like(acc_ref)
    p_ref[1] = jnp.zeros(p_ref.shape[1:], BF16)
    a_ref[1] = jnp.ones(a_ref.shape[1:], F32)

    def colreduce(x, op):
        x4 = x.reshape(8, tk // 64, 8, tq)
        return op(op(op(x4, axis=1), axis=0), axis=0, keepdims=True)

    def scores(j, slot):
        jc = jnp.minimum(j, nkb - 1)
        kj = k_ref[pl.ds(pl.multiple_of(jc * tk, tk), tk), :]
        for mi in range(2):
            s = _nt(kj, qm[mi])
            s_ref[slot, mi] = s
            bm_ref[slot, mi] = colreduce(s, jnp.max)

    def values(j, slot):
        vj = vt_ref[jnp.clip(j, 0, nkb - 1)]
        for mi in range(2):
            acc_ref[mi] = a_ref[slot, mi] * acc_ref[mi] + _dot(vj, p_ref[slot, mi])

    def softmax(j, slot, masked):
        for mi in range(2):
            s = s_ref[slot, mi]
            if masked:
                d = (lax.broadcasted_iota(jnp.int32, (tk, tq), 0)
                     - lax.broadcasted_iota(jnp.int32, (tk, tq), 1))
                s = jnp.where(d <= i * tq - j * tk, s, -jnp.inf)
                bmax = colreduce(s, jnp.max)
            else:
                bmax = bm_ref[slot, mi]
            m_old = m_ref[mi]
            m_new = jnp.maximum(m_old, bmax)
            alpha = jnp.exp2(m_old - m_new)
            p = jnp.exp2(s - m_new)
            l_ref[mi] = alpha * l_ref[mi] + colreduce(p, jnp.sum)
            m_ref[mi] = m_new
            a_ref[slot, mi] = alpha
            p_ref[slot, mi] = p.astype(BF16)

    def half(j, cur, masked):
        nxt = 1 - cur
        scores(j + 1, nxt)
        values(j - 1, nxt)
        softmax(j, cur, masked)

    def pair(jj, carry):
        half(2 * jj, 0, False)
        half(2 * jj + 1, 1, False)
        return carry

    scores(0, 0)
    n_pairs = (i + 2) // 2
    lax.fori_loop(0, n_pairs - 1, pair, 0)
    j_last = 2 * (n_pairs - 1)
    half(j_last, 0, True)
    half(j_last + 1, 1, True)
    values(j_last + 1, 1)

    lam = (jnp.exp(jnp.sum(lq1_ref[...] * lk1_ref[...])) - jnp.exp(jnp.sum(lq2_ref[...] * lk2_ref[...]))
           + lambda_init)
    o = acc_ref[0] / l_ref[0] - lam * (acc_ref[1] / l_ref[1])
    o = o * lax.rsqrt(jnp.mean(o * o, axis=0, keepdims=True) + LN_EPS) * (1.0 - lambda_init)
    o = o * sg_ref[...]
    o_ref[...] = o.T.astype(o_ref.dtype)


def _diff_attention(qk, vt, lq1, lk1, lq2, lk2, subln_g, lambda_init, tq):
    T = qk.shape[0]
    tk = vt.shape[2]
    assert tq == tk
    H, E = DIFF_HEADS, 2 * DIFF_HEAD
    vec = lambda: pl.BlockSpec((1, DIFF_HEAD), lambda h, i: (0, 0))
    return pl.pallas_call(
        functools.partial(_diff_kernel, lambda_init=lambda_init),
        grid=(H, T // tq),
        in_specs=[pl.BlockSpec((tq, E), lambda h, i: (i, h)),
                  pl.BlockSpec((T, E), lambda h, i: (0, H + h)),
                  pl.BlockSpec((T // tk, E, tk), lambda h, i: (0, h, 0)),
                  vec(), vec(), vec(), vec(),
                  pl.BlockSpec((E, 1), lambda h, i: (0, 0))],
        out_specs=pl.BlockSpec((tq, E), lambda h, i: (i, h)),
        out_shape=jax.ShapeDtypeStruct((T, DIFF_V), BF16),
        scratch_shapes=[pltpu.VMEM((2, 1, tq), F32), pltpu.VMEM((2, 1, tq), F32),
                        pltpu.VMEM((2, E, tq), F32), pltpu.VMEM((2, 2, tk, tq), F32),
                        pltpu.VMEM((2, 2, tk, tq), BF16), pltpu.VMEM((2, 2, 1, tq), F32),
                        pltpu.VMEM((2, 2, 1, tq), F32)],
        compiler_params=_cparams(("arbitrary", "arbitrary")),
        name="diff_attn",
    )(qk, qk, vt, lq1, lk1, lq2, lk2, subln_g)


def _gla_kernel(c_ref, g2_ref, gb_ref, gng_ref, tri_ref, segv_ref, o_ref,
                st_ref, oacc_ref):
    TB = c_ref.shape[0]
    KW, VW, S = GLA_K_WIDTH, GLA_V_WIDTH, GLA_SUB
    DK, DV = GLA_KEY, GLA_VALUE

    @pl.when(pl.program_id(0) == 0)
    def _():
        st_ref[...] = jnp.zeros_like(st_ref)

    g_lo = c_ref[:, 2 * KW + 2 * VW:]
    gate = _dot(g_lo, g2_ref[...], precision=HIGHEST) + gb_ref[...]
    log_a = jax.nn.log_sigmoid(gate) * (1.0 / GLA_TAU)
    b = _dot(tri_ref[...], log_a, precision=HIGHEST)
    NS = TB // S
    b3 = b.reshape(NS, S, KW)
    q3 = c_ref[:, 0:KW].reshape(NS, S, KW)
    k3 = c_ref[:, KW:2 * KW].reshape(NS, S, KW)
    v = c_ref[:, 2 * KW:2 * KW + VW]
    v3 = v.reshape(NS, S, VW)
    bl3 = b3[:, S - 1:S, :]
    qb = (q3 * jnp.exp(b3)).astype(BF16).reshape(TB, KW)
    kb = (k3 * jnp.exp(bl3 - b3)).astype(BF16).reshape(TB, KW)
    dec = jnp.exp(bl3)
    vb = v.astype(BF16)

    ri = lax.broadcasted_iota(jnp.int32, (NS, S, 1), 1)
    segv = segv_ref[...]
    acc = jnp.zeros((NS, S, VW), F32)
    for j in range(S):
        e = jnp.exp(jnp.minimum(b3 - b3[:, j:j + 1, :], 0.0))
        t = (q3 * e * k3[:, j:j + 1, :]).astype(BF16).reshape(TB, KW)
        att = _dot(t, segv).reshape(NS, S, VW)
        acc = acc + jnp.where(ri >= j, att, 0.0) * v3[:, j:j + 1, :]
    oacc_ref[...] = acc.reshape(TB, VW)

    for h in range(GLA_HEADS):
        lk = slice(h * DK, (h + 1) * DK)
        lv = slice(h * DV, (h + 1) * DV)
        kv = [_tn(vb[s * S:(s + 1) * S, lv], kb[s * S:(s + 1) * S, lk]) for s in range(NS)]
        st = st_ref[h]
        for s in range(NS):
            rows = slice(s * S, (s + 1) * S)
            oacc_ref[rows, lv] += _nt(qb[rows, lk], st.astype(BF16))
            st = st * dec[s][:, lk] + kv[s]
        st_ref[h] = st

    o = oacc_ref[...]
    rgate = c_ref[:, 2 * KW + VW:2 * KW + 2 * VW]
    for h in range(GLA_HEADS):
        lv = slice(h * DV, (h + 1) * DV)
        oh = o[:, lv]
        oh = oh * lax.rsqrt(jnp.mean(oh * oh, axis=1, keepdims=True) + LN_EPS) * gng_ref[:, lv]
        o_ref[:, lv] = (oh * jax.nn.silu(rgate[:, lv])).astype(o_ref.dtype)


def _gla(c_proj, g2, gb, gng, tb):
    T = c_proj.shape[0]
    tb = min(tb, T)
    KW, VW = GLA_K_WIDTH, GLA_V_WIDTH
    t_idx = np.arange(tb)
    tri = jnp.asarray((t_idx[:, None] // GLA_SUB == t_idx[None, :] // GLA_SUB)
                      & (t_idx[None, :] <= t_idx[:, None]), dtype=F32)
    segv = jnp.asarray((np.arange(KW) // GLA_KEY)[:, None] == (np.arange(VW) // GLA_VALUE)[None, :], dtype=BF16)
    full = lambda shp: pl.BlockSpec(shp, lambda i: (0,) * len(shp))
    return pl.pallas_call(
        _gla_kernel,
        grid=(T // tb,),
        in_specs=[pl.BlockSpec((tb, GLA_IN_PAD), lambda i: (i, 0)),
                  full((LANE, KW)), full((1, KW)), full((1, VW)), full((tb, tb)), full((KW, VW))],
        out_specs=pl.BlockSpec((tb, VW), lambda i: (i, 0)),
        out_shape=jax.ShapeDtypeStruct((T, VW), BF16),
        scratch_shapes=[pltpu.VMEM((GLA_HEADS, GLA_VALUE, GLA_KEY), F32),
                        pltpu.VMEM((tb, VW), F32)],
        compiler_params=_cparams(("arbitrary",)),
        name="gla",
    )(c_proj, g2, gb, gng, tri, segv)


def _merge_kernel(oa_ref, ob_ref, oc_ref, g_ref, pa_ref, pb_ref, pc_ref, o_ref):
    D = D_MODEL
    m = jax.nn.sigmoid(g_ref[:, 0:D]) * _dot(oa_ref[...], pa_ref[...])
    m = m + jax.nn.sigmoid(g_ref[:, D:2 * D]) * _dot(ob_ref[...], pb_ref[...])
    m = m + jax.nn.sigmoid(g_ref[:, 2 * D:3 * D]) * _dot(oc_ref[...], pc_ref[...])
    o_ref[...] = m.astype(o_ref.dtype)


def _merge(oa, ob, oc, gates, pa, pb, pc, tm):
    T = oa.shape[0]
    tm = min(tm, T)
    D = D_MODEL
    rowblk = lambda n: pl.BlockSpec((tm, n), lambda i: (i, 0))
    full = lambda shp: pl.BlockSpec(shp, lambda i: (0, 0))
    return pl.pallas_call(
        _merge_kernel,
        grid=(T // tm,),
        in_specs=[rowblk(RWKV_WIDTH), rowblk(DIFF_V), rowblk(GLA_V_WIDTH), rowblk(3 * D),
                  full((RWKV_WIDTH, D)), full((DIFF_V, D)), full((GLA_V_WIDTH, D))],
        out_specs=rowblk(D),
        out_shape=jax.ShapeDtypeStruct((T, D), BF16),
        compiler_params=_cparams(("arbitrary",)),
        name="merge",
    )(oa, ob, oc, gates, pa, pb, pc)


def _layer_norm(z, g, b):
    mu = jnp.mean(z, axis=1, keepdims=True)
    zc = z - mu
    var = jnp.mean(zc * zc, axis=1, keepdims=True)
    return zc * lax.rsqrt(var + LN_EPS) * g + b


def _mm_ln_kernel(a_ref, w_ref, res_ref, g_ref, b_ref, o_ref, ob_ref):
    z = DEEPNORM_ALPHA * res_ref[...] + _dot(a_ref[...], w_ref[...])
    out = _layer_norm(z, g_ref[...], b_ref[...])
    o_ref[...] = out
    ob_ref[...] = out.astype(BF16)


def _mm_ln(a, w, res, g, b, tm):
    T, K = a.shape
    D = D_MODEL
    tm = min(tm, T)
    rowblk = lambda n: pl.BlockSpec((tm, n), lambda i: (i, 0))
    full = lambda shp: pl.BlockSpec(shp, lambda i: (0, 0))
    return pl.pallas_call(
        _mm_ln_kernel,
        grid=(T // tm,),
        in_specs=[rowblk(K), full((K, D)), rowblk(D), full((1, D)), full((1, D))],
        out_specs=[rowblk(D), rowblk(D)],
        out_shape=[jax.ShapeDtypeStruct((T, D), F32), jax.ShapeDtypeStruct((T, D), BF16)],
        compiler_params=_cparams(("arbitrary",)),
        name="wo_ln",
    )(a, w, res, g, b)


def _router_kernel(x_ref, w_ref, b_ref, o_ref):
    G, E = MOE_GROUPS, MOE_EXPERTS
    logits = _dot(x_ref[...], w_ref[...], precision=HIGHEST) + b_ref[...]
    lane = lax.broadcasted_iota(jnp.int32, logits.shape, 1)
    neg = -jnp.inf
    big = jnp.int32(1 << 20)
    is_g = lane < G
    lg = jnp.where(is_g, logits, neg)
    gmax = jnp.max(lg, axis=1, keepdims=True)
    gidx = jnp.min(jnp.where(is_g & (lg == gmax), lane, big), axis=1, keepdims=True)
    g_p = 1.0 / jnp.sum(jnp.exp(lg - gmax), axis=1, keepdims=True)
    lo = G + gidx * E
    in_grp = (lane >= lo) & (lane < lo + E)
    le = jnp.where(in_grp, logits, neg)
    m1 = jnp.max(le, axis=1, keepdims=True)
    i1 = jnp.min(jnp.where(in_grp & (le == m1), lane, big), axis=1, keepdims=True)
    le2 = jnp.where(lane == i1, neg, le)
    m2 = jnp.max(le2, axis=1, keepdims=True)
    i2 = jnp.min(jnp.where(in_grp & (le2 == m2), lane, big), axis=1, keepdims=True)
    e2 = jnp.exp(m2 - m1)
    p1 = 1.0 / (1.0 + e2)
    p2 = e2 / (1.0 + e2)
    comb = jnp.where(lane == i1, g_p * p1, 0.0) + jnp.where(lane == i2, g_p * p2, 0.0)
    o_ref[...] = comb


def _router(x1, wr, br, tm):
    T, D = x1.shape
    tm = min(tm, T)
    return pl.pallas_call(
        _router_kernel,
        grid=(T // tm,),
        in_specs=[pl.BlockSpec((tm, D), lambda i: (i, 0)),
                  pl.BlockSpec((D, LANE), lambda i: (0, 0)),
                  pl.BlockSpec((1, LANE), lambda i: (0, 0))],
        out_specs=pl.BlockSpec((tm, LANE), lambda i: (i, 0)),
        out_shape=jax.ShapeDtypeStruct((T, LANE), F32),
        compiler_params=_cparams(("arbitrary",)),
        name="router",
    )(x1, wr, br)


def _moe_kernel(xb_ref, x_ref, comb_ref, wg_ref, wu_ref, wd_ref, g_ref, b_ref, o_ref, ob_ref, acc_ref):
    e = pl.program_id(1)

    @pl.when(e == 0)
    def _():
        acc_ref[...] = jnp.zeros_like(acc_ref)

    xb = xb_ref[...]
    hg = _dot(xb, wg_ref[0])
    hu = _dot(xb, wu_ref[0])
    comb = comb_ref[...]
    lane = lax.broadcasted_iota(jnp.int32, comb.shape, 1)
    cw = jnp.sum(jnp.where(lane == e + MOE_GROUPS, comb, 0.0), axis=1, keepdims=True)
    h = (jax.nn.silu(hg) * hu * cw).astype(BF16)
    acc_ref[...] += _dot(h, wd_ref[0])

    @pl.when(e == N_EXPERTS - 1)
    def _():
        z = DEEPNORM_ALPHA * x_ref[...] + acc_ref[...]
        out = _layer_norm(z, g_ref[...], b_ref[...])
        o_ref[...] = out
        ob_ref[...] = out.astype(BF16)


def _moe(xb, x, comb, wg, wu, wd, g, b, tm):
    T, D = x.shape
    F = MOE_HIDDEN
    tm = min(tm, T)
    rowblk = lambda n: pl.BlockSpec((tm, n), lambda i, e: (i, 0))
    return pl.pallas_call(
        _moe_kernel,
        grid=(T // tm, N_EXPERTS),
        in_specs=[rowblk(D), rowblk(D), rowblk(LANE),
                  pl.BlockSpec((1, D, F), lambda i, e: (e, 0, 0)),
                  pl.BlockSpec((1, D, F), lambda i, e: (e, 0, 0)),
                  pl.BlockSpec((1, F, D), lambda i, e: (e, 0, 0)),
                  pl.BlockSpec((1, D), lambda i, e: (0, 0)),
                  pl.BlockSpec((1, D), lambda i, e: (0, 0))],
        out_specs=[rowblk(D), rowblk(D)],
        out_shape=[jax.ShapeDtypeStruct((T, D), F32), jax.ShapeDtypeStruct((T, D), BF16)],
        scratch_shapes=[pltpu.VMEM((tm, D), F32)],
        compiler_params=_cparams(("arbitrary", "arbitrary")),
        name="moe_experts",
    )(xb, x, comb, wg, wu, wd, g, b)


def _pad_cols(w, n):
    return jnp.pad(w, ((0, 0), (0, n - w.shape[1])))


def _layer(x, xb, l, p, rope_tabs):
    T = x.shape[0]
    W = RWKV_WIDTH
    KW, VW = GLA_K_WIDTH, GLA_V_WIDTH
    lambda_init = 0.8 - 0.6 * math.exp(-0.3 * l)

    w_in = p['w_in'][l]
    o_b = RWKV_IN
    o_c = o_b + DIFF_IN
    o_g = o_c + GLA_IN
    lo1 = 3 * W + RWKV_DECAY_LORA
    lo2 = lo1 + RWKV_AAA_LORA
    w_a = jnp.concatenate([_pad_cols(w_in[:, :lo1], 3 * W + LORA_PAD), _pad_cols(w_in[:, lo1:lo2], LORA_PAD),
                           w_in[:, lo2:o_b]], axis=1).astype(BF16)
    mu = p['rwkv_mu'][l]
    mu_a = jnp.concatenate([jnp.pad(mu[:lo1], (0, LORA_PAD - RWKV_DECAY_LORA)),
                            jnp.pad(mu[lo1:lo2], (0, LORA_PAD - RWKV_AAA_LORA)), mu[lo2:]])[None, :]
    qscale = DIFF_HEAD ** -0.5 * math.log2(math.e)
    w_qk = jnp.concatenate([w_in[:, o_b:o_b + DIFF_QK] * qscale, w_in[:, o_b + DIFF_QK:o_b + 2 * DIFF_QK]],
                           axis=1).astype(BF16)
    w_vt = w_in[:, o_b + 2 * DIFF_QK:o_c].T.astype(BF16)
    c0 = o_c
    w_c = jnp.concatenate([w_in[:, c0:c0 + KW] * (GLA_KEY ** -0.5), w_in[:, c0 + KW:c0 + 2 * KW + VW],
                           w_in[:, c0 + 2 * KW + VW + GLA_GATE_LORA:o_g],
                           _pad_cols(w_in[:, c0 + 2 * KW + VW:c0 + 2 * KW + VW + GLA_GATE_LORA], LANE)],
                          axis=1).astype(BF16)
    w_g = w_in[:, o_g:].astype(BF16)

    proj_a = _matmul(xb, w_a, F32, 1024, 1024)
    qk = _matmul_rope(xb, w_qk, rope_tabs, 1024, 1024)
    tk = min(512, T)
    vt = _matmul_vt(w_vt, xb, tk)
    proj_c = _matmul(xb, w_c, F32, 512, GLA_IN_PAD)
    gates = _matmul(xb, w_g, F32, 1024, 1024)

    padr = lambda w: jnp.pad(w, ((0, LORA_PAD - w.shape[0]), (0, 0)))
    r2 = lambda v: v.reshape(1, -1)
    o_a = _rwkv(proj_a, mu_a, padr(p['rwkv_w2'][l]).astype(BF16), r2(p['rwkv_w0'][l]),
                padr(p['rwkv_a2'][l]).astype(BF16), r2(p['rwkv_a0'][l]), p['rwkv_g2'][l].astype(BF16),
                r2(p['rwkv_kk_scale'][l]), r2(p['rwkv_ka_scale'][l]), r2(p['rwkv_rk'][l]),
                r2(p['rwkv_gn_g'][l]), r2(p['rwkv_gn_b'][l]), 256)
    o_bb = _diff_attention(qk, vt, r2(p['diff_lq1'][l]), r2(p['diff_lk1'][l]), r2(p['diff_lq2'][l]),
                           r2(p['diff_lk2'][l]), p['diff_subln_g'][l].reshape(-1, 1), lambda_init, tk)
    g2p = jnp.pad(p['gla_g2'][l], ((0, LANE - GLA_GATE_LORA), (0, 0)))
    o_c = _gla(proj_c, g2p, r2(p['gla_gb'][l]), r2(p['gla_gn_g'][l]), 256)

    merged = _merge(o_a, o_bb, o_c, gates, p['proj_a'][l].astype(BF16), p['proj_b'][l].astype(BF16),
                    p['proj_c'][l].astype(BF16), 256)
    x1, x1b = _mm_ln(merged, p['w_o'][l].astype(BF16), x, r2(p['ln1_g'][l]), r2(p['ln1_b'][l]), 256)

    wr = _pad_cols(jnp.concatenate([p['router_group_w'][l], p['router_expert_w'][l]], axis=1), LANE)
    br = _pad_cols(jnp.concatenate([p['router_group_b'][l], p['router_expert_b'][l]])[None, :], LANE)
    comb = _router(x1, wr, br, 512)
    F = MOE_HIDDEN
    wg = p['moe_w_gate'][l].reshape(N_EXPERTS, D_MODEL, F).astype(BF16)
    wu = p['moe_w_up'][l].reshape(N_EXPERTS, D_MODEL, F).astype(BF16)
    wd = p['moe_w_down'][l].reshape(N_EXPERTS, F, D_MODEL).astype(BF16)
    return _moe(x1b, x1, comb, wg, wu, wd, r2(p['ln2_g'][l]), r2(p['ln2_b'][l]), 512)


def kernel(x, w_in, rwkv_mu, rwkv_w2, rwkv_w0, rwkv_a2, rwkv_a0, rwkv_g2, rwkv_kk_scale, rwkv_ka_scale, rwkv_rk, rwkv_gn_g, rwkv_gn_b, diff_lq1, diff_lk1, diff_lq2, diff_lk2, diff_subln_g, gla_g2, gla_gb, gla_gn_g, proj_a, proj_b, proj_c, w_o, ln1_g, ln1_b, router_group_w, router_group_b, router_expert_w, router_expert_b, moe_w_gate, moe_w_up, moe_w_down, ln2_g, ln2_b):
    p = dict(w_in=w_in, rwkv_mu=rwkv_mu, rwkv_w2=rwkv_w2, rwkv_w0=rwkv_w0, rwkv_a2=rwkv_a2, rwkv_a0=rwkv_a0,
             rwkv_g2=rwkv_g2, rwkv_kk_scale=rwkv_kk_scale, rwkv_ka_scale=rwkv_ka_scale, rwkv_rk=rwkv_rk,
             rwkv_gn_g=rwkv_gn_g, rwkv_gn_b=rwkv_gn_b, diff_lq1=diff_lq1, diff_lk1=diff_lk1, diff_lq2=diff_lq2,
             diff_lk2=diff_lk2, diff_subln_g=diff_subln_g, gla_g2=gla_g2, gla_gb=gla_gb, gla_gn_g=gla_gn_g,
             proj_a=proj_a, proj_b=proj_b, proj_c=proj_c, w_o=w_o, ln1_g=ln1_g, ln1_b=ln1_b,
             router_group_w=router_group_w, router_group_b=router_group_b, router_expert_w=router_expert_w,
             router_expert_b=router_expert_b, moe_w_gate=moe_w_gate, moe_w_up=moe_w_up, moe_w_down=moe_w_down,
             ln2_g=ln2_g, ln2_b=ln2_b)
    B, T, D = x.shape
    assert B == 1 and D == D_MODEL
    xf = x.reshape(T, D)
    xb = xf.astype(BF16)
    tabs = _rope_tables(T)
    for l in range(DEPTH):
        xf, xb = _layer(xf, xb, l, p, tabs)
    return xf.reshape(B, T, D)
```

```python
import functools
import math

import numpy as np
import jax
import jax.numpy as jnp
from jax import lax
from jax.experimental import pallas as pl
from jax.experimental.pallas import tpu as pltpu

F32 = jnp.float32
BF16 = jnp.bfloat16
HIGHEST = lax.Precision.HIGHEST

D_MODEL = 2048
DEPTH = 2
DEEPNORM_ALPHA = (2 * DEPTH) ** 0.25
LN_EPS = 1e-5

RWKV_HEADS = 8
RWKV_HEAD = 64
RWKV_WIDTH = RWKV_HEADS * RWKV_HEAD
RWKV_DECAY_LORA = 96
RWKV_AAA_LORA = 96
RWKV_GATE_LORA = 256
RWKV_GN_EPS = 64e-5
RWKV_IN = 3 * RWKV_WIDTH + RWKV_DECAY_LORA + RWKV_AAA_LORA + RWKV_GATE_LORA
RWKV_CHUNK = 64
LORA_PAD = 128
RWKV_IN_PAD = 3 * RWKV_WIDTH + 2 * LORA_PAD + RWKV_GATE_LORA

DIFF_HEADS = 8
DIFF_HEAD = 64
DIFF_QK = DIFF_HEADS * 2 * DIFF_HEAD
DIFF_V = DIFF_HEADS * 2 * DIFF_HEAD
DIFF_IN = 2 * DIFF_QK + DIFF_V
ROPE_THETA = 500000.0
ROPE_DIM = DIFF_HEAD // 4
ROPE_HALF = ROPE_DIM // 2

GLA_HEADS = 4
GLA_KEY = 64
GLA_VALUE = 128
GLA_K_WIDTH = GLA_HEADS * GLA_KEY
GLA_V_WIDTH = GLA_HEADS * GLA_VALUE
GLA_GATE_LORA = 16
GLA_TAU = 16.0
GLA_SUB = 16
GLA_IN = 2 * GLA_K_WIDTH + 2 * GLA_V_WIDTH + GLA_GATE_LORA
GLA_IN_PAD = 2 * GLA_K_WIDTH + 2 * GLA_V_WIDTH + 128

N_BRANCHES = 3
MOE_GROUPS = 4
MOE_EXPERTS = 8
MOE_HIDDEN = 256
N_EXPERTS = MOE_GROUPS * MOE_EXPERTS
MOE_TILE = 512
LANE = 128
V7X_VMEM_LIMIT = 56 * 1024 * 1024


def _cparams(sem):
    return pltpu.CompilerParams(dimension_semantics=sem, vmem_limit_bytes=V7X_VMEM_LIMIT)


def _nt(a, b, **kw):
    return lax.dot_general(a, b, (((1,), (1,)), ((), ())), preferred_element_type=F32, **kw)


def _tn(a, b, **kw):
    return lax.dot_general(a, b, (((0,), (0,)), ((), ())), preferred_element_type=F32, **kw)


def _dot(a, b, **kw):
    return jnp.dot(a, b, preferred_element_type=F32, **kw)


def _mm_kernel(a_ref, b_ref, o_ref):
    o_ref[...] = _dot(a_ref[...], b_ref[...]).astype(o_ref.dtype)


def _matmul(a, b, out_dtype, tm, tn):
    M, K = a.shape
    N = b.shape[1]
    tm, tn = min(tm, M), min(tn, N)
    return pl.pallas_call(
        _mm_kernel,
        grid=(N // tn, M // tm),
        in_specs=[pl.BlockSpec((tm, K), lambda j, i: (i, 0)),
                  pl.BlockSpec((K, tn), lambda j, i: (0, j))],
        out_specs=pl.BlockSpec((tm, tn), lambda j, i: (i, j)),
        out_shape=jax.ShapeDtypeStruct((M, N), out_dtype),
        compiler_params=_cparams(("arbitrary", "arbitrary")),
        name="proj_mm",
    )(a, b)


def _mm_rope_kernel(a_ref, b_ref, c_ref, sa_ref, sb_ref, o_ref):
    acc = _dot(a_ref[...], b_ref[...])
    c, sa, sb = c_ref[...], sa_ref[...], sb_ref[...]
    for h in range(acc.shape[1] // LANE):
        blk = acc[:, h * LANE:(h + 1) * LANE]
        out = blk * c + pltpu.roll(blk, LANE - ROPE_HALF, 1) * sa + pltpu.roll(blk, ROPE_HALF, 1) * sb
        o_ref[:, h * LANE:(h + 1) * LANE] = out.astype(o_ref.dtype)


def _matmul_rope(a, b, tabs, tm, tn):
    M, K = a.shape
    N = b.shape[1]
    tm, tn = min(tm, M), min(tn, N)
    tab_spec = pl.BlockSpec((tm, LANE), lambda j, i: (i, 0))
    return pl.pallas_call(
        _mm_rope_kernel,
        grid=(N // tn, M // tm),
        in_specs=[pl.BlockSpec((tm, K), lambda j, i: (i, 0)),
                  pl.BlockSpec((K, tn), lambda j, i: (0, j)),
                  tab_spec, tab_spec, tab_spec],
        out_specs=pl.BlockSpec((tm, tn), lambda j, i: (i, j)),
        out_shape=jax.ShapeDtypeStruct((M, N), BF16),
        compiler_params=_cparams(("arbitrary", "arbitrary")),
        name="proj_qk_rope",
    )(a, b, *tabs)


def _mm_vt_kernel(wt_ref, x_ref, o_ref):
    o_ref[0] = _nt(wt_ref[...], x_ref[...]).astype(o_ref.dtype)


def _matmul_vt(wt, x, tk):
    N, K = wt.shape
    T = x.shape[0]
    return pl.pallas_call(
        _mm_vt_kernel,
        grid=(T // tk,),
        in_specs=[pl.BlockSpec((N, K), lambda i: (0, 0)),
                  pl.BlockSpec((tk, K), lambda i: (i, 0))],
        out_specs=pl.BlockSpec((1, N, tk), lambda i: (i, 0, 0)),
        out_shape=jax.ShapeDtypeStruct((T // tk, N, tk), BF16),
        compiler_params=_cparams(("arbitrary",)),
        name="proj_vt",
    )(wt, x)


def _rope_tables(T):
    pos = jnp.arange(T, dtype=F32)[:, None]
    inv_freq = ROPE_THETA ** (-jnp.arange(ROPE_HALF, dtype=F32) / ROPE_HALF)
    lane = np.arange(LANE) % DIFF_HEAD
    ang = pos * inv_freq[None, :]
    cos8, sin8 = jnp.cos(ang), jnp.sin(ang)
    idx = jnp.asarray(lane % ROPE_HALF)
    cos_l, sin_l = cos8[:, idx], sin8[:, idx]
    first = jnp.asarray(lane < ROPE_HALF)[None, :]
    second = jnp.asarray((lane >= ROPE_HALF) & (lane < ROPE_DIM))[None, :]
    c = jnp.where(first | second, cos_l, 1.0)
    sa = jnp.where(first, -sin_l, 0.0)
    sb = jnp.where(second, sin_l, 0.0)
    return c, sa, sb


def _rwkv_kernel(a_ref, mu_ref, w2_ref, w0_ref, a2_ref, a0_ref, g2_ref, kks_ref, kas_ref, rk_ref,
                 gng_ref, gnb_ref, seg_ref, tri_ref, o_ref,
                 prev_ref, h_ref, at_ref, rt_ref, bt_ref, kt_ref, bp_ref, kp_ref, v_ref, gc_ref, y_ref):
    TB = a_ref.shape[0]
    C, N, W = RWKV_CHUNK, RWKV_HEAD, RWKV_WIDTH

    @pl.when(pl.program_id(0) == 0)
    def _():
        prev_ref[...] = jnp.zeros_like(prev_ref)
        h_ref[...] = jnp.zeros_like(h_ref)

    a = a_ref[...]
    row = lax.broadcasted_iota(jnp.int32, a.shape, 0)
    shifted = jnp.where(row == 0, prev_ref[...], pltpu.roll(a, 1, 0))
    prev_ref[...] = a[TB - 1:TB, :]
    xs = a + (shifted - a) * mu_ref[...]

    r = xs[:, 0:W]
    k = xs[:, W:2 * W]
    v = xs[:, 2 * W:3 * W]
    w_lo = xs[:, 3 * W:3 * W + LORA_PAD]
    a_lo = xs[:, 3 * W + LORA_PAD:3 * W + 2 * LORA_PAD]
    g_lo = xs[:, 3 * W + 2 * LORA_PAD:]

    z = w0_ref[...] + _dot(jnp.tanh(w_lo).astype(BF16), w2_ref[...])
    w = -jax.nn.softplus(-z) - 0.5
    lw = -jnp.exp(w)
    aa = jax.nn.sigmoid(a0_ref[...] + _dot(a_lo.astype(BF16), a2_ref[...]))
    g = _dot(jax.nn.sigmoid(g_lo).astype(BF16), g2_ref[...])

    seg = seg_ref[...]
    kk = k * kks_ref[...]
    ss = _dot((kk * kk).astype(BF16), seg)
    kk = kk * lax.rsqrt(jnp.maximum(ss, 1e-24))
    km = k * (1.0 + (aa - 1.0) * kas_ref[...])
    beta = kk * aa

    tri = tri_ref[...]
    cum = _dot(tri, lw, precision=HIGHEST)
    rowc = lax.broadcasted_iota(jnp.int32, (TB, TB), 0) // C
    colc = lax.broadcasted_iota(jnp.int32, (TB, TB), 1) // C
    allc = jnp.where(rowc == colc, 1.0, 0.0).astype(F32)
    cum_c = _dot(allc, lw, precision=HIGHEST)

    e_in = jnp.exp(cum)
    e_neg = jnp.exp(-cum)
    e_end = jnp.exp(cum_c - cum)
    at_ref[...] = (-kk * jnp.exp(cum - lw)).astype(BF16)
    rt_ref[...] = (r * e_in).astype(BF16)
    bt_ref[...] = (beta * e_neg).astype(BF16)
    kt_ref[...] = (km * e_neg).astype(BF16)
    bp_ref[...] = (beta * e_end).astype(BF16)
    kp_ref[...] = (km * e_end).astype(BF16)
    v_ref[...] = v.astype(BF16)
    gc_ref[...] = jnp.exp(cum_c)

    ri = lax.broadcasted_iota(jnp.int32, (C, C), 0)
    ci = lax.broadcasted_iota(jnp.int32, (C, C), 1)
    strict = ci < ri
    incl = ci <= ri
    eye = ci == ri

    def chunk(c, carry):
        rows = pl.ds(pl.multiple_of(c * C, C), C)
        HS = range(RWKV_HEADS)
        sl = [slice(h * N, (h + 1) * N) for h in HS]
        cat = lambda a, b: jnp.concatenate([a, b], axis=0)
        at = [at_ref[rows, L] for L in sl]
        rt = [rt_ref[rows, L] for L in sl]
        bt = [bt_ref[rows, L] for L in sl]
        kt = [kt_ref[rows, L] for L in sl]
        bp = [bp_ref[rows, L] for L in sl]
        kp = [kp_ref[rows, L] for L in sl]
        vh = [v_ref[rows, L] for L in sl]
        ar = [cat(at[h], rt[h]) for h in HS]
        xb = [_nt(ar[h], bt[h]) for h in HS]
        xk = [_nt(ar[h], kt[h]) for h in HS]
        a_ab = [jnp.where(strict, xb[h][:C], 0.0) for h in HS]
        a_rb = [jnp.where(incl, xb[h][C:], 0.0).astype(BF16) for h in HS]
        a_ak = [jnp.where(strict, xk[h][:C], 0.0).astype(BF16) for h in HS]
        a_rk = [jnp.where(incl, xk[h][C:], 0.0).astype(BF16) for h in HS]
        ident = jnp.where(eye, 1.0, 0.0)
        t_inv = [ident + a_ab[h] for h in HS]
        ap = a_ab
        for _ in range(5):
            apb = [ap[h].astype(BF16) for h in HS]
            ap = [_dot(apb[h], apb[h]) for h in HS]
            t_inv = [t_inv[h] + _dot(t_inv[h].astype(BF16), ap[h].astype(BF16)) for h in HS]
        tb = [t_inv[h].astype(BF16) for h in HS]
        wv = [_dot(a_ak[h], vh[h]).astype(BF16) for h in HS]
        pb = [_dot(tb[h], at[h]).astype(BF16) for h in HS]
        u0b = [_dot(tb[h], wv[h]).astype(BF16) for h in HS]
        qm = [rt[h].astype(F32) + _dot(a_rb[h], pb[h]) for h in HS]
        y0 = [_dot(a_rb[h], u0b[h]) + _dot(a_rk[h], vh[h]) for h in HS]
        gm = [jnp.where(eye, gc_ref[rows, sl[h]][0:1, :], 0.0) + _tn(bp[h], pb[h]) for h in HS]
        hadd = [_tn(cat(bp[h], kp[h]), cat(u0b[h], vh[h])) for h in HS]
        for h in HS:
            z = _dot(cat(qm[h], gm[h]).astype(BF16), h_ref[h].astype(BF16))
            y_ref[rows, sl[h]] = z[:C] + y0[h]
            h_ref[h] = z[C:] + hadd[h]
        return carry

    lax.fori_loop(0, TB // C, chunk, 0)

    y = y_ref[...]
    segm = seg_ref[...]
    mu = _dot(y.astype(BF16), segm) * (1.0 / N)
    yc = y - mu
    var = _dot((yc * yc).astype(BF16), segm) * (1.0 / N)
    yn = yc * lax.rsqrt(var + RWKV_GN_EPS) * gng_ref[...] + gnb_ref[...]
    bonus = _dot((r * km * rk_ref[...]).astype(BF16), segm) * v
    o_ref[...] = ((yn + bonus) * g).astype(o_ref.dtype)


def _rwkv(a_proj, mu, w2, w0, a2, a0, g2, kks, kas, rk, gng, gnb, tb):
    T = a_proj.shape[0]
    tb = min(tb, T)
    W = RWKV_WIDTH
    hid = np.arange(W) // RWKV_HEAD
    seg = jnp.asarray(hid[:, None] == hid[None, :], dtype=BF16)
    t_idx = np.arange(tb)
    tri = jnp.asarray((t_idx[:, None] // RWKV_CHUNK == t_idx[None, :] // RWKV_CHUNK)
                      & (t_idx[None, :] <= t_idx[:, None]), dtype=F32)
    full = lambda shp: pl.BlockSpec(shp, lambda i: (0,) * len(shp))
    row = lambda n: full((1, n))
    bf = lambda: pltpu.VMEM((tb, W), BF16)
    return pl.pallas_call(
        _rwkv_kernel,
        grid=(T // tb,),
        in_specs=[pl.BlockSpec((tb, RWKV_IN_PAD), lambda i: (i, 0)),
                  row(RWKV_IN_PAD), full((LORA_PAD, W)), row(W), full((LORA_PAD, W)), row(W),
                  full((RWKV_GATE_LORA, W)), row(W), row(W), row(W), row(W), row(W),
                  full((W, W)), full((tb, tb))],
        out_specs=pl.BlockSpec((tb, W), lambda i: (i, 0)),
        out_shape=jax.ShapeDtypeStruct((T, W), BF16),
        scratch_shapes=[pltpu.VMEM((1, RWKV_IN_PAD), F32),
                        pltpu.VMEM((RWKV_HEADS, RWKV_HEAD, RWKV_HEAD), F32),
                        bf(), bf(), bf(), bf(), bf(), bf(), bf(),
                        pltpu.VMEM((tb, W), F32), pltpu.VMEM((tb, W), F32)],
        compiler_params=_cparams(("arbitrary",)),
        name="rwkv7",
    )(a_proj, mu, w2, w0, a2, a0, g2, kks, kas, rk, gng, gnb, seg, tri)


def _diff_kernel(q_ref, k_ref, vt_ref, lq1_ref, lk1_ref, lq2_ref, lk2_ref, sg_ref, o_ref,
                 m_ref, l_ref, acc_ref, s_ref, p_ref, a_ref, *, lambda_init):
    tq = q_ref.shape[0]
    nkb, _, tk = vt_ref.shape
    i = pl.program_id(1)
    q = q_ref[...]
    lane = lax.broadcasted_iota(jnp.int32, q.shape, 1)
    qm = (jnp.where(lane < DIFF_HEAD, q, jnp.zeros_like(q)),
          jnp.where(lane >= DIFF_HEAD, q, jnp.zeros_like(q)))
    m_ref[...] = jnp.full(m_ref.shape, -jnp.inf, F32)
    l_ref[...] = jnp.zeros_like(l_ref)
    acc_ref[...] = jnp.zeros_like(acc_ref)
    p_ref[1] = jnp.zeros(p_ref.shape[1:], BF16)
    a_ref[1] = jnp.ones(a_ref.shape[1:], F32)

    def scores(j, slot):
        jc = jnp.minimum(j, nkb - 1)
        kj = k_ref[pl.ds(pl.multiple_of(jc * tk, tk), tk), :]
        for mi in range(2):
            s_ref[slot, mi] = _nt(kj, qm[mi])

    def values(j, slot):
        vj = vt_ref[jnp.clip(j, 0, nkb - 1)]
        for mi in range(2):
            acc_ref[mi] = a_ref[slot, mi] * acc_ref[mi] + _dot(vj, p_ref[slot, mi])

    def softmax(j, slot, masked):
        for mi in range(2):
            s = s_ref[slot, mi]
            if masked:
                d = (lax.broadcasted_iota(jnp.int32, (tk, tq), 0)
                     - lax.broadcasted_iota(jnp.int32, (tk, tq), 1))
                s = jnp.where(d <= i * tq - j * tk, s, -jnp.inf)
            m_old = m_ref[mi]
            m_new = jnp.maximum(m_old, jnp.max(s, axis=0, keepdims=True))
            alpha = jnp.exp2(m_old - m_new)
            p = jnp.exp2(s - m_new)
            l_ref[mi] = alpha * l_ref[mi] + jnp.sum(p, axis=0, keepdims=True)
            m_ref[mi] = m_new
            a_ref[slot, mi] = alpha
            p_ref[slot, mi] = p.astype(BF16)

    def half(j, cur, masked):
        nxt = 1 - cur
        scores(j + 1, nxt)
        values(j - 1, nxt)
        softmax(j, cur, masked)

    def pair(jj, carry):
        half(2 * jj, 0, False)
        half(2 * jj + 1, 1, False)
        return carry

    scores(0, 0)
    n_pairs = ((i + 1) * (tq // tk) + 1) // 2
    lax.fori_loop(0, n_pairs - 1, pair, 0)
    j_last = 2 * (n_pairs - 1)
    half(j_last, 0, True)
    half(j_last + 1, 1, True)
    values(j_last + 1, 1)

    lam = (jnp.exp(jnp.sum(lq1_ref[...] * lk1_ref[...])) - jnp.exp(jnp.sum(lq2_ref[...] * lk2_ref[...]))
           + lambda_init)
    o = acc_ref[0] / l_ref[0] - lam * (acc_ref[1] / l_ref[1])
    o = o * lax.rsqrt(jnp.mean(o * o, axis=0, keepdims=True) + LN_EPS) * (1.0 - lambda_init)
    o = o * sg_ref[...]
    o_ref[...] = o.T.astype(o_ref.dtype)


def _diff_attention(qk, vt, lq1, lk1, lq2, lk2, subln_g, lambda_init, tq):
    T = qk.shape[0]
    tk = vt.shape[2]
    assert tq % tk == 0 and T % tq == 0
    H, E = DIFF_HEADS, 2 * DIFF_HEAD
    vec = lambda: pl.BlockSpec((1, DIFF_HEAD), lambda h, i: (0, 0))
    return pl.pallas_call(
        functools.partial(_diff_kernel, lambda_init=lambda_init),
        grid=(H, T // tq),
        in_specs=[pl.BlockSpec((tq, E), lambda h, i: (i, h)),
                  pl.BlockSpec((T, E), lambda h, i: (0, H + h)),
                  pl.BlockSpec((T // tk, E, tk), lambda h, i: (0, h, 0)),
                  vec(), vec(), vec(), vec(),
                  pl.BlockSpec((E, 1), lambda h, i: (0, 0))],
        out_specs=pl.BlockSpec((tq, E), lambda h, i: (i, h)),
        out_shape=jax.ShapeDtypeStruct((T, DIFF_V), BF16),
        scratch_shapes=[pltpu.VMEM((2, 1, tq), F32), pltpu.VMEM((2, 1, tq), F32),
                        pltpu.VMEM((2, E, tq), F32), pltpu.VMEM((2, 2, tk, tq), F32),
                        pltpu.VMEM((2, 2, tk, tq), BF16), pltpu.VMEM((2, 2, 1, tq), F32)],
        compiler_params=_cparams(("arbitrary", "arbitrary")),
        name="diff_attn",
    )(qk, qk, vt, lq1, lk1, lq2, lk2, subln_g)


def _gla_kernel(c_ref, g2_ref, gb_ref, gng_ref, tri_ref, segv_ref, o_ref,
                st_ref, oacc_ref):
    TB = c_ref.shape[0]
    KW, VW, S = GLA_K_WIDTH, GLA_V_WIDTH, GLA_SUB
    DK, DV = GLA_KEY, GLA_VALUE

    @pl.when(pl.program_id(0) == 0)
    def _():
        st_ref[...] = jnp.zeros_like(st_ref)

    g_lo = c_ref[:, 2 * KW + 2 * VW:]
    gate = _dot(g_lo, g2_ref[...], precision=HIGHEST) + gb_ref[...]
    log_a = jax.nn.log_sigmoid(gate) * (1.0 / GLA_TAU)
    b = _dot(tri_ref[...], log_a, precision=HIGHEST)
    NS = TB // S
    b3 = b.reshape(NS, S, KW)
    q3 = c_ref[:, 0:KW].reshape(NS, S, KW)
    k3 = c_ref[:, KW:2 * KW].reshape(NS, S, KW)
    v = c_ref[:, 2 * KW:2 * KW + VW]
    v3 = v.reshape(NS, S, VW)
    bl3 = b3[:, S - 1:S, :]
    qb = (q3 * jnp.exp(b3)).astype(BF16).reshape(TB, KW)
    kb = (k3 * jnp.exp(bl3 - b3)).astype(BF16).reshape(TB, KW)
    dec = jnp.exp(bl3)
    vb = v.astype(BF16)

    ri = lax.broadcasted_iota(jnp.int32, (NS, S, 1), 1)
    segv = segv_ref[...]
    acc = jnp.zeros((NS, S, VW), F32)
    for j in range(S):
        e = jnp.exp(jnp.minimum(b3 - b3[:, j:j + 1, :], 0.0))
        t = (q3 * e * k3[:, j:j + 1, :]).astype(BF16).reshape(TB, KW)
        att = _dot(t, segv).reshape(NS, S, VW)
        acc = acc + jnp.where(ri >= j, att, 0.0) * v3[:, j:j + 1, :]
    oacc_ref[...] = acc.reshape(TB, VW)

    for h in range(GLA_HEADS):
        lk = slice(h * DK, (h + 1) * DK)
        lv = slice(h * DV, (h + 1) * DV)
        kv = [_tn(vb[s * S:(s + 1) * S, lv], kb[s * S:(s + 1) * S, lk]) for s in range(NS)]
        st = st_ref[h]
        for s in range(NS):
            rows = slice(s * S, (s + 1) * S)
            oacc_ref[rows, lv] += _nt(qb[rows, lk], st.astype(BF16))
            st = st * dec[s][:, lk] + kv[s]
        st_ref[h] = st

    o = oacc_ref[...]
    rgate = c_ref[:, 2 * KW + VW:2 * KW + 2 * VW]
    for h in range(GLA_HEADS):
        lv = slice(h * DV, (h + 1) * DV)
        oh = o[:, lv]
        oh = oh * lax.rsqrt(jnp.mean(oh * oh, axis=1, keepdims=True) + LN_EPS) * gng_ref[:, lv]
        o_ref[:, lv] = (oh * jax.nn.silu(rgate[:, lv])).astype(o_ref.dtype)


def _gla(c_proj, g2, gb, gng, tb):
    T = c_proj.shape[0]
    tb = min(tb, T)
    KW, VW = GLA_K_WIDTH, GLA_V_WIDTH
    t_idx = np.arange(tb)
    tri = jnp.asarray((t_idx[:, None] // GLA_SUB == t_idx[None, :] // GLA_SUB)
                      & (t_idx[None, :] <= t_idx[:, None]), dtype=F32)
    segv = jnp.asarray((np.arange(KW) // GLA_KEY)[:, None] == (np.arange(VW) // GLA_VALUE)[None, :], dtype=BF16)
    full = lambda shp: pl.BlockSpec(shp, lambda i: (0,) * len(shp))
    return pl.pallas_call(
        _gla_kernel,
        grid=(T // tb,),
        in_specs=[pl.BlockSpec((tb, GLA_IN_PAD), lambda i: (i, 0)),
                  full((LANE, KW)), full((1, KW)), full((1, VW)), full((tb, tb)), full((KW, VW))],
        out_specs=pl.BlockSpec((tb, VW), lambda i: (i, 0)),
        out_shape=jax.ShapeDtypeStruct((T, VW), BF16),
        scratch_shapes=[pltpu.VMEM((GLA_HEADS, GLA_VALUE, GLA_KEY), F32),
                        pltpu.VMEM((tb, VW), F32)],
        compiler_params=_cparams(("arbitrary",)),
        name="gla",
    )(c_proj, g2, gb, gng, tri, segv)


def _merge_kernel(oa_ref, ob_ref, oc_ref, g_ref, pa_ref, pb_ref, pc_ref, o_ref):
    D = D_MODEL
    m = jax.nn.sigmoid(g_ref[:, 0:D]) * _dot(oa_ref[...], pa_ref[...])
    m = m + jax.nn.sigmoid(g_ref[:, D:2 * D]) * _dot(ob_ref[...], pb_ref[...])
    m = m + jax.nn.sigmoid(g_ref[:, 2 * D:3 * D]) * _dot(oc_ref[...], pc_ref[...])
    o_ref[...] = m.astype(o_ref.dtype)


def _merge(oa, ob, oc, gates, pa, pb, pc, tm):
    T = oa.shape[0]
    tm = min(tm, T)
    D = D_MODEL
    rowblk = lambda n: pl.BlockSpec((tm, n), lambda i: (i, 0))
    full = lambda shp: pl.BlockSpec(shp, lambda i: (0, 0))
    return pl.pallas_call(
        _merge_kernel,
        grid=(T // tm,),
        in_specs=[rowblk(RWKV_WIDTH), rowblk(DIFF_V), rowblk(GLA_V_WIDTH), rowblk(3 * D),
                  full((RWKV_WIDTH, D)), full((DIFF_V, D)), full((GLA_V_WIDTH, D))],
        out_specs=rowblk(D),
        out_shape=jax.ShapeDtypeStruct((T, D), BF16),
        compiler_params=_cparams(("arbitrary",)),
        name="merge",
    )(oa, ob, oc, gates, pa, pb, pc)


def _layer_norm(z, g, b):
    mu = jnp.mean(z, axis=1, keepdims=True)
    zc = z - mu
    var = jnp.mean(zc * zc, axis=1, keepdims=True)
    return zc * lax.rsqrt(var + LN_EPS) * g + b


def _mm_ln_kernel(a_ref, w_ref, res_ref, g_ref, b_ref, o_ref, ob_ref):
    z = DEEPNORM_ALPHA * res_ref[...] + _dot(a_ref[...], w_ref[...])
    out = _layer_norm(z, g_ref[...], b_ref[...])
    o_ref[...] = out
    ob_ref[...] = out.astype(BF16)


def _mm_ln(a, w, res, g, b, tm):
    T, K = a.shape
    D = D_MODEL
    tm = min(tm, T)
    rowblk = lambda n: pl.BlockSpec((tm, n), lambda i: (i, 0))
    full = lambda shp: pl.BlockSpec(shp, lambda i: (0, 0))
    return pl.pallas_call(
        _mm_ln_kernel,
        grid=(T // tm,),
        in_specs=[rowblk(K), full((K, D)), rowblk(D), full((1, D)), full((1, D))],
        out_specs=[rowblk(D), rowblk(D)],
        out_shape=[jax.ShapeDtypeStruct((T, D), F32), jax.ShapeDtypeStruct((T, D), BF16)],
        compiler_params=_cparams(("arbitrary",)),
        name="wo_ln",
    )(a, w, res, g, b)


def _router_kernel(x_ref, w_ref, b_ref, o_ref):
    G, E = MOE_GROUPS, MOE_EXPERTS
    logits = _dot(x_ref[...], w_ref[...], precision=HIGHEST) + b_ref[...]
    lane = lax.broadcasted_iota(jnp.int32, logits.shape, 1)
    neg = -jnp.inf
    big = jnp.int32(1 << 20)
    is_g = lane < G
    lg = jnp.where(is_g, logits, neg)
    gmax = jnp.max(lg, axis=1, keepdims=True)
    gidx = jnp.min(jnp.where(is_g & (lg == gmax), lane, big), axis=1, keepdims=True)
    g_p = 1.0 / jnp.sum(jnp.exp(lg - gmax), axis=1, keepdims=True)
    lo = G + gidx * E
    in_grp = (lane >= lo) & (lane < lo + E)
    le = jnp.where(in_grp, logits, neg)
    m1 = jnp.max(le, axis=1, keepdims=True)
    i1 = jnp.min(jnp.where(in_grp & (le == m1), lane, big), axis=1, keepdims=True)
    le2 = jnp.where(lane == i1, neg, le)
    m2 = jnp.max(le2, axis=1, keepdims=True)
    i2 = jnp.min(jnp.where(in_grp & (le2 == m2), lane, big), axis=1, keepdims=True)
    e2 = jnp.exp(m2 - m1)
    p1 = 1.0 / (1.0 + e2)
    p2 = e2 / (1.0 + e2)
    comb = jnp.where(lane == i1, g_p * p1, 0.0) + jnp.where(lane == i2, g_p * p2, 0.0)
    o_ref[...] = jnp.where(lane == 0, gidx.astype(F32), comb)


def _router(x1, wr, br, tm):
    T, D = x1.shape
    tm = min(tm, T)
    return pl.pallas_call(
        _router_kernel,
        grid=(T // tm,),
        in_specs=[pl.BlockSpec((tm, D), lambda i: (i, 0)),
                  pl.BlockSpec((D, LANE), lambda i: (0, 0)),
                  pl.BlockSpec((1, LANE), lambda i: (0, 0))],
        out_specs=pl.BlockSpec((tm, LANE), lambda i: (i, 0)),
        out_shape=jax.ShapeDtypeStruct((T, LANE), F32),
        compiler_params=_cparams(("arbitrary",)),
        name="router",
    )(x1, wr, br)


def _moe_tables(comb, tm):
    T = comb.shape[0]
    G = MOE_GROUPS
    nt = T // tm + G
    gid = comb[:, 0].astype(jnp.int32)
    order = jnp.argsort(gid).astype(jnp.int32)
    counts = jnp.sum((gid[:, None] == jnp.arange(G)[None, :]).astype(jnp.int32), axis=0)
    starts = jnp.cumsum(counts) - counts
    ntile = (counts + tm - 1) // tm
    tstart = jnp.cumsum(ntile) - ntile
    n = jnp.arange(nt, dtype=jnp.int32)
    tg = jnp.sum((n[:, None] >= tstart[None, 1:]).astype(jnp.int32), axis=1)
    k = n - tstart[tg]
    row0 = starts[tg] + k * tm
    nvalid = jnp.clip(counts[tg] - k * tm, 0, tm).astype(jnp.int32)
    rows = jnp.clip(row0[:, None] + jnp.arange(tm, dtype=jnp.int32)[None, :], 0, T - 1)
    idx = jnp.take(order, rows).reshape(nt, 1, tm)
    return tg.astype(jnp.int32), nvalid, idx


def _moe_kernel(tg_ref, nv_ref, idx_ref, x_hbm, comb_hbm, wg_ref, wu_ref, wd_ref, y_hbm,
                xs_ref, cs_ref, acc_ref, sem_ref):
    n = pl.program_id(0)
    e = pl.program_id(1)
    nv = nv_ref[n]
    tm = xs_ref.shape[0]

    def gather(r):
        tok = idx_ref[0, 0, r]
        return (pltpu.make_async_copy(x_hbm.at[pl.ds(tok, 1)], xs_ref.at[pl.ds(r, 1)], sem_ref.at[0]),
                pltpu.make_async_copy(comb_hbm.at[pl.ds(tok, 1)], cs_ref.at[pl.ds(r, 1)], sem_ref.at[1]))

    def scatter(r):
        tok = idx_ref[0, 0, r]
        return pltpu.make_async_copy(acc_ref.at[pl.ds(r, 1)], y_hbm.at[pl.ds(tok, 1)], sem_ref.at[2])

    @pl.when((e == 0) & (nv > 0))
    def _():
        def start(r, c):
            cx, cc = gather(r)
            cx.start()
            cc.start()
            return c

        def wait(r, c):
            cx, cc = gather(r)
            cx.wait()
            cc.wait()
            return c

        lax.fori_loop(0, tm, start, 0)
        lax.fori_loop(0, tm, wait, 0)

    @pl.when(nv > 0)
    def _():
        xb = xs_ref[...].astype(BF16)
        hg = _dot(xb, wg_ref[0])
        hu = _dot(xb, wu_ref[0])
        comb = cs_ref[...]
        lane = lax.broadcasted_iota(jnp.int32, comb.shape, 1)
        col = MOE_GROUPS + tg_ref[n] * MOE_EXPERTS + e
        cw = jnp.sum(jnp.where(lane == col, comb, 0.0), axis=1, keepdims=True)
        h = (jax.nn.silu(hg) * hu * cw).astype(BF16)
        y = _dot(h, wd_ref[0])

        @pl.when(e == 0)
        def _():
            acc_ref[...] = y

        @pl.when(e > 0)
        def _():
            acc_ref[...] += y

    @pl.when((e == MOE_EXPERTS - 1) & (nv > 0))
    def _():
        def start(r, c):
            scatter(r).start()
            return c

        def wait(r, c):
            scatter(r).wait()
            return c

        lax.fori_loop(0, nv, start, 0)
        lax.fori_loop(0, nv, wait, 0)


def _moe(x, comb, wg, wu, wd, tm):
    T, D = x.shape
    F = MOE_HIDDEN
    tm = min(tm, T)
    tg, nvalid, idx = _moe_tables(comb, tm)
    nt = tg.shape[0]
    wspec = lambda shp: pl.BlockSpec(shp, lambda n, e, tg_r, nv_r: (tg_r[n] * MOE_EXPERTS + e, 0, 0))
    grid_spec = pltpu.PrefetchScalarGridSpec(
        num_scalar_prefetch=2,
        grid=(nt, MOE_EXPERTS),
        in_specs=[pl.BlockSpec((1, 1, tm), lambda n, e, tg_r, nv_r: (n, 0, 0), memory_space=pltpu.SMEM),
                  pl.BlockSpec(memory_space=pl.ANY), pl.BlockSpec(memory_space=pl.ANY),
                  wspec((1, D, F)), wspec((1, D, F)), wspec((1, F, D))],
        out_specs=pl.BlockSpec(memory_space=pl.ANY),
        scratch_shapes=[pltpu.VMEM((tm, D), F32), pltpu.VMEM((tm, LANE), F32), pltpu.VMEM((tm, D), F32),
                        pltpu.SemaphoreType.DMA((3,))],
    )
    return pl.pallas_call(
        _moe_kernel,
        grid_spec=grid_spec,
        out_shape=jax.ShapeDtypeStruct((T, D), F32),
        compiler_params=_cparams(("arbitrary", "arbitrary")),
        name="moe_experts",
    )(tg, nvalid, idx, x, comb, wg, wu, wd)


def _res_ln_kernel(x_ref, y_ref, g_ref, b_ref, o_ref, ob_ref):
    out = _layer_norm(DEEPNORM_ALPHA * x_ref[...] + y_ref[...], g_ref[...], b_ref[...])
    o_ref[...] = out
    ob_ref[...] = out.astype(BF16)


def _res_ln(x, y, g, b, tm):
    T, D = x.shape
    tm = min(tm, T)
    rowblk = pl.BlockSpec((tm, D), lambda i: (i, 0))
    vec = pl.BlockSpec((1, D), lambda i: (0, 0))
    return pl.pallas_call(
        _res_ln_kernel,
        grid=(T // tm,),
        in_specs=[rowblk, rowblk, vec, vec],
        out_specs=[rowblk, rowblk],
        out_shape=[jax.ShapeDtypeStruct((T, D), F32), jax.ShapeDtypeStruct((T, D), BF16)],
        compiler_params=_cparams(("arbitrary",)),
        name="moe_ln",
    )(x, y, g, b)


def _pad_cols(w, n):
    return jnp.pad(w, ((0, 0), (0, n - w.shape[1])))


def _layer(x, xb, l, p, rope_tabs):
    T = x.shape[0]
    W = RWKV_WIDTH
    KW, VW = GLA_K_WIDTH, GLA_V_WIDTH
    lambda_init = 0.8 - 0.6 * math.exp(-0.3 * l)

    w_in = p['w_in'][l]
    o_b = RWKV_IN
    o_c = o_b + DIFF_IN
    o_g = o_c + GLA_IN
    lo1 = 3 * W + RWKV_DECAY_LORA
    lo2 = lo1 + RWKV_AAA_LORA
    w_a = jnp.concatenate([_pad_cols(w_in[:, :lo1], 3 * W + LORA_PAD), _pad_cols(w_in[:, lo1:lo2], LORA_PAD),
                           w_in[:, lo2:o_b]], axis=1).astype(BF16)
    mu = p['rwkv_mu'][l]
    mu_a = jnp.concatenate([jnp.pad(mu[:lo1], (0, LORA_PAD - RWKV_DECAY_LORA)),
                            jnp.pad(mu[lo1:lo2], (0, LORA_PAD - RWKV_AAA_LORA)), mu[lo2:]])[None, :]
    qscale = DIFF_HEAD ** -0.5 * math.log2(math.e)
    w_qk = jnp.concatenate([w_in[:, o_b:o_b + DIFF_QK] * qscale, w_in[:, o_b + DIFF_QK:o_b + 2 * DIFF_QK]],
                           axis=1).astype(BF16)
    w_vt = w_in[:, o_b + 2 * DIFF_QK:o_c].T.astype(BF16)
    c0 = o_c
    w_c = jnp.concatenate([w_in[:, c0:c0 + KW] * (GLA_KEY ** -0.5), w_in[:, c0 + KW:c0 + 2 * KW + VW],
                           w_in[:, c0 + 2 * KW + VW + GLA_GATE_LORA:o_g],
                           _pad_cols(w_in[:, c0 + 2 * KW + VW:c0 + 2 * KW + VW + GLA_GATE_LORA], LANE)],
                          axis=1).astype(BF16)
    w_g = w_in[:, o_g:].astype(BF16)

    proj_a = _matmul(xb, w_a, F32, 1024, 1024)
    qk = _matmul_rope(xb, w_qk, rope_tabs, 1024, 1024)
    tk = min(512, T)
    vt = _matmul_vt(w_vt, xb, tk)
    proj_c = _matmul(xb, w_c, F32, 512, GLA_IN_PAD)
    gates = _matmul(xb, w_g, F32, 1024, 1024)

    padr = lambda w: jnp.pad(w, ((0, LORA_PAD - w.shape[0]), (0, 0)))
    r2 = lambda v: v.reshape(1, -1)
    o_a = _rwkv(proj_a, mu_a, padr(p['rwkv_w2'][l]).astype(BF16), r2(p['rwkv_w0'][l]),
                padr(p['rwkv_a2'][l]).astype(BF16), r2(p['rwkv_a0'][l]), p['rwkv_g2'][l].astype(BF16),
                r2(p['rwkv_kk_scale'][l]), r2(p['rwkv_ka_scale'][l]), r2(p['rwkv_rk'][l]),
                r2(p['rwkv_gn_g'][l]), r2(p['rwkv_gn_b'][l]), 256)
    o_bb = _diff_attention(qk, vt, r2(p['diff_lq1'][l]), r2(p['diff_lk1'][l]), r2(p['diff_lq2'][l]),
                           r2(p['diff_lk2'][l]), p['diff_subln_g'][l].reshape(-1, 1), lambda_init, min(2 * tk, T))
    g2p = jnp.pad(p['gla_g2'][l], ((0, LANE - GLA_GATE_LORA), (0, 0)))
    o_c = _gla(proj_c, g2p, r2(p['gla_gb'][l]), r2(p['gla_gn_g'][l]), 256)

    merged = _merge(o_a, o_bb, o_c, gates, p['proj_a'][l].astype(BF16), p['proj_b'][l].astype(BF16),
                    p['proj_c'][l].astype(BF16), 256)
    x1, x1b = _mm_ln(merged, p['w_o'][l].astype(BF16), x, r2(p['ln1_g'][l]), r2(p['ln1_b'][l]), 256)

    wr = _pad_cols(jnp.concatenate([p['router_group_w'][l], p['router_expert_w'][l]], axis=1), LANE)
    br = _pad_cols(jnp.concatenate([p['router_group_b'][l], p['router_expert_b'][l]])[None, :], LANE)
    comb = _router(x1, wr, br, 512)
    F = MOE_HIDDEN
    wg = p['moe_w_gate'][l].reshape(N_EXPERTS, D_MODEL, F).astype(BF16)
    wu = p['moe_w_up'][l].reshape(N_EXPERTS, D_MODEL, F).astype(BF16)
    wd = p['moe_w_down'][l].reshape(N_EXPERTS, F, D_MODEL).astype(BF16)
    y = _moe(x1, comb, wg, wu, wd, MOE_TILE)
    return _res_ln(x1, y, r2(p['ln2_g'][l]), r2(p['ln2_b'][l]), 512)


def kernel(x, w_in, rwkv_mu, rwkv_w2, rwkv_w0, rwkv_a2, rwkv_a0, rwkv_g2, rwkv_kk_scale, rwkv_ka_scale, rwkv_rk, rwkv_gn_g, rwkv_gn_b, diff_lq1, diff_lk1, diff_lq2, diff_lk2, diff_subln_g, gla_g2, gla_gb, gla_gn_g, proj_a, proj_b, proj_c, w_o, ln1_g, ln1_b, router_group_w, router_group_b, router_expert_w, router_expert_b, moe_w_gate, moe_w_up, moe_w_down, ln2_g, ln2_b):
    p = dict(w_in=w_in, rwkv_mu=rwkv_mu, rwkv_w2=rwkv_w2, rwkv_w0=rwkv_w0, rwkv_a2=rwkv_a2, rwkv_a0=rwkv_a0,
             rwkv_g2=rwkv_g2, rwkv_kk_scale=rwkv_kk_scale, rwkv_ka_scale=rwkv_ka_scale, rwkv_rk=rwkv_rk,
             rwkv_gn_g=rwkv_gn_g, rwkv_gn_b=rwkv_gn_b, diff_lq1=diff_lq1, diff_lk1=diff_lk1, diff_lq2=diff_lq2,
             diff_lk2=diff_lk2, diff_subln_g=diff_subln_g, gla_g2=gla_g2, gla_gb=gla_gb, gla_gn_g=gla_gn_g,
             proj_a=proj_a, proj_b=proj_b, proj_c=proj_c, w_o=w_o, ln1_g=ln1_g, ln1_b=ln1_b,
             router_group_w=router_group_w, router_group_b=router_group_b, router_expert_w=router_expert_w,
             router_expert_b=router_expert_b, moe_w_gate=moe_w_gate, moe_w_up=moe_w_up, moe_w_down=moe_w_down,
             ln2_g=ln2_g, ln2_b=ln2_b)
    B, T, D = x.shape
    assert B == 1 and D == D_MODEL
    xf = x.reshape(T, D)
    xb = xf.astype(BF16)
    tabs = _rope_tables(T)
    for l in range(DEPTH):
        xf, xb = _layer(xf, xb, l, p, tabs)
    return xf.reshape(B, T, D)
```

```python
import functools
import math

import numpy as np
import jax
import jax.numpy as jnp
from jax import lax
from jax.experimental import pallas as pl
from jax.experimental.pallas import tpu as pltpu

F32 = jnp.float32
BF16 = jnp.bfloat16
HIGHEST = lax.Precision.HIGHEST

D_MODEL = 2048
DEPTH = 2
DEEPNORM_ALPHA = (2 * DEPTH) ** 0.25
LN_EPS = 1e-5

RWKV_HEADS = 8
RWKV_HEAD = 64
RWKV_WIDTH = RWKV_HEADS * RWKV_HEAD
RWKV_DECAY_LORA = 96
RWKV_AAA_LORA = 96
RWKV_GATE_LORA = 256
RWKV_GN_EPS = 64e-5
RWKV_IN = 3 * RWKV_WIDTH + RWKV_DECAY_LORA + RWKV_AAA_LORA + RWKV_GATE_LORA
RWKV_CHUNK = 64
LORA_PAD = 128
RWKV_IN_PAD = 3 * RWKV_WIDTH + 2 * LORA_PAD + RWKV_GATE_LORA

DIFF_HEADS = 8
DIFF_HEAD = 64
DIFF_QK = DIFF_HEADS * 2 * DIFF_HEAD
DIFF_V = DIFF_HEADS * 2 * DIFF_HEAD
DIFF_IN = 2 * DIFF_QK + DIFF_V
ROPE_THETA = 500000.0
ROPE_DIM = DIFF_HEAD // 4
ROPE_HALF = ROPE_DIM // 2

GLA_HEADS = 4
GLA_KEY = 64
GLA_VALUE = 128
GLA_K_WIDTH = GLA_HEADS * GLA_KEY
GLA_V_WIDTH = GLA_HEADS * GLA_VALUE
GLA_GATE_LORA = 16
GLA_TAU = 16.0
GLA_SUB = 16
GLA_IN = 2 * GLA_K_WIDTH + 2 * GLA_V_WIDTH + GLA_GATE_LORA
GLA_IN_PAD = 2 * GLA_K_WIDTH + 2 * GLA_V_WIDTH + 128

N_BRANCHES = 3
MOE_GROUPS = 4
MOE_EXPERTS = 8
MOE_HIDDEN = 256
N_EXPERTS = MOE_GROUPS * MOE_EXPERTS
MOE_TILE = 512
LANE = 128
V7X_VMEM_LIMIT = 56 * 1024 * 1024


def _cparams(sem):
    return pltpu.CompilerParams(dimension_semantics=sem, vmem_limit_bytes=V7X_VMEM_LIMIT)


def _nt(a, b, **kw):
    return lax.dot_general(a, b, (((1,), (1,)), ((), ())), preferred_element_type=F32, **kw)


def _tn(a, b, **kw):
    return lax.dot_general(a, b, (((0,), (0,)), ((), ())), preferred_element_type=F32, **kw)


def _dot(a, b, **kw):
    return jnp.dot(a, b, preferred_element_type=F32, **kw)


def _mm_kernel(a_ref, b_ref, o_ref, *, gate):
    acc = _dot(a_ref[...], b_ref[...])
    if gate:
        acc = jax.nn.sigmoid(acc)
    o_ref[...] = acc.astype(o_ref.dtype)


def _matmul(a, b, out_dtype, tm, tn, gate=False):
    M, K = a.shape
    N = b.shape[1]
    tm, tn = min(tm, M), min(tn, N)
    return pl.pallas_call(
        functools.partial(_mm_kernel, gate=gate),
        grid=(N // tn, M // tm),
        in_specs=[pl.BlockSpec((tm, K), lambda j, i: (i, 0)),
                  pl.BlockSpec((K, tn), lambda j, i: (0, j))],
        out_specs=pl.BlockSpec((tm, tn), lambda j, i: (i, j)),
        out_shape=jax.ShapeDtypeStruct((M, N), out_dtype),
        compiler_params=_cparams(("arbitrary", "arbitrary")),
        name="proj_mm",
    )(a, b)


def _mm_rope_kernel(a_ref, b_ref, c_ref, sa_ref, sb_ref, o_ref):
    acc = _dot(a_ref[...], b_ref[...])
    c, sa, sb = c_ref[...], sa_ref[...], sb_ref[...]
    for h in range(acc.shape[1] // LANE):
        blk = acc[:, h * LANE:(h + 1) * LANE]
        out = blk * c + pltpu.roll(blk, LANE - ROPE_HALF, 1) * sa + pltpu.roll(blk, ROPE_HALF, 1) * sb
        o_ref[:, h * LANE:(h + 1) * LANE] = out.astype(o_ref.dtype)


def _matmul_rope(a, b, tabs, tm, tn):
    M, K = a.shape
    N = b.shape[1]
    tm, tn = min(tm, M), min(tn, N)
    tab_spec = pl.BlockSpec((tm, LANE), lambda j, i: (i, 0))
    return pl.pallas_call(
        _mm_rope_kernel,
        grid=(N // tn, M // tm),
        in_specs=[pl.BlockSpec((tm, K), lambda j, i: (i, 0)),
                  pl.BlockSpec((K, tn), lambda j, i: (0, j)),
                  tab_spec, tab_spec, tab_spec],
        out_specs=pl.BlockSpec((tm, tn), lambda j, i: (i, j)),
        out_shape=jax.ShapeDtypeStruct((M, N), BF16),
        compiler_params=_cparams(("arbitrary", "arbitrary")),
        name="proj_qk_rope",
    )(a, b, *tabs)


def _mm_vt_kernel(wt_ref, x_ref, o_ref):
    o_ref[0] = _nt(wt_ref[...], x_ref[...]).astype(o_ref.dtype)


def _matmul_vt(wt, x, tk):
    N, K = wt.shape
    T = x.shape[0]
    return pl.pallas_call(
        _mm_vt_kernel,
        grid=(T // tk,),
        in_specs=[pl.BlockSpec((N, K), lambda i: (0, 0)),
                  pl.BlockSpec((tk, K), lambda i: (i, 0))],
        out_specs=pl.BlockSpec((1, N, tk), lambda i: (i, 0, 0)),
        out_shape=jax.ShapeDtypeStruct((T // tk, N, tk), BF16),
        compiler_params=_cparams(("arbitrary",)),
        name="proj_vt",
    )(wt, x)


def _rope_tables(T):
    pos = jnp.arange(T, dtype=F32)[:, None]
    inv_freq = ROPE_THETA ** (-jnp.arange(ROPE_HALF, dtype=F32) / ROPE_HALF)
    lane = np.arange(LANE) % DIFF_HEAD
    ang = pos * inv_freq[None, :]
    cos8, sin8 = jnp.cos(ang), jnp.sin(ang)
    idx = jnp.asarray(lane % ROPE_HALF)
    cos_l, sin_l = cos8[:, idx], sin8[:, idx]
    first = jnp.asarray(lane < ROPE_HALF)[None, :]
    second = jnp.asarray((lane >= ROPE_HALF) & (lane < ROPE_DIM))[None, :]
    c = jnp.where(first | second, cos_l, 1.0)
    sa = jnp.where(first, -sin_l, 0.0)
    sb = jnp.where(second, sin_l, 0.0)
    return c, sa, sb


def _rwkv_kernel(a_ref, mu_ref, w2_ref, w0_ref, a2_ref, a0_ref, g2_ref, kks_ref, kas_ref, rk_ref,
                 gng_ref, gnb_ref, seg_ref, tri_ref, o_ref,
                 prev_ref, h_ref, at_ref, rt_ref, bt_ref, kt_ref, bp_ref, kp_ref, v_ref, gc_ref, y_ref):
    TB = a_ref.shape[0]
    C, N, W = RWKV_CHUNK, RWKV_HEAD, RWKV_WIDTH

    @pl.when(pl.program_id(0) == 0)
    def _():
        prev_ref[...] = jnp.zeros_like(prev_ref)
        h_ref[...] = jnp.zeros_like(h_ref)

    a = a_ref[...]
    row = lax.broadcasted_iota(jnp.int32, a.shape, 0)
    shifted = jnp.where(row == 0, prev_ref[...], pltpu.roll(a, 1, 0))
    prev_ref[...] = a[TB - 1:TB, :]
    xs = a + (shifted - a) * mu_ref[...]

    r = xs[:, 0:W]
    k = xs[:, W:2 * W]
    v = xs[:, 2 * W:3 * W]
    w_lo = xs[:, 3 * W:3 * W + LORA_PAD]
    a_lo = xs[:, 3 * W + LORA_PAD:3 * W + 2 * LORA_PAD]
    g_lo = xs[:, 3 * W + 2 * LORA_PAD:]

    z = w0_ref[...] + _dot(jnp.tanh(w_lo).astype(BF16), w2_ref[...])
    w = -jax.nn.softplus(-z) - 0.5
    lw = -jnp.exp(w)
    aa = jax.nn.sigmoid(a0_ref[...] + _dot(a_lo.astype(BF16), a2_ref[...]))
    g = _dot(jax.nn.sigmoid(g_lo).astype(BF16), g2_ref[...])

    seg = seg_ref[...]
    kk = k * kks_ref[...]
    ss = _dot((kk * kk).astype(BF16), seg)
    kk = kk * lax.rsqrt(jnp.maximum(ss, 1e-24))
    km = k * (1.0 + (aa - 1.0) * kas_ref[...])
    beta = kk * aa

    tri = tri_ref[...]
    cum = _dot(tri, lw, precision=HIGHEST)
    rowc = lax.broadcasted_iota(jnp.int32, (TB, TB), 0) // C
    colc = lax.broadcasted_iota(jnp.int32, (TB, TB), 1) // C
    allc = jnp.where(rowc == colc, 1.0, 0.0).astype(F32)
    cum_c = _dot(allc, lw, precision=HIGHEST)

    e_in = jnp.exp(cum)
    e_neg = jnp.exp(-cum)
    e_end = jnp.exp(cum_c - cum)
    at_ref[...] = (-kk * jnp.exp(cum - lw)).astype(BF16)
    rt_ref[...] = (r * e_in).astype(BF16)
    bt_ref[...] = (beta * e_neg).astype(BF16)
    kt_ref[...] = (km * e_neg).astype(BF16)
    bp_ref[...] = (beta * e_end).astype(BF16)
    kp_ref[...] = (km * e_end).astype(BF16)
    v_ref[...] = v.astype(BF16)
    gc_ref[...] = jnp.exp(cum_c)

    ri = lax.broadcasted_iota(jnp.int32, (C, C), 0)
    ci = lax.broadcasted_iota(jnp.int32, (C, C), 1)
    strict = ci < ri
    incl = ci <= ri
    eye = ci == ri

    def chunk(c, carry):
        rows = pl.ds(pl.multiple_of(c * C, C), C)
        HS = range(RWKV_HEADS)
        sl = [slice(h * N, (h + 1) * N) for h in HS]
        cat = lambda a, b: jnp.concatenate([a, b], axis=0)
        at = [at_ref[rows, L] for L in sl]
        rt = [rt_ref[rows, L] for L in sl]
        bt = [bt_ref[rows, L] for L in sl]
        kt = [kt_ref[rows, L] for L in sl]
        bp = [bp_ref[rows, L] for L in sl]
        kp = [kp_ref[rows, L] for L in sl]
        vh = [v_ref[rows, L] for L in sl]
        ar = [cat(at[h], rt[h]) for h in HS]
        xb = [_nt(ar[h], bt[h]) for h in HS]
        xk = [_nt(ar[h], kt[h]) for h in HS]
        a_ab = [jnp.where(strict, xb[h][:C], 0.0) for h in HS]
        a_rb = [jnp.where(incl, xb[h][C:], 0.0).astype(BF16) for h in HS]
        a_ak = [jnp.where(strict, xk[h][:C], 0.0).astype(BF16) for h in HS]
        a_rk = [jnp.where(incl, xk[h][C:], 0.0).astype(BF16) for h in HS]
        ident = jnp.where(eye, 1.0, 0.0)
        t_inv = [ident + a_ab[h] for h in HS]
        ap = a_ab
        for _ in range(5):
            apb = [ap[h].astype(BF16) for h in HS]
            ap = [_dot(apb[h], apb[h]) for h in HS]
            t_inv = [t_inv[h] + _dot(t_inv[h].astype(BF16), ap[h].astype(BF16)) for h in HS]
        tb = [t_inv[h].astype(BF16) for h in HS]
        wv = [_dot(a_ak[h], vh[h]).astype(BF16) for h in HS]
        pb = [_dot(tb[h], at[h]).astype(BF16) for h in HS]
        u0b = [_dot(tb[h], wv[h]).astype(BF16) for h in HS]
        qm = [rt[h].astype(F32) + _dot(a_rb[h], pb[h]) for h in HS]
        y0 = [_dot(a_rb[h], u0b[h]) + _dot(a_rk[h], vh[h]) for h in HS]
        gm = [jnp.where(eye, gc_ref[rows, sl[h]][0:1, :], 0.0) + _tn(bp[h], pb[h]) for h in HS]
        hadd = [_tn(cat(bp[h], kp[h]), cat(u0b[h], vh[h])) for h in HS]
        for h in HS:
            z = _dot(cat(qm[h], gm[h]).astype(BF16), h_ref[h].astype(BF16))
            y_ref[rows, sl[h]] = z[:C] + y0[h]
            h_ref[h] = z[C:] + hadd[h]
        return carry

    lax.fori_loop(0, TB // C, chunk, 0)

    y = y_ref[...]
    segm = seg_ref[...]
    mu = _dot(y.astype(BF16), segm) * (1.0 / N)
    yc = y - mu
    var = _dot((yc * yc).astype(BF16), segm) * (1.0 / N)
    yn = yc * lax.rsqrt(var + RWKV_GN_EPS) * gng_ref[...] + gnb_ref[...]
    bonus = _dot((r * km * rk_ref[...]).astype(BF16), segm) * v
    o_ref[...] = ((yn + bonus) * g).astype(o_ref.dtype)


def _rwkv(a_proj, mu, w2, w0, a2, a0, g2, kks, kas, rk, gng, gnb, tb):
    T = a_proj.shape[0]
    tb = min(tb, T)
    W = RWKV_WIDTH
    hid = np.arange(W) // RWKV_HEAD
    seg = jnp.asarray(hid[:, None] == hid[None, :], dtype=BF16)
    t_idx = np.arange(tb)
    tri = jnp.asarray((t_idx[:, None] // RWKV_CHUNK == t_idx[None, :] // RWKV_CHUNK)
                      & (t_idx[None, :] <= t_idx[:, None]), dtype=F32)
    full = lambda shp: pl.BlockSpec(shp, lambda i: (0,) * len(shp))
    row = lambda n: full((1, n))
    bf = lambda: pltpu.VMEM((tb, W), BF16)
    return pl.pallas_call(
        _rwkv_kernel,
        grid=(T // tb,),
        in_specs=[pl.BlockSpec((tb, RWKV_IN_PAD), lambda i: (i, 0)),
                  row(RWKV_IN_PAD), full((LORA_PAD, W)), row(W), full((LORA_PAD, W)), row(W),
                  full((RWKV_GATE_LORA, W)), row(W), row(W), row(W), row(W), row(W),
                  full((W, W)), full((tb, tb))],
        out_specs=pl.BlockSpec((tb, W), lambda i: (i, 0)),
        out_shape=jax.ShapeDtypeStruct((T, W), BF16),
        scratch_shapes=[pltpu.VMEM((1, RWKV_IN_PAD), F32),
                        pltpu.VMEM((RWKV_HEADS, RWKV_HEAD, RWKV_HEAD), F32),
                        bf(), bf(), bf(), bf(), bf(), bf(), bf(),
                        pltpu.VMEM((tb, W), F32), pltpu.VMEM((tb, W), F32)],
        compiler_params=_cparams(("arbitrary",)),
        name="rwkv7",
    )(a_proj, mu, w2, w0, a2, a0, g2, kks, kas, rk, gng, gnb, seg, tri)


def _diff_kernel(q_ref, k_ref, vt_ref, lq1_ref, lk1_ref, lq2_ref, lk2_ref, sg_ref, o_ref,
                 m_ref, l_ref, acc_ref, s_ref, p_ref, a_ref, *, lambda_init):
    tq = q_ref.shape[0]
    nkb, _, tk = vt_ref.shape
    i = pl.program_id(1)
    q = q_ref[...]
    lane = lax.broadcasted_iota(jnp.int32, q.shape, 1)
    qm = (jnp.where(lane < DIFF_HEAD, q, jnp.zeros_like(q)),
          jnp.where(lane >= DIFF_HEAD, q, jnp.zeros_like(q)))
    m_ref[...] = jnp.full(m_ref.shape, -jnp.inf, F32)
    l_ref[...] = jnp.zeros_like(l_ref)
    acc_ref[...] = jnp.zeros_like(acc_ref)
    p_ref[1] = jnp.zeros(p_ref.shape[1:], BF16)
    a_ref[1] = jnp.ones(a_ref.shape[1:], F32)

    def scores(j, slot):
        jc = jnp.minimum(j, nkb - 1)
        kj = k_ref[pl.ds(pl.multiple_of(jc * tk, tk), tk), :]
        for mi in range(2):
            s_ref[slot, mi] = _nt(kj, qm[mi])

    def values(j, slot):
        vj = vt_ref[jnp.clip(j, 0, nkb - 1)]
        for mi in range(2):
            acc_ref[mi] = a_ref[slot, mi] * acc_ref[mi] + _dot(vj, p_ref[slot, mi])

    def softmax(j, slot, masked):
        for mi in range(2):
            s = s_ref[slot, mi]
            if masked:
                d = (lax.broadcasted_iota(jnp.int32, (tk, tq), 0)
                     - lax.broadcasted_iota(jnp.int32, (tk, tq), 1))
                s = jnp.where(d <= i * tq - j * tk, s, -jnp.inf)
            m_old = m_ref[mi]
            m_new = jnp.maximum(m_old, jnp.max(s, axis=0, keepdims=True))
            alpha = jnp.exp2(m_old - m_new)
            p = jnp.exp2(s - m_new)
            l_ref[mi] = alpha * l_ref[mi] + jnp.sum(p, axis=0, keepdims=True)
            m_ref[mi] = m_new
            a_ref[slot, mi] = alpha
            p_ref[slot, mi] = p.astype(BF16)

    def half(j, cur, masked):
        nxt = 1 - cur
        scores(j + 1, nxt)
        values(j - 1, nxt)
        softmax(j, cur, masked)

    def pair(jj, carry):
        half(2 * jj, 0, False)
        half(2 * jj + 1, 1, False)
        return carry

    scores(0, 0)
    n_pairs = ((i + 1) * (tq // tk) + 1) // 2
    lax.fori_loop(0, n_pairs - 1, pair, 0)
    j_last = 2 * (n_pairs - 1)
    half(j_last, 0, True)
    half(j_last + 1, 1, True)
    values(j_last + 1, 1)

    lam = (jnp.exp(jnp.sum(lq1_ref[...] * lk1_ref[...])) - jnp.exp(jnp.sum(lq2_ref[...] * lk2_ref[...]))
           + lambda_init)
    o = acc_ref[0] / l_ref[0] - lam * (acc_ref[1] / l_ref[1])
    o = o * lax.rsqrt(jnp.mean(o * o, axis=0, keepdims=True) + LN_EPS) * (1.0 - lambda_init)
    o = o * sg_ref[...]
    o_ref[...] = o.T.astype(o_ref.dtype)


def _diff_attention(qk, vt, lq1, lk1, lq2, lk2, subln_g, lambda_init, tq):
    T = qk.shape[0]
    tk = vt.shape[2]
    assert tq % tk == 0 and T % tq == 0
    H, E = DIFF_HEADS, 2 * DIFF_HEAD
    vec = lambda: pl.BlockSpec((1, DIFF_HEAD), lambda h, i: (0, 0))
    return pl.pallas_call(
        functools.partial(_diff_kernel, lambda_init=lambda_init),
        grid=(H, T // tq),
        in_specs=[pl.BlockSpec((tq, E), lambda h, i: (i, h)),
                  pl.BlockSpec((T, E), lambda h, i: (0, H + h)),
                  pl.BlockSpec((T // tk, E, tk), lambda h, i: (0, h, 0)),
                  vec(), vec(), vec(), vec(),
                  pl.BlockSpec((E, 1), lambda h, i: (0, 0))],
        out_specs=pl.BlockSpec((tq, E), lambda h, i: (i, h)),
        out_shape=jax.ShapeDtypeStruct((T, DIFF_V), BF16),
        scratch_shapes=[pltpu.VMEM((2, 1, tq), F32), pltpu.VMEM((2, 1, tq), F32),
                        pltpu.VMEM((2, E, tq), F32), pltpu.VMEM((2, 2, tk, tq), F32),
                        pltpu.VMEM((2, 2, tk, tq), BF16), pltpu.VMEM((2, 2, 1, tq), F32)],
        compiler_params=_cparams(("arbitrary", "arbitrary")),
        name="diff_attn",
    )(qk, qk, vt, lq1, lk1, lq2, lk2, subln_g)


def _gla_kernel(c_ref, g2_ref, gb_ref, gng_ref, tri_ref, segv_ref, o_ref,
                st_ref, oacc_ref):
    TB = c_ref.shape[0]
    KW, VW, S = GLA_K_WIDTH, GLA_V_WIDTH, GLA_SUB
    DK, DV = GLA_KEY, GLA_VALUE

    @pl.when(pl.program_id(0) == 0)
    def _():
        st_ref[...] = jnp.zeros_like(st_ref)

    g_lo = c_ref[:, 2 * KW + 2 * VW:]
    gate = _dot(g_lo, g2_ref[...], precision=HIGHEST) + gb_ref[...]
    log_a = jax.nn.log_sigmoid(gate) * (1.0 / GLA_TAU)
    b = _dot(tri_ref[...], log_a, precision=HIGHEST)
    NS = TB // S
    b3 = b.reshape(NS, S, KW)
    q3 = c_ref[:, 0:KW].reshape(NS, S, KW)
    k3 = c_ref[:, KW:2 * KW].reshape(NS, S, KW)
    v = c_ref[:, 2 * KW:2 * KW + VW]
    v3 = v.reshape(NS, S, VW)
    bl3 = b3[:, S - 1:S, :]
    qb = (q3 * jnp.exp(b3)).astype(BF16).reshape(TB, KW)
    kb = (k3 * jnp.exp(bl3 - b3)).astype(BF16).reshape(TB, KW)
    dec = jnp.exp(bl3)
    vb = v.astype(BF16)

    ri = lax.broadcasted_iota(jnp.int32, (NS, S, 1), 1)
    segv = segv_ref[...]
    acc = jnp.zeros((NS, S, VW), F32)
    for j in range(S):
        e = jnp.exp(jnp.minimum(b3 - b3[:, j:j + 1, :], 0.0))
        t = (q3 * e * k3[:, j:j + 1, :]).astype(BF16).reshape(TB, KW)
        att = _dot(t, segv).reshape(NS, S, VW)
        acc = acc + jnp.where(ri >= j, att, 0.0) * v3[:, j:j + 1, :]
    oacc_ref[...] = acc.reshape(TB, VW)

    for h in range(GLA_HEADS):
        lk = slice(h * DK, (h + 1) * DK)
        lv = slice(h * DV, (h + 1) * DV)
        kv = [_tn(vb[s * S:(s + 1) * S, lv], kb[s * S:(s + 1) * S, lk]) for s in range(NS)]
        st = st_ref[h]
        for s in range(NS):
            rows = slice(s * S, (s + 1) * S)
            oacc_ref[rows, lv] += _nt(qb[rows, lk], st.astype(BF16))
            st = st * dec[s][:, lk] + kv[s]
        st_ref[h] = st

    o = oacc_ref[...]
    rgate = c_ref[:, 2 * KW + VW:2 * KW + 2 * VW]
    for h in range(GLA_HEADS):
        lv = slice(h * DV, (h + 1) * DV)
        oh = o[:, lv]
        oh = oh * lax.rsqrt(jnp.mean(oh * oh, axis=1, keepdims=True) + LN_EPS) * gng_ref[:, lv]
        o_ref[:, lv] = (oh * jax.nn.silu(rgate[:, lv])).astype(o_ref.dtype)


def _gla(c_proj, g2, gb, gng, tb):
    T = c_proj.shape[0]
    tb = min(tb, T)
    KW, VW = GLA_K_WIDTH, GLA_V_WIDTH
    t_idx = np.arange(tb)
    tri = jnp.asarray((t_idx[:, None] // GLA_SUB == t_idx[None, :] // GLA_SUB)
                      & (t_idx[None, :] <= t_idx[:, None]), dtype=F32)
    segv = jnp.asarray((np.arange(KW) // GLA_KEY)[:, None] == (np.arange(VW) // GLA_VALUE)[None, :], dtype=BF16)
    full = lambda shp: pl.BlockSpec(shp, lambda i: (0,) * len(shp))
    return pl.pallas_call(
        _gla_kernel,
        grid=(T // tb,),
        in_specs=[pl.BlockSpec((tb, GLA_IN_PAD), lambda i: (i, 0)),
                  full((LANE, KW)), full((1, KW)), full((1, VW)), full((tb, tb)), full((KW, VW))],
        out_specs=pl.BlockSpec((tb, VW), lambda i: (i, 0)),
        out_shape=jax.ShapeDtypeStruct((T, VW), BF16),
        scratch_shapes=[pltpu.VMEM((GLA_HEADS, GLA_VALUE, GLA_KEY), F32),
                        pltpu.VMEM((tb, VW), F32)],
        compiler_params=_cparams(("arbitrary",)),
        name="gla",
    )(c_proj, g2, gb, gng, tri, segv)


def _merge_kernel(oa_ref, ob_ref, oc_ref, g_ref, pa_ref, pb_ref, pc_ref, o_ref):
    D = D_MODEL
    m = g_ref[:, 0:D].astype(F32) * _dot(oa_ref[...], pa_ref[...])
    m = m + g_ref[:, D:2 * D].astype(F32) * _dot(ob_ref[...], pb_ref[...])
    m = m + g_ref[:, 2 * D:3 * D].astype(F32) * _dot(oc_ref[...], pc_ref[...])
    o_ref[...] = m.astype(o_ref.dtype)


def _merge(oa, ob, oc, gates, pa, pb, pc, tm):
    T = oa.shape[0]
    tm = min(tm, T)
    D = D_MODEL
    rowblk = lambda n: pl.BlockSpec((tm, n), lambda i: (i, 0))
    full = lambda shp: pl.BlockSpec(shp, lambda i: (0, 0))
    return pl.pallas_call(
        _merge_kernel,
        grid=(T // tm,),
        in_specs=[rowblk(RWKV_WIDTH), rowblk(DIFF_V), rowblk(GLA_V_WIDTH), rowblk(3 * D),
                  full((RWKV_WIDTH, D)), full((DIFF_V, D)), full((GLA_V_WIDTH, D))],
        out_specs=rowblk(D),
        out_shape=jax.ShapeDtypeStruct((T, D), BF16),
        compiler_params=_cparams(("arbitrary",)),
        name="merge",
    )(oa, ob, oc, gates, pa, pb, pc)


def _layer_norm(z, g, b):
    mu = jnp.mean(z, axis=1, keepdims=True)
    zc = z - mu
    var = jnp.mean(zc * zc, axis=1, keepdims=True)
    return zc * lax.rsqrt(var + LN_EPS) * g + b


def _mm_ln_kernel(a_ref, w_ref, res_ref, g_ref, b_ref, o_ref, ob_ref):
    z = DEEPNORM_ALPHA * res_ref[...] + _dot(a_ref[...], w_ref[...])
    out = _layer_norm(z, g_ref[...], b_ref[...])
    o_ref[...] = out
    ob_ref[...] = out.astype(BF16)


def _mm_ln(a, w, res, g, b, tm):
    T, K = a.shape
    D = D_MODEL
    tm = min(tm, T)
    rowblk = lambda n: pl.BlockSpec((tm, n), lambda i: (i, 0))
    full = lambda shp: pl.BlockSpec(shp, lambda i: (0, 0))
    return pl.pallas_call(
        _mm_ln_kernel,
        grid=(T // tm,),
        in_specs=[rowblk(K), full((K, D)), rowblk(D), full((1, D)), full((1, D))],
        out_specs=[rowblk(D), rowblk(D)],
        out_shape=[jax.ShapeDtypeStruct((T, D), F32), jax.ShapeDtypeStruct((T, D), BF16)],
        compiler_params=_cparams(("arbitrary",)),
        name="wo_ln",
    )(a, w, res, g, b)


def _router_kernel(x_ref, w_ref, b_ref, o_ref):
    G, E = MOE_GROUPS, MOE_EXPERTS
    logits = _dot(x_ref[...], w_ref[...], precision=HIGHEST) + b_ref[...]
    lane = lax.broadcasted_iota(jnp.int32, logits.shape, 1)
    neg = -jnp.inf
    big = jnp.int32(1 << 20)
    is_g = lane < G
    lg = jnp.where(is_g, logits, neg)
    gmax = jnp.max(lg, axis=1, keepdims=True)
    gidx = jnp.min(jnp.where(is_g & (lg == gmax), lane, big), axis=1, keepdims=True)
    g_p = 1.0 / jnp.sum(jnp.exp(lg - gmax), axis=1, keepdims=True)
    lo = G + gidx * E
    in_grp = (lane >= lo) & (lane < lo + E)
    le = jnp.where(in_grp, logits, neg)
    m1 = jnp.max(le, axis=1, keepdims=True)
    i1 = jnp.min(jnp.where(in_grp & (le == m1), lane, big), axis=1, keepdims=True)
    le2 = jnp.where(lane == i1, neg, le)
    m2 = jnp.max(le2, axis=1, keepdims=True)
    i2 = jnp.min(jnp.where(in_grp & (le2 == m2), lane, big), axis=1, keepdims=True)
    e2 = jnp.exp(m2 - m1)
    p1 = 1.0 / (1.0 + e2)
    p2 = e2 / (1.0 + e2)
    comb = jnp.where(lane == i1, g_p * p1, 0.0) + jnp.where(lane == i2, g_p * p2, 0.0)
    o_ref[...] = jnp.where(lane == 0, gidx.astype(F32), comb)


def _router(x1, wr, br, tm):
    T, D = x1.shape
    tm = min(tm, T)
    return pl.pallas_call(
        _router_kernel,
        grid=(T // tm,),
        in_specs=[pl.BlockSpec((tm, D), lambda i: (i, 0)),
                  pl.BlockSpec((D, LANE), lambda i: (0, 0)),
                  pl.BlockSpec((1, LANE), lambda i: (0, 0))],
        out_specs=pl.BlockSpec((tm, LANE), lambda i: (i, 0)),
        out_shape=jax.ShapeDtypeStruct((T, LANE), F32),
        compiler_params=_cparams(("arbitrary",)),
        name="router",
    )(x1, wr, br)


def _moe_tables(comb, tm):
    T = comb.shape[0]
    G = MOE_GROUPS
    nt = T // tm + G
    gid = comb[:, 0].astype(jnp.int32)
    order = jnp.argsort(gid).astype(jnp.int32)
    counts = jnp.sum((gid[:, None] == jnp.arange(G)[None, :]).astype(jnp.int32), axis=0)
    starts = jnp.cumsum(counts) - counts
    ntile = (counts + tm - 1) // tm
    tstart = jnp.cumsum(ntile) - ntile
    n = jnp.arange(nt, dtype=jnp.int32)
    tg = jnp.sum((n[:, None] >= tstart[None, 1:]).astype(jnp.int32), axis=1)
    k = n - tstart[tg]
    row0 = starts[tg] + k * tm
    nvalid = jnp.clip(counts[tg] - k * tm, 0, tm).astype(jnp.int32)
    rows = jnp.clip(row0[:, None] + jnp.arange(tm, dtype=jnp.int32)[None, :], 0, T - 1)
    idx = jnp.take(order, rows).reshape(nt, 1, tm)
    return tg.astype(jnp.int32), nvalid, idx


def _moe_kernel(tg_ref, nv_ref, idx_ref, idxn_ref, x_hbm, comb_hbm, wg_ref, wu_ref, wd_ref, y_hbm,
                xs_ref, cs_ref, acc_ref, yb_ref, pend_ref, gx_sem, gc_sem, sc_sem):
    n = pl.program_id(0)
    e = pl.program_id(1)
    nt = pl.num_programs(0)
    nv = nv_ref[n]
    slot = n % 2
    tm = acc_ref.shape[0]
    part = tm // MOE_EXPERTS

    def start_gathers(idx, dst_slot, r0, count):
        def body(r, c):
            tok = idx[0, 0, r0 + r]
            pltpu.make_async_copy(x_hbm.at[pl.ds(tok, 1)], xs_ref.at[dst_slot, pl.ds(r0 + r, 1)],
                                  gx_sem.at[dst_slot]).start()
            pltpu.make_async_copy(comb_hbm.at[pl.ds(tok, 1)], cs_ref.at[dst_slot, pl.ds(r0 + r, 1)],
                                  gc_sem.at[dst_slot]).start()
            return c
        lax.fori_loop(0, count, body, 0)

    def scatter(r, s):
        tok = idx_ref[0, 0, r]
        return pltpu.make_async_copy(yb_ref.at[s, pl.ds(r, 1)], y_hbm.at[pl.ds(tok, 1)], sc_sem.at[s])

    def drain(s):
        def body(r, c):
            pltpu.make_async_copy(yb_ref.at[s, pl.ds(0, 1)], y_hbm.at[pl.ds(0, 1)], sc_sem.at[s]).wait()
            return c
        lax.fori_loop(0, pend_ref[s], body, 0)
        pend_ref[s] = 0

    @pl.when((n == 0) & (e == 0))
    def _():
        pend_ref[0] = 0
        pend_ref[1] = 0

        @pl.when(nv > 0)
        def _():
            start_gathers(idx_ref, 0, 0, tm)

    @pl.when((e == 0) & (nv > 0))
    def _():
        pltpu.make_async_copy(x_hbm.at[pl.ds(0, tm)], xs_ref.at[slot], gx_sem.at[slot]).wait()
        pltpu.make_async_copy(comb_hbm.at[pl.ds(0, tm)], cs_ref.at[slot], gc_sem.at[slot]).wait()

    nxt = jnp.minimum(n + 1, nt - 1)

    @pl.when((n + 1 < nt) & (nv_ref[nxt] > 0))
    def _():
        start_gathers(idxn_ref, 1 - slot, e * part, part)

    @pl.when(nv > 0)
    def _():
        xb = xs_ref[slot].astype(BF16)
        hg = _dot(xb, wg_ref[0].astype(BF16))
        hu = _dot(xb, wu_ref[0].astype(BF16))
        comb = cs_ref[slot]
        lane = lax.broadcasted_iota(jnp.int32, comb.shape, 1)
        col = MOE_GROUPS + tg_ref[n] * MOE_EXPERTS + e
        cw = jnp.sum(jnp.where(lane == col, comb, 0.0), axis=1, keepdims=True)
        h = (jax.nn.silu(hg) * hu * cw).astype(BF16)
        y = _dot(h, wd_ref[0].astype(BF16))

        @pl.when(e == 0)
        def _():
            acc_ref[...] = y

        @pl.when((e > 0) & (e < MOE_EXPERTS - 1))
        def _():
            acc_ref[...] += y

        @pl.when(e == MOE_EXPERTS - 1)
        def _():
            yb_ref[slot] = acc_ref[...] + y

    @pl.when(e == MOE_EXPERTS - 1)
    def _():
        drain(1 - slot)

        @pl.when(nv > 0)
        def _():
            def body(r, c):
                scatter(r, slot).start()
                return c
            lax.fori_loop(0, nv, body, 0)
            pend_ref[slot] = nv

        @pl.when(n == nt - 1)
        def _():
            drain(slot)


def _moe(x, comb, wg, wu, wd, tm):
    T, D = x.shape
    F = MOE_HIDDEN
    tm = min(tm, T)
    tg, nvalid, idx = _moe_tables(comb, tm)
    nt = tg.shape[0]
    wspec = lambda shp: pl.BlockSpec(shp, lambda n, e, tg_r, nv_r: (tg_r[n] * MOE_EXPERTS + e, 0, 0))
    ispec = lambda f: pl.BlockSpec((1, 1, tm), f, memory_space=pltpu.SMEM)
    grid_spec = pltpu.PrefetchScalarGridSpec(
        num_scalar_prefetch=2,
        grid=(nt, MOE_EXPERTS),
        in_specs=[ispec(lambda n, e, tg_r, nv_r: (n, 0, 0)),
                  ispec(lambda n, e, tg_r, nv_r: (jnp.minimum(n + 1, nt - 1), 0, 0)),
                  pl.BlockSpec(memory_space=pl.ANY), pl.BlockSpec(memory_space=pl.ANY),
                  wspec((1, D, F)), wspec((1, D, F)), wspec((1, F, D))],
        out_specs=pl.BlockSpec(memory_space=pl.ANY),
        scratch_shapes=[pltpu.VMEM((2, tm, D), F32), pltpu.VMEM((2, tm, LANE), F32), pltpu.VMEM((tm, D), F32),
                        pltpu.VMEM((2, tm, D), F32), pltpu.SMEM((2,), jnp.int32),
                        pltpu.SemaphoreType.DMA((2,)), pltpu.SemaphoreType.DMA((2,)),
                        pltpu.SemaphoreType.DMA((2,))],
    )
    return pl.pallas_call(
        _moe_kernel,
        grid_spec=grid_spec,
        out_shape=jax.ShapeDtypeStruct((T, D), F32),
        compiler_params=_cparams(("arbitrary", "arbitrary")),
        name="moe_experts",
    )(tg, nvalid, idx, idx, x, comb, wg, wu, wd)


def _res_ln_kernel(x_ref, y_ref, g_ref, b_ref, o_ref, ob_ref):
    out = _layer_norm(DEEPNORM_ALPHA * x_ref[...] + y_ref[...], g_ref[...], b_ref[...])
    o_ref[...] = out
    ob_ref[...] = out.astype(BF16)


def _res_ln(x, y, g, b, tm):
    T, D = x.shape
    tm = min(tm, T)
    rowblk = pl.BlockSpec((tm, D), lambda i: (i, 0))
    vec = pl.BlockSpec((1, D), lambda i: (0, 0))
    return pl.pallas_call(
        _res_ln_kernel,
        grid=(T // tm,),
        in_specs=[rowblk, rowblk, vec, vec],
        out_specs=[rowblk, rowblk],
        out_shape=[jax.ShapeDtypeStruct((T, D), F32), jax.ShapeDtypeStruct((T, D), BF16)],
        compiler_params=_cparams(("arbitrary",)),
        name="moe_ln",
    )(x, y, g, b)


def _pad_cols(w, n):
    return jnp.pad(w, ((0, 0), (0, n - w.shape[1])))


def _layer(x, xb, l, p, rope_tabs):
    T = x.shape[0]
    W = RWKV_WIDTH
    KW, VW = GLA_K_WIDTH, GLA_V_WIDTH
    lambda_init = 0.8 - 0.6 * math.exp(-0.3 * l)

    w_in = p['w_in'][l]
    o_b = RWKV_IN
    o_c = o_b + DIFF_IN
    o_g = o_c + GLA_IN
    lo1 = 3 * W + RWKV_DECAY_LORA
    lo2 = lo1 + RWKV_AAA_LORA
    w_a = jnp.concatenate([_pad_cols(w_in[:, :lo1], 3 * W + LORA_PAD), _pad_cols(w_in[:, lo1:lo2], LORA_PAD),
                           w_in[:, lo2:o_b]], axis=1).astype(BF16)
    mu = p['rwkv_mu'][l]
    mu_a = jnp.concatenate([jnp.pad(mu[:lo1], (0, LORA_PAD - RWKV_DECAY_LORA)),
                            jnp.pad(mu[lo1:lo2], (0, LORA_PAD - RWKV_AAA_LORA)), mu[lo2:]])[None, :]
    qscale = DIFF_HEAD ** -0.5 * math.log2(math.e)
    w_qk = jnp.concatenate([w_in[:, o_b:o_b + DIFF_QK] * qscale, w_in[:, o_b + DIFF_QK:o_b + 2 * DIFF_QK]],
                           axis=1).astype(BF16)
    w_vt = w_in[:, o_b + 2 * DIFF_QK:o_c].T.astype(BF16)
    c0 = o_c
    w_c = jnp.concatenate([w_in[:, c0:c0 + KW] * (GLA_KEY ** -0.5), w_in[:, c0 + KW:c0 + 2 * KW + VW],
                           w_in[:, c0 + 2 * KW + VW + GLA_GATE_LORA:o_g],
                           _pad_cols(w_in[:, c0 + 2 * KW + VW:c0 + 2 * KW + VW + GLA_GATE_LORA], LANE)],
                          axis=1).astype(BF16)
    w_g = w_in[:, o_g:].astype(BF16)

    proj_a = _matmul(xb, w_a, F32, 1024, 1024)
    qk = _matmul_rope(xb, w_qk, rope_tabs, 1024, 1024)
    tk = min(512, T)
    vt = _matmul_vt(w_vt, xb, tk)
    proj_c = _matmul(xb, w_c, F32, 512, GLA_IN_PAD)
    gates = _matmul(xb, w_g, BF16, 1024, 1024, gate=True)

    padr = lambda w: jnp.pad(w, ((0, LORA_PAD - w.shape[0]), (0, 0)))
    r2 = lambda v: v.reshape(1, -1)
    o_a = _rwkv(proj_a, mu_a, padr(p['rwkv_w2'][l]).astype(BF16), r2(p['rwkv_w0'][l]),
                padr(p['rwkv_a2'][l]).astype(BF16), r2(p['rwkv_a0'][l]), p['rwkv_g2'][l].astype(BF16),
                r2(p['rwkv_kk_scale'][l]), r2(p['rwkv_ka_scale'][l]), r2(p['rwkv_rk'][l]),
                r2(p['rwkv_gn_g'][l]), r2(p['rwkv_gn_b'][l]), 256)
    o_bb = _diff_attention(qk, vt, r2(p['diff_lq1'][l]), r2(p['diff_lk1'][l]), r2(p['diff_lq2'][l]),
                           r2(p['diff_lk2'][l]), p['diff_subln_g'][l].reshape(-1, 1), lambda_init, min(2 * tk, T))
    g2p = jnp.pad(p['gla_g2'][l], ((0, LANE - GLA_GATE_LORA), (0, 0)))
    o_c = _gla(proj_c, g2p, r2(p['gla_gb'][l]), r2(p['gla_gn_g'][l]), 256)

    merged = _merge(o_a, o_bb, o_c, gates, p['proj_a'][l].astype(BF16), p['proj_b'][l].astype(BF16),
                    p['proj_c'][l].astype(BF16), 256)
    x1, x1b = _mm_ln(merged, p['w_o'][l].astype(BF16), x, r2(p['ln1_g'][l]), r2(p['ln1_b'][l]), 256)

    wr = _pad_cols(jnp.concatenate([p['router_group_w'][l], p['router_expert_w'][l]], axis=1), LANE)
    br = _pad_cols(jnp.concatenate([p['router_group_b'][l], p['router_expert_b'][l]])[None, :], LANE)
    comb = _router(x1, wr, br, 512)
    F = MOE_HIDDEN
    wg = p['moe_w_gate'][l].reshape(N_EXPERTS, D_MODEL, F)
    wu = p['moe_w_up'][l].reshape(N_EXPERTS, D_MODEL, F)
    wd = p['moe_w_down'][l].reshape(N_EXPERTS, F, D_MODEL)
    y = _moe(x1, comb, wg, wu, wd, MOE_TILE)
    return _res_ln(x1, y, r2(p['ln2_g'][l]), r2(p['ln2_b'][l]), 512)


def kernel(x, w_in, rwkv_mu, rwkv_w2, rwkv_w0, rwkv_a2, rwkv_a0, rwkv_g2, rwkv_kk_scale, rwkv_ka_scale, rwkv_rk, rwkv_gn_g, rwkv_gn_b, diff_lq1, diff_lk1, diff_lq2, diff_lk2, diff_subln_g, gla_g2, gla_gb, gla_gn_g, proj_a, proj_b, proj_c, w_o, ln1_g, ln1_b, router_group_w, router_group_b, router_expert_w, router_expert_b, moe_w_gate, moe_w_up, moe_w_down, ln2_g, ln2_b):
    p = dict(w_in=w_in, rwkv_mu=rwkv_mu, rwkv_w2=rwkv_w2, rwkv_w0=rwkv_w0, rwkv_a2=rwkv_a2, rwkv_a0=rwkv_a0,
             rwkv_g2=rwkv_g2, rwkv_kk_scale=rwkv_kk_scale, rwkv_ka_scale=rwkv_ka_scale, rwkv_rk=rwkv_rk,
             rwkv_gn_g=rwkv_gn_g, rwkv_gn_b=rwkv_gn_b, diff_lq1=diff_lq1, diff_lk1=diff_lk1, diff_lq2=diff_lq2,
             diff_lk2=diff_lk2, diff_subln_g=diff_subln_g, gla_g2=gla_g2, gla_gb=gla_gb, gla_gn_g=gla_gn_g,
             proj_a=proj_a, proj_b=proj_b, proj_c=proj_c, w_o=w_o, ln1_g=ln1_g, ln1_b=ln1_b,
             router_group_w=router_group_w, router_group_b=router_group_b, router_expert_w=router_expert_w,
             router_expert_b=router_expert_b, moe_w_gate=moe_w_gate, moe_w_up=moe_w_up, moe_w_down=moe_w_down,
             ln2_g=ln2_g, ln2_b=ln2_b)
    B, T, D = x.shape
    assert B == 1 and D == D_MODEL
    xf = x.reshape(T, D)
    xb = xf.astype(BF16)
    tabs = _rope_tables(T)
    for l in range(DEPTH):
        xf, xb = _layer(xf, xb, l, p, tabs)
    return xf.reshape(B, T, D)
```

```python
import functools
import math

import numpy as np
import jax
import jax.numpy as jnp
from jax import lax
from jax.experimental import pallas as pl
from jax.experimental.pallas import tpu as pltpu

F32 = jnp.float32
BF16 = jnp.bfloat16
HIGHEST = lax.Precision.HIGHEST

D_MODEL = 2048
DEPTH = 2
DEEPNORM_ALPHA = (2 * DEPTH) ** 0.25
LN_EPS = 1e-5

RWKV_HEADS = 8
RWKV_HEAD = 64
RWKV_WIDTH = RWKV_HEADS * RWKV_HEAD
RWKV_DECAY_LORA = 96
RWKV_AAA_LORA = 96
RWKV_GATE_LORA = 256
RWKV_GN_EPS = 64e-5
RWKV_IN = 3 * RWKV_WIDTH + RWKV_DECAY_LORA + RWKV_AAA_LORA + RWKV_GATE_LORA
RWKV_CHUNK = 64
LORA_PAD = 128
RWKV_IN_PAD = 3 * RWKV_WIDTH + 2 * LORA_PAD + RWKV_GATE_LORA

DIFF_HEADS = 8
DIFF_HEAD = 64
DIFF_QK = DIFF_HEADS * 2 * DIFF_HEAD
DIFF_V = DIFF_HEADS * 2 * DIFF_HEAD
DIFF_IN = 2 * DIFF_QK + DIFF_V
ROPE_THETA = 500000.0
ROPE_DIM = DIFF_HEAD // 4
ROPE_HALF = ROPE_DIM // 2

GLA_HEADS = 4
GLA_KEY = 64
GLA_VALUE = 128
GLA_K_WIDTH = GLA_HEADS * GLA_KEY
GLA_V_WIDTH = GLA_HEADS * GLA_VALUE
GLA_GATE_LORA = 16
GLA_TAU = 16.0
GLA_SUB = 16
GLA_IN = 2 * GLA_K_WIDTH + 2 * GLA_V_WIDTH + GLA_GATE_LORA
GLA_IN_PAD = 2 * GLA_K_WIDTH + 2 * GLA_V_WIDTH + 128

N_BRANCHES = 3
MOE_GROUPS = 4
MOE_EXPERTS = 8
MOE_HIDDEN = 256
N_EXPERTS = MOE_GROUPS * MOE_EXPERTS
MOE_TILE = 512
LANE = 128
V7X_VMEM_LIMIT = 56 * 1024 * 1024


def _cparams(sem):
    return pltpu.CompilerParams(dimension_semantics=sem, vmem_limit_bytes=V7X_VMEM_LIMIT)


def _nt(a, b, **kw):
    return lax.dot_general(a, b, (((1,), (1,)), ((), ())), preferred_element_type=F32, **kw)


def _tn(a, b, **kw):
    return lax.dot_general(a, b, (((0,), (0,)), ((), ())), preferred_element_type=F32, **kw)


def _dot(a, b, **kw):
    return jnp.dot(a, b, preferred_element_type=F32, **kw)


def _mm_kernel(a_ref, b_ref, o_ref, *, gate):
    acc = _dot(a_ref[...], b_ref[...])
    if gate:
        acc = jax.nn.sigmoid(acc)
    o_ref[...] = acc.astype(o_ref.dtype)


def _matmul(a, b, out_dtype, tm, tn, gate=False):
    M, K = a.shape
    N = b.shape[1]
    tm, tn = min(tm, M), min(tn, N)
    return pl.pallas_call(
        functools.partial(_mm_kernel, gate=gate),
        grid=(N // tn, M // tm),
        in_specs=[pl.BlockSpec((tm, K), lambda j, i: (i, 0)),
                  pl.BlockSpec((K, tn), lambda j, i: (0, j))],
        out_specs=pl.BlockSpec((tm, tn), lambda j, i: (i, j)),
        out_shape=jax.ShapeDtypeStruct((M, N), out_dtype),
        compiler_params=_cparams(("arbitrary", "arbitrary")),
        name="proj_mm",
    )(a, b)


def _mm_rope_kernel(a_ref, b_ref, c_ref, sa_ref, sb_ref, o_ref):
    acc = _dot(a_ref[...], b_ref[...])
    c, sa, sb = c_ref[...], sa_ref[...], sb_ref[...]
    for h in range(acc.shape[1] // LANE):
        blk = acc[:, h * LANE:(h + 1) * LANE]
        out = blk * c + pltpu.roll(blk, LANE - ROPE_HALF, 1) * sa + pltpu.roll(blk, ROPE_HALF, 1) * sb
        o_ref[:, h * LANE:(h + 1) * LANE] = out.astype(o_ref.dtype)


def _matmul_rope(a, b, tabs, tm, tn):
    M, K = a.shape
    N = b.shape[1]
    tm, tn = min(tm, M), min(tn, N)
    tab_spec = pl.BlockSpec((tm, LANE), lambda j, i: (i, 0))
    return pl.pallas_call(
        _mm_rope_kernel,
        grid=(N // tn, M // tm),
        in_specs=[pl.BlockSpec((tm, K), lambda j, i: (i, 0)),
                  pl.BlockSpec((K, tn), lambda j, i: (0, j)),
                  tab_spec, tab_spec, tab_spec],
        out_specs=pl.BlockSpec((tm, tn), lambda j, i: (i, j)),
        out_shape=jax.ShapeDtypeStruct((M, N), BF16),
        compiler_params=_cparams(("arbitrary", "arbitrary")),
        name="proj_qk_rope",
    )(a, b, *tabs)


def _mm_vt_kernel(wt_ref, x_ref, o_ref):
    o_ref[0] = _nt(wt_ref[...], x_ref[...]).astype(o_ref.dtype)


def _matmul_vt(wt, x, tk):
    N, K = wt.shape
    T = x.shape[0]
    return pl.pallas_call(
        _mm_vt_kernel,
        grid=(T // tk,),
        in_specs=[pl.BlockSpec((N, K), lambda i: (0, 0)),
                  pl.BlockSpec((tk, K), lambda i: (i, 0))],
        out_specs=pl.BlockSpec((1, N, tk), lambda i: (i, 0, 0)),
        out_shape=jax.ShapeDtypeStruct((T // tk, N, tk), BF16),
        compiler_params=_cparams(("arbitrary",)),
        name="proj_vt",
    )(wt, x)


def _rope_tables(T):
    pos = jnp.arange(T, dtype=F32)[:, None]
    inv_freq = ROPE_THETA ** (-jnp.arange(ROPE_HALF, dtype=F32) / ROPE_HALF)
    lane = np.arange(LANE) % DIFF_HEAD
    ang = pos * inv_freq[None, :]
    cos8, sin8 = jnp.cos(ang), jnp.sin(ang)
    idx = jnp.asarray(lane % ROPE_HALF)
    cos_l, sin_l = cos8[:, idx], sin8[:, idx]
    first = jnp.asarray(lane < ROPE_HALF)[None, :]
    second = jnp.asarray((lane >= ROPE_HALF) & (lane < ROPE_DIM))[None, :]
    c = jnp.where(first | second, cos_l, 1.0)
    sa = jnp.where(first, -sin_l, 0.0)
    sb = jnp.where(second, sin_l, 0.0)
    return c, sa, sb


def _rwkv_kernel(a_ref, mu_ref, w2_ref, w0_ref, a2_ref, a0_ref, g2_ref, kks_ref, kas_ref, rk_ref,
                 gng_ref, gnb_ref, seg_ref, tri_ref, o_ref,
                 prev_ref, h_ref, at_ref, rt_ref, bt_ref, kt_ref, bp_ref, kp_ref, v_ref, gc_ref, y_ref):
    TB = a_ref.shape[0]
    C, N, W = RWKV_CHUNK, RWKV_HEAD, RWKV_WIDTH

    @pl.when(pl.program_id(0) == 0)
    def _():
        prev_ref[...] = jnp.zeros_like(prev_ref)
        h_ref[...] = jnp.zeros_like(h_ref)

    a = a_ref[...]
    row = lax.broadcasted_iota(jnp.int32, a.shape, 0)
    shifted = jnp.where(row == 0, prev_ref[...], pltpu.roll(a, 1, 0))
    prev_ref[...] = a[TB - 1:TB, :]
    xs = a + (shifted - a) * mu_ref[...]

    r = xs[:, 0:W]
    k = xs[:, W:2 * W]
    v = xs[:, 2 * W:3 * W]
    w_lo = xs[:, 3 * W:3 * W + LORA_PAD]
    a_lo = xs[:, 3 * W + LORA_PAD:3 * W + 2 * LORA_PAD]
    g_lo = xs[:, 3 * W + 2 * LORA_PAD:]

    z = w0_ref[...] + _dot(jnp.tanh(w_lo).astype(BF16), w2_ref[...])
    w = -jax.nn.softplus(-z) - 0.5
    lw = -jnp.exp(w)
    aa = jax.nn.sigmoid(a0_ref[...] + _dot(a_lo.astype(BF16), a2_ref[...]))
    g = _dot(jax.nn.sigmoid(g_lo).astype(BF16), g2_ref[...])

    seg = seg_ref[...]
    kk = k * kks_ref[...]
    ss = _dot((kk * kk).astype(BF16), seg)
    kk = kk * lax.rsqrt(jnp.maximum(ss, 1e-24))
    km = k * (1.0 + (aa - 1.0) * kas_ref[...])
    beta = kk * aa

    tri = tri_ref[...]
    cum = _dot(tri, lw, precision=HIGHEST)
    rowc = lax.broadcasted_iota(jnp.int32, (TB, TB), 0) // C
    colc = lax.broadcasted_iota(jnp.int32, (TB, TB), 1) // C
    allc = jnp.where(rowc == colc, 1.0, 0.0).astype(F32)
    cum_c = _dot(allc, lw, precision=HIGHEST)

    e_in = jnp.exp(cum)
    e_neg = jnp.exp(-cum)
    e_end = jnp.exp(cum_c - cum)
    at_ref[...] = (-kk * jnp.exp(cum - lw)).astype(BF16)
    rt_ref[...] = (r * e_in).astype(BF16)
    bt_ref[...] = (beta * e_neg).astype(BF16)
    kt_ref[...] = (km * e_neg).astype(BF16)
    bp_ref[...] = (beta * e_end).astype(BF16)
    kp_ref[...] = (km * e_end).astype(BF16)
    v_ref[...] = v.astype(BF16)
    gc_ref[...] = jnp.exp(cum_c)

    ri = lax.broadcasted_iota(jnp.int32, (C, C), 0)
    ci = lax.broadcasted_iota(jnp.int32, (C, C), 1)
    strict = ci < ri
    incl = ci <= ri
    eye = ci == ri

    def chunk(c, carry):
        NC = 2
        rows = [pl.ds(pl.multiple_of((c * NC + ci) * C, C), C) for ci in range(NC)]
        HS = range(NC * RWKV_HEADS)
        rw = [rows[i // RWKV_HEADS] for i in HS]
        sl = [slice((i % RWKV_HEADS) * N, (i % RWKV_HEADS + 1) * N) for i in HS]
        cat = lambda a, b: jnp.concatenate([a, b], axis=0)
        at = [at_ref[rw[i], sl[i]] for i in HS]
        rt = [rt_ref[rw[i], sl[i]] for i in HS]
        bt = [bt_ref[rw[i], sl[i]] for i in HS]
        kt = [kt_ref[rw[i], sl[i]] for i in HS]
        bp = [bp_ref[rw[i], sl[i]] for i in HS]
        kp = [kp_ref[rw[i], sl[i]] for i in HS]
        vh = [v_ref[rw[i], sl[i]] for i in HS]
        ar = [cat(at[h], rt[h]) for h in HS]
        xb = [_nt(ar[h], bt[h]) for h in HS]
        xk = [_nt(ar[h], kt[h]) for h in HS]
        a_ab = [jnp.where(strict, xb[h][:C], 0.0) for h in HS]
        a_rb = [jnp.where(incl, xb[h][C:], 0.0).astype(BF16) for h in HS]
        a_ak = [jnp.where(strict, xk[h][:C], 0.0).astype(BF16) for h in HS]
        a_rk = [jnp.where(incl, xk[h][C:], 0.0).astype(BF16) for h in HS]
        ident = jnp.where(eye, 1.0, 0.0)
        t_inv = [ident + a_ab[h] for h in HS]
        ap = a_ab
        for _ in range(5):
            apb = [ap[h].astype(BF16) for h in HS]
            ap = [_dot(apb[h], apb[h]) for h in HS]
            t_inv = [t_inv[h] + _dot(t_inv[h].astype(BF16), ap[h].astype(BF16)) for h in HS]
        tb = [t_inv[h].astype(BF16) for h in HS]
        wv = [_dot(a_ak[h], vh[h]).astype(BF16) for h in HS]
        pb = [_dot(tb[h], at[h]).astype(BF16) for h in HS]
        u0b = [_dot(tb[h], wv[h]).astype(BF16) for h in HS]
        qm = [rt[h].astype(F32) + _dot(a_rb[h], pb[h]) for h in HS]
        y0 = [_dot(a_rb[h], u0b[h]) + _dot(a_rk[h], vh[h]) for h in HS]
        gm = [jnp.where(eye, gc_ref[rw[h], sl[h]][0:1, :], 0.0) + _tn(bp[h], pb[h]) for h in HS]
        hadd = [_tn(cat(bp[h], kp[h]), cat(u0b[h], vh[h])) for h in HS]
        qg = [cat(qm[h], gm[h]).astype(BF16) for h in HS]
        for i in HS:
            hd = i % RWKV_HEADS
            z = _dot(qg[i], h_ref[hd].astype(BF16))
            y_ref[rw[i], sl[i]] = z[:C] + y0[i]
            h_ref[hd] = z[C:] + hadd[i]
        return carry

    lax.fori_loop(0, TB // (2 * C), chunk, 0)

    y = y_ref[...]
    segm = seg_ref[...]
    mu = _dot(y.astype(BF16), segm) * (1.0 / N)
    yc = y - mu
    var = _dot((yc * yc).astype(BF16), segm) * (1.0 / N)
    yn = yc * lax.rsqrt(var + RWKV_GN_EPS) * gng_ref[...] + gnb_ref[...]
    bonus = _dot((r * km * rk_ref[...]).astype(BF16), segm) * v
    o_ref[...] = ((yn + bonus) * g).astype(o_ref.dtype)


def _rwkv(a_proj, mu, w2, w0, a2, a0, g2, kks, kas, rk, gng, gnb, tb):
    T = a_proj.shape[0]
    tb = min(tb, T)
    W = RWKV_WIDTH
    hid = np.arange(W) // RWKV_HEAD
    seg = jnp.asarray(hid[:, None] == hid[None, :], dtype=BF16)
    t_idx = np.arange(tb)
    tri = jnp.asarray((t_idx[:, None] // RWKV_CHUNK == t_idx[None, :] // RWKV_CHUNK)
                      & (t_idx[None, :] <= t_idx[:, None]), dtype=F32)
    full = lambda shp: pl.BlockSpec(shp, lambda i: (0,) * len(shp))
    row = lambda n: full((1, n))
    bf = lambda: pltpu.VMEM((tb, W), BF16)
    return pl.pallas_call(
        _rwkv_kernel,
        grid=(T // tb,),
        in_specs=[pl.BlockSpec((tb, RWKV_IN_PAD), lambda i: (i, 0)),
                  row(RWKV_IN_PAD), full((LORA_PAD, W)), row(W), full((LORA_PAD, W)), row(W),
                  full((RWKV_GATE_LORA, W)), row(W), row(W), row(W), row(W), row(W),
                  full((W, W)), full((tb, tb))],
        out_specs=pl.BlockSpec((tb, W), lambda i: (i, 0)),
        out_shape=jax.ShapeDtypeStruct((T, W), BF16),
        scratch_shapes=[pltpu.VMEM((1, RWKV_IN_PAD), F32),
                        pltpu.VMEM((RWKV_HEADS, RWKV_HEAD, RWKV_HEAD), F32),
                        bf(), bf(), bf(), bf(), bf(), bf(), bf(),
                        pltpu.VMEM((tb, W), F32), pltpu.VMEM((tb, W), F32)],
        compiler_params=_cparams(("arbitrary",)),
        name="rwkv7",
    )(a_proj, mu, w2, w0, a2, a0, g2, kks, kas, rk, gng, gnb, seg, tri)


def _diff_kernel(q_ref, k_ref, vt_ref, lq1_ref, lk1_ref, lq2_ref, lk2_ref, sg_ref, o_ref,
                 m_ref, l_ref, acc_ref, s_ref, p_ref, a_ref, *, lambda_init):
    tq = q_ref.shape[0]
    nkb, _, tk = vt_ref.shape
    i = pl.program_id(1)
    q = q_ref[...]
    lane = lax.broadcasted_iota(jnp.int32, q.shape, 1)
    qm = (jnp.where(lane < DIFF_HEAD, q, jnp.zeros_like(q)),
          jnp.where(lane >= DIFF_HEAD, q, jnp.zeros_like(q)))
    m_ref[...] = jnp.full(m_ref.shape, -jnp.inf, F32)
    l_ref[...] = jnp.zeros_like(l_ref)
    acc_ref[...] = jnp.zeros_like(acc_ref)
    p_ref[1] = jnp.zeros(p_ref.shape[1:], BF16)
    a_ref[1] = jnp.ones(a_ref.shape[1:], F32)

    def scores(j, slot):
        jc = jnp.minimum(j, nkb - 1)
        kj = k_ref[pl.ds(pl.multiple_of(jc * tk, tk), tk), :]
        for mi in range(2):
            s_ref[slot, mi] = _nt(kj, qm[mi])

    def values(j, slot):
        vj = vt_ref[jnp.clip(j, 0, nkb - 1)]
        for mi in range(2):
            acc_ref[mi] = a_ref[slot, mi] * acc_ref[mi] + _dot(vj, p_ref[slot, mi])

    def softmax(j, slot, masked):
        for mi in range(2):
            s = s_ref[slot, mi]
            if masked:
                d = (lax.broadcasted_iota(jnp.int32, (tk, tq), 0)
                     - lax.broadcasted_iota(jnp.int32, (tk, tq), 1))
                s = jnp.where(d <= i * tq - j * tk, s, -jnp.inf)
            m_old = m_ref[mi]
            m_new = jnp.maximum(m_old, jnp.max(s, axis=0, keepdims=True))
            alpha = jnp.exp2(m_old - m_new)
            p = jnp.exp2(s - m_new)
            l_ref[mi] = alpha * l_ref[mi] + jnp.sum(p, axis=0, keepdims=True)
            m_ref[mi] = m_new
            a_ref[slot, mi] = alpha
            p_ref[slot, mi] = p.astype(BF16)

    def half(j, cur, masked):
        nxt = 1 - cur
        scores(j + 1, nxt)
        values(j - 1, nxt)
        softmax(j, cur, masked)

    def pair(jj, carry):
        half(2 * jj, 0, False)
        half(2 * jj + 1, 1, False)
        return carry

    scores(0, 0)
    n_pairs = ((i + 1) * (tq // tk) + 1) // 2
    lax.fori_loop(0, n_pairs - 1, pair, 0)
    j_last = 2 * (n_pairs - 1)
    half(j_last, 0, True)
    half(j_last + 1, 1, True)
    values(j_last + 1, 1)

    lam = (jnp.exp(jnp.sum(lq1_ref[...] * lk1_ref[...])) - jnp.exp(jnp.sum(lq2_ref[...] * lk2_ref[...]))
           + lambda_init)
    o = acc_ref[0] / l_ref[0] - lam * (acc_ref[1] / l_ref[1])
    o = o * lax.rsqrt(jnp.mean(o * o, axis=0, keepdims=True) + LN_EPS) * (1.0 - lambda_init)
    o = o * sg_ref[...]
    o_ref[...] = o.T.astype(o_ref.dtype)


def _diff_attention(qk, vt, lq1, lk1, lq2, lk2, subln_g, lambda_init, tq):
    T = qk.shape[0]
    tk = vt.shape[2]
    assert tq % tk == 0 and T % tq == 0
    H, E = DIFF_HEADS, 2 * DIFF_HEAD
    vec = lambda: pl.BlockSpec((1, DIFF_HEAD), lambda h, i: (0, 0))
    return pl.pallas_call(
        functools.partial(_diff_kernel, lambda_init=lambda_init),
        grid=(H, T // tq),
        in_specs=[pl.BlockSpec((tq, E), lambda h, i: (i, h)),
                  pl.BlockSpec((T, E), lambda h, i: (0, H + h)),
                  pl.BlockSpec((T // tk, E, tk), lambda h, i: (0, h, 0)),
                  vec(), vec(), vec(), vec(),
                  pl.BlockSpec((E, 1), lambda h, i: (0, 0))],
        out_specs=pl.BlockSpec((tq, E), lambda h, i: (i, h)),
        out_shape=jax.ShapeDtypeStruct((T, DIFF_V), BF16),
        scratch_shapes=[pltpu.VMEM((2, 1, tq), F32), pltpu.VMEM((2, 1, tq), F32),
                        pltpu.VMEM((2, E, tq), F32), pltpu.VMEM((2, 2, tk, tq), F32),
                        pltpu.VMEM((2, 2, tk, tq), BF16), pltpu.VMEM((2, 2, 1, tq), F32)],
        compiler_params=_cparams(("arbitrary", "arbitrary")),
        name="diff_attn",
    )(qk, qk, vt, lq1, lk1, lq2, lk2, subln_g)


def _gla_kernel(c_ref, g2_ref, gb_ref, gng_ref, tri_ref, segv_ref, o_ref,
                st_ref, oacc_ref):
    TB = c_ref.shape[0]
    KW, VW, S = GLA_K_WIDTH, GLA_V_WIDTH, GLA_SUB
    DK, DV = GLA_KEY, GLA_VALUE

    @pl.when(pl.program_id(0) == 0)
    def _():
        st_ref[...] = jnp.zeros_like(st_ref)

    g_lo = c_ref[:, 2 * KW + 2 * VW:]
    gate = _dot(g_lo, g2_ref[...], precision=HIGHEST) + gb_ref[...]
    log_a = jax.nn.log_sigmoid(gate) * (1.0 / GLA_TAU)
    b = _dot(tri_ref[...], log_a, precision=HIGHEST)
    NS = TB // S
    b3 = b.reshape(NS, S, KW)
    q3 = c_ref[:, 0:KW].reshape(NS, S, KW)
    k3 = c_ref[:, KW:2 * KW].reshape(NS, S, KW)
    v = c_ref[:, 2 * KW:2 * KW + VW]
    v3 = v.reshape(NS, S, VW)
    bl3 = b3[:, S - 1:S, :]
    qb = (q3 * jnp.exp(b3)).astype(BF16).reshape(TB, KW)
    kb = (k3 * jnp.exp(bl3 - b3)).astype(BF16).reshape(TB, KW)
    dec = jnp.exp(bl3)
    vb = v.astype(BF16)

    ri = lax.broadcasted_iota(jnp.int32, (NS, S, 1), 1)
    segv = segv_ref[...]
    acc = jnp.zeros((NS, S, VW), F32)
    for j in range(S):
        e = jnp.exp(jnp.minimum(b3 - b3[:, j:j + 1, :], 0.0))
        t = (q3 * e * k3[:, j:j + 1, :]).astype(BF16).reshape(TB, KW)
        att = _dot(t, segv).reshape(NS, S, VW)
        acc = acc + jnp.where(ri >= j, att, 0.0) * v3[:, j:j + 1, :]
    oacc_ref[...] = acc.reshape(TB, VW)

    for h in range(GLA_HEADS):
        lk = slice(h * DK, (h + 1) * DK)
        lv = slice(h * DV, (h + 1) * DV)
        kv = [_tn(vb[s * S:(s + 1) * S, lv], kb[s * S:(s + 1) * S, lk]) for s in range(NS)]
        st = st_ref[h]
        for s in range(NS):
            rows = slice(s * S, (s + 1) * S)
            oacc_ref[rows, lv] += _nt(qb[rows, lk], st.astype(BF16))
            st = st * dec[s][:, lk] + kv[s]
        st_ref[h] = st

    o = oacc_ref[...]
    rgate = c_ref[:, 2 * KW + VW:2 * KW + 2 * VW]
    for h in range(GLA_HEADS):
        lv = slice(h * DV, (h + 1) * DV)
        oh = o[:, lv]
        oh = oh * lax.rsqrt(jnp.mean(oh * oh, axis=1, keepdims=True) + LN_EPS) * gng_ref[:, lv]
        o_ref[:, lv] = (oh * jax.nn.silu(rgate[:, lv])).astype(o_ref.dtype)


def _gla(c_proj, g2, gb, gng, tb):
    T = c_proj.shape[0]
    tb = min(tb, T)
    KW, VW = GLA_K_WIDTH, GLA_V_WIDTH
    t_idx = np.arange(tb)
    tri = jnp.asarray((t_idx[:, None] // GLA_SUB == t_idx[None, :] // GLA_SUB)
                      & (t_idx[None, :] <= t_idx[:, None]), dtype=F32)
    segv = jnp.asarray((np.arange(KW) // GLA_KEY)[:, None] == (np.arange(VW) // GLA_VALUE)[None, :], dtype=BF16)
    full = lambda shp: pl.BlockSpec(shp, lambda i: (0,) * len(shp))
    return pl.pallas_call(
        _gla_kernel,
        grid=(T // tb,),
        in_specs=[pl.BlockSpec((tb, GLA_IN_PAD), lambda i: (i, 0)),
                  full((LANE, KW)), full((1, KW)), full((1, VW)), full((tb, tb)), full((KW, VW))],
        out_specs=pl.BlockSpec((tb, VW), lambda i: (i, 0)),
        out_shape=jax.ShapeDtypeStruct((T, VW), BF16),
        scratch_shapes=[pltpu.VMEM((GLA_HEADS, GLA_VALUE, GLA_KEY), F32),
                        pltpu.VMEM((tb, VW), F32)],
        compiler_params=_cparams(("arbitrary",)),
        name="gla",
    )(c_proj, g2, gb, gng, tri, segv)


def _merge_kernel(oa_ref, ob_ref, oc_ref, g_ref, pa_ref, pb_ref, pc_ref, o_ref):
    D = D_MODEL
    m = g_ref[:, 0:D].astype(F32) * _dot(oa_ref[...], pa_ref[...])
    m = m + g_ref[:, D:2 * D].astype(F32) * _dot(ob_ref[...], pb_ref[...])
    m = m + g_ref[:, 2 * D:3 * D].astype(F32) * _dot(oc_ref[...], pc_ref[...])
    o_ref[...] = m.astype(o_ref.dtype)


def _merge(oa, ob, oc, gates, pa, pb, pc, tm):
    T = oa.shape[0]
    tm = min(tm, T)
    D = D_MODEL
    rowblk = lambda n: pl.BlockSpec((tm, n), lambda i: (i, 0))
    full = lambda shp: pl.BlockSpec(shp, lambda i: (0, 0))
    return pl.pallas_call(
        _merge_kernel,
        grid=(T // tm,),
        in_specs=[rowblk(RWKV_WIDTH), rowblk(DIFF_V), rowblk(GLA_V_WIDTH), rowblk(3 * D),
                  full((RWKV_WIDTH, D)), full((DIFF_V, D)), full((GLA_V_WIDTH, D))],
        out_specs=rowblk(D),
        out_shape=jax.ShapeDtypeStruct((T, D), BF16),
        compiler_params=_cparams(("arbitrary",)),
        name="merge",
    )(oa, ob, oc, gates, pa, pb, pc)


def _layer_norm(z, g, b):
    mu = jnp.mean(z, axis=1, keepdims=True)
    zc = z - mu
    var = jnp.mean(zc * zc, axis=1, keepdims=True)
    return zc * lax.rsqrt(var + LN_EPS) * g + b


def _mm_ln_kernel(a_ref, w_ref, res_ref, g_ref, b_ref, wr_ref, br_ref, o_ref, c_ref):
    z = DEEPNORM_ALPHA * res_ref[...] + _dot(a_ref[...], w_ref[...])
    out = _layer_norm(z, g_ref[...], b_ref[...])
    o_ref[...] = out
    c_ref[...] = _route(out, wr_ref[...], br_ref[...])


def _mm_ln(a, w, res, g, b, wr, br, tm):
    T, K = a.shape
    D = D_MODEL
    tm = min(tm, T)
    rowblk = lambda n: pl.BlockSpec((tm, n), lambda i: (i, 0))
    full = lambda shp: pl.BlockSpec(shp, lambda i: (0, 0))
    return pl.pallas_call(
        _mm_ln_kernel,
        grid=(T // tm,),
        in_specs=[rowblk(K), full((K, D)), rowblk(D), full((1, D)), full((1, D)), full((D, LANE)), full((1, LANE))],
        out_specs=[rowblk(D), rowblk(LANE)],
        out_shape=[jax.ShapeDtypeStruct((T, D), F32), jax.ShapeDtypeStruct((T, LANE), F32)],
        compiler_params=_cparams(("arbitrary",)),
        name="wo_ln_route",
    )(a, w, res, g, b, wr, br)


def _route(x, w, b):
    G, E = MOE_GROUPS, MOE_EXPERTS
    logits = _dot(x, w, precision=HIGHEST) + b
    lane = lax.broadcasted_iota(jnp.int32, logits.shape, 1)
    neg = -jnp.inf
    big = jnp.int32(1 << 20)
    is_g = lane < G
    lg = jnp.where(is_g, logits, neg)
    gmax = jnp.max(lg, axis=1, keepdims=True)
    gidx = jnp.min(jnp.where(is_g & (lg == gmax), lane, big), axis=1, keepdims=True)
    g_p = 1.0 / jnp.sum(jnp.exp(lg - gmax), axis=1, keepdims=True)
    lo = G + gidx * E
    in_grp = (lane >= lo) & (lane < lo + E)
    le = jnp.where(in_grp, logits, neg)
    m1 = jnp.max(le, axis=1, keepdims=True)
    i1 = jnp.min(jnp.where(in_grp & (le == m1), lane, big), axis=1, keepdims=True)
    le2 = jnp.where(lane == i1, neg, le)
    m2 = jnp.max(le2, axis=1, keepdims=True)
    i2 = jnp.min(jnp.where(in_grp & (le2 == m2), lane, big), axis=1, keepdims=True)
    e2 = jnp.exp(m2 - m1)
    p1 = 1.0 / (1.0 + e2)
    p2 = e2 / (1.0 + e2)
    comb = jnp.where(lane == i1, g_p * p1, 0.0) + jnp.where(lane == i2, g_p * p2, 0.0)
    return jnp.where(lane == 0, gidx.astype(F32), comb)


def _moe_tables(comb, tm):
    T = comb.shape[0]
    G = MOE_GROUPS
    nt = T // tm + G
    gid = comb[:, 0].astype(jnp.int32)
    order = jnp.argsort(gid).astype(jnp.int32)
    counts = jnp.sum((gid[:, None] == jnp.arange(G)[None, :]).astype(jnp.int32), axis=0)
    starts = jnp.cumsum(counts) - counts
    ntile = (counts + tm - 1) // tm
    tstart = jnp.cumsum(ntile) - ntile
    n = jnp.arange(nt, dtype=jnp.int32)
    tg = jnp.sum((n[:, None] >= tstart[None, 1:]).astype(jnp.int32), axis=1)
    k = n - tstart[tg]
    row0 = starts[tg] + k * tm
    nvalid = jnp.clip(counts[tg] - k * tm, 0, tm).astype(jnp.int32)
    rows = jnp.clip(row0[:, None] + jnp.arange(tm, dtype=jnp.int32)[None, :], 0, T - 1)
    idx = jnp.take(order, rows).reshape(nt, 1, tm)
    return tg.astype(jnp.int32), nvalid, idx


def _moe_kernel(tg_ref, nv_ref, idx_ref, idxn_ref, x_hbm, comb_hbm, wg_ref, wu_ref, wd_ref, y_hbm,
                xs_ref, cs_ref, acc_ref, pend_ref, gx_sem, gc_sem, sc_sem):
    n = pl.program_id(0)
    e = pl.program_id(1)
    nt = pl.num_programs(0)
    nv = nv_ref[n]
    slot = n % 2
    other = 1 - slot
    tm = acc_ref.shape[1]
    part = tm // MOE_EXPERTS

    def start_row(idx, dst_slot, r):
        tok = idx[0, 0, r]
        pltpu.make_async_copy(x_hbm.at[pl.ds(tok, 1)], xs_ref.at[dst_slot, pl.ds(r, 1)],
                              gx_sem.at[dst_slot]).start()
        pltpu.make_async_copy(comb_hbm.at[pl.ds(tok, 1)], cs_ref.at[dst_slot, pl.ds(r, 1)],
                              gc_sem.at[dst_slot]).start()

    def wait_rows(s):
        pltpu.make_async_copy(x_hbm.at[pl.ds(0, tm)], xs_ref.at[s], gx_sem.at[s]).wait()
        pltpu.make_async_copy(comb_hbm.at[pl.ds(0, tm)], cs_ref.at[s], gc_sem.at[s]).wait()

    def drain(s):
        def body(r, c):
            pltpu.make_async_copy(acc_ref.at[s, pl.ds(0, 1)], y_hbm.at[pl.ds(0, 1)], sc_sem.at[s]).wait()
            return c
        lax.fori_loop(0, pend_ref[s], body, 0)
        pend_ref[s] = 0

    @pl.when((n == 0) & (e == 0))
    def _():
        pend_ref[0] = 0
        pend_ref[1] = 0

        def body(r, c):
            start_row(idx_ref, 0, r)
            return c
        lax.fori_loop(0, tm, body, 0)

    @pl.when(e == 0)
    def _():
        wait_rows(slot)

    for r in range(part):
        start_row(idxn_ref, other, e * part + r)

    xb = xs_ref[slot].astype(BF16)
    hg = _dot(xb, wg_ref[0].astype(BF16))
    hu = _dot(xb, wu_ref[0].astype(BF16))
    comb = cs_ref[slot]
    lane = lax.broadcasted_iota(jnp.int32, comb.shape, 1)
    col = MOE_GROUPS + tg_ref[n] * MOE_EXPERTS + e
    cw = jnp.sum(jnp.where(lane == col, comb, 0.0), axis=1, keepdims=True)
    h = (jax.nn.silu(hg) * hu * cw).astype(BF16)
    y = _dot(h, wd_ref[0].astype(BF16))
    acc_ref[slot] = y + jnp.where(e > 0, acc_ref[slot], 0.0)

    @pl.when(e == MOE_EXPERTS - 1)
    def _():
        drain(other)

        def body(r, c):
            tok = idx_ref[0, 0, r]
            pltpu.make_async_copy(acc_ref.at[slot, pl.ds(r, 1)], y_hbm.at[pl.ds(tok, 1)], sc_sem.at[slot]).start()
            return c
        lax.fori_loop(0, nv, body, 0)
        pend_ref[slot] = nv

        @pl.when(n == nt - 1)
        def _():
            drain(slot)
            wait_rows(other)


def _moe(x, comb, wg, wu, wd, tm):
    T, D = x.shape
    F = MOE_HIDDEN
    tm = min(tm, T)
    tg, nvalid, idx = _moe_tables(comb, tm)
    nt = tg.shape[0]
    wspec = lambda shp: pl.BlockSpec(shp, lambda n, e, tg_r, nv_r: (tg_r[n] * MOE_EXPERTS + e, 0, 0))
    ispec = lambda f: pl.BlockSpec((1, 1, tm), f, memory_space=pltpu.SMEM)
    grid_spec = pltpu.PrefetchScalarGridSpec(
        num_scalar_prefetch=2,
        grid=(nt, MOE_EXPERTS),
        in_specs=[ispec(lambda n, e, tg_r, nv_r: (n, 0, 0)),
                  ispec(lambda n, e, tg_r, nv_r: (jnp.minimum(n + 1, nt - 1), 0, 0)),
                  pl.BlockSpec(memory_space=pl.ANY), pl.BlockSpec(memory_space=pl.ANY),
                  wspec((1, D, F)), wspec((1, D, F)), wspec((1, F, D))],
        out_specs=pl.BlockSpec(memory_space=pl.ANY),
        scratch_shapes=[pltpu.VMEM((2, tm, D), F32), pltpu.VMEM((2, tm, LANE), F32), pltpu.VMEM((2, tm, D), F32),
                        pltpu.SMEM((2,), jnp.int32),
                        pltpu.SemaphoreType.DMA((2,)), pltpu.SemaphoreType.DMA((2,)),
                        pltpu.SemaphoreType.DMA((2,))],
    )
    return pl.pallas_call(
        _moe_kernel,
        grid_spec=grid_spec,
        out_shape=jax.ShapeDtypeStruct((T, D), F32),
        compiler_params=_cparams(("arbitrary", "arbitrary")),
        name="moe_experts",
    )(tg, nvalid, idx, idx, x, comb, wg, wu, wd)


def _res_ln_kernel(x_ref, y_ref, g_ref, b_ref, o_ref, ob_ref):
    out = _layer_norm(DEEPNORM_ALPHA * x_ref[...] + y_ref[...], g_ref[...], b_ref[...])
    o_ref[...] = out
    ob_ref[...] = out.astype(BF16)


def _res_ln(x, y, g, b, tm):
    T, D = x.shape
    tm = min(tm, T)
    rowblk = pl.BlockSpec((tm, D), lambda i: (i, 0))
    vec = pl.BlockSpec((1, D), lambda i: (0, 0))
    return pl.pallas_call(
        _res_ln_kernel,
        grid=(T // tm,),
        in_specs=[rowblk, rowblk, vec, vec],
        out_specs=[rowblk, rowblk],
        out_shape=[jax.ShapeDtypeStruct((T, D), F32), jax.ShapeDtypeStruct((T, D), BF16)],
        compiler_params=_cparams(("arbitrary",)),
        name="moe_ln",
    )(x, y, g, b)


def _pad_cols(w, n):
    return jnp.pad(w, ((0, 0), (0, n - w.shape[1])))


def _layer(x, xb, l, p, rope_tabs):
    T = x.shape[0]
    W = RWKV_WIDTH
    KW, VW = GLA_K_WIDTH, GLA_V_WIDTH
    lambda_init = 0.8 - 0.6 * math.exp(-0.3 * l)

    w_in = p['w_in'][l]
    o_b = RWKV_IN
    o_c = o_b + DIFF_IN
    o_g = o_c + GLA_IN
    lo1 = 3 * W + RWKV_DECAY_LORA
    lo2 = lo1 + RWKV_AAA_LORA
    w_a = jnp.concatenate([_pad_cols(w_in[:, :lo1], 3 * W + LORA_PAD), _pad_cols(w_in[:, lo1:lo2], LORA_PAD),
                           w_in[:, lo2:o_b]], axis=1).astype(BF16)
    mu = p['rwkv_mu'][l]
    mu_a = jnp.concatenate([jnp.pad(mu[:lo1], (0, LORA_PAD - RWKV_DECAY_LORA)),
                            jnp.pad(mu[lo1:lo2], (0, LORA_PAD - RWKV_AAA_LORA)), mu[lo2:]])[None, :]
    qscale = DIFF_HEAD ** -0.5 * math.log2(math.e)
    w_qk = jnp.concatenate([w_in[:, o_b:o_b + DIFF_QK] * qscale, w_in[:, o_b + DIFF_QK:o_b + 2 * DIFF_QK]],
                           axis=1).astype(BF16)
    w_vt = w_in[:, o_b + 2 * DIFF_QK:o_c].T.astype(BF16)
    c0 = o_c
    w_c = jnp.concatenate([w_in[:, c0:c0 + KW] * (GLA_KEY ** -0.5), w_in[:, c0 + KW:c0 + 2 * KW + VW],
                           w_in[:, c0 + 2 * KW + VW + GLA_GATE_LORA:o_g],
                           _pad_cols(w_in[:, c0 + 2 * KW + VW:c0 + 2 * KW + VW + GLA_GATE_LORA], LANE)],
                          axis=1).astype(BF16)
    w_g = w_in[:, o_g:].astype(BF16)

    proj_a = _matmul(xb, w_a, F32, 1024, 1024)
    qk = _matmul_rope(xb, w_qk, rope_tabs, 1024, 1024)
    tk = min(512, T)
    vt = _matmul_vt(w_vt, xb, tk)
    proj_c = _matmul(xb, w_c, F32, 512, GLA_IN_PAD)
    gates = _matmul(xb, w_g, BF16, 1024, 1024, gate=True)

    padr = lambda w: jnp.pad(w, ((0, LORA_PAD - w.shape[0]), (0, 0)))
    r2 = lambda v: v.reshape(1, -1)
    o_a = _rwkv(proj_a, mu_a, padr(p['rwkv_w2'][l]).astype(BF16), r2(p['rwkv_w0'][l]),
                padr(p['rwkv_a2'][l]).astype(BF16), r2(p['rwkv_a0'][l]), p['rwkv_g2'][l].astype(BF16),
                r2(p['rwkv_kk_scale'][l]), r2(p['rwkv_ka_scale'][l]), r2(p['rwkv_rk'][l]),
                r2(p['rwkv_gn_g'][l]), r2(p['rwkv_gn_b'][l]), 256)
    o_bb = _diff_attention(qk, vt, r2(p['diff_lq1'][l]), r2(p['diff_lk1'][l]), r2(p['diff_lq2'][l]),
                           r2(p['diff_lk2'][l]), p['diff_subln_g'][l].reshape(-1, 1), lambda_init, min(2 * tk, T))
    g2p = jnp.pad(p['gla_g2'][l], ((0, LANE - GLA_GATE_LORA), (0, 0)))
    o_c = _gla(proj_c, g2p, r2(p['gla_gb'][l]), r2(p['gla_gn_g'][l]), 256)

    merged = _merge(o_a, o_bb, o_c, gates, p['proj_a'][l].astype(BF16), p['proj_b'][l].astype(BF16),
                    p['proj_c'][l].astype(BF16), 256)
    wr = _pad_cols(jnp.concatenate([p['router_group_w'][l], p['router_expert_w'][l]], axis=1), LANE)
    br = _pad_cols(jnp.concatenate([p['router_group_b'][l], p['router_expert_b'][l]])[None, :], LANE)
    x1, comb = _mm_ln(merged, p['w_o'][l].astype(BF16), x, r2(p['ln1_g'][l]), r2(p['ln1_b'][l]), wr, br, 256)
    F = MOE_HIDDEN
    wg = p['moe_w_gate'][l].reshape(N_EXPERTS, D_MODEL, F)
    wu = p['moe_w_up'][l].reshape(N_EXPERTS, D_MODEL, F)
    wd = p['moe_w_down'][l].reshape(N_EXPERTS, F, D_MODEL)
    y = _moe(x1, comb, wg, wu, wd, MOE_TILE)
    return _res_ln(x1, y, r2(p['ln2_g'][l]), r2(p['ln2_b'][l]), 512)


def kernel(x, w_in, rwkv_mu, rwkv_w2, rwkv_w0, rwkv_a2, rwkv_a0, rwkv_g2, rwkv_kk_scale, rwkv_ka_scale, rwkv_rk, rwkv_gn_g, rwkv_gn_b, diff_lq1, diff_lk1, diff_lq2, diff_lk2, diff_subln_g, gla_g2, gla_gb, gla_gn_g, proj_a, proj_b, proj_c, w_o, ln1_g, ln1_b, router_group_w, router_group_b, router_expert_w, router_expert_b, moe_w_gate, moe_w_up, moe_w_down, ln2_g, ln2_b):
    p = dict(w_in=w_in, rwkv_mu=rwkv_mu, rwkv_w2=rwkv_w2, rwkv_w0=rwkv_w0, rwkv_a2=rwkv_a2, rwkv_a0=rwkv_a0,
             rwkv_g2=rwkv_g2, rwkv_kk_scale=rwkv_kk_scale, rwkv_ka_scale=rwkv_ka_scale, rwkv_rk=rwkv_rk,
             rwkv_gn_g=rwkv_gn_g, rwkv_gn_b=rwkv_gn_b, diff_lq1=diff_lq1, diff_lk1=diff_lk1, diff_lq2=diff_lq2,
             diff_lk2=diff_lk2, diff_subln_g=diff_subln_g, gla_g2=gla_g2, gla_gb=gla_gb, gla_gn_g=gla_gn_g,
             proj_a=proj_a, proj_b=proj_b, proj_c=proj_c, w_o=w_o, ln1_g=ln1_g, ln1_b=ln1_b,
             router_group_w=router_group_w, router_group_b=router_group_b, router_expert_w=router_expert_w,
             router_expert_b=router_expert_b, moe_w_gate=moe_w_gate, moe_w_up=moe_w_up, moe_w_down=moe_w_down,
             ln2_g=ln2_g, ln2_b=ln2_b)
    B, T, D = x.shape
    assert B == 1 and D == D_MODEL
    xf = x.reshape(T, D)
    xb = xf.astype(BF16)
    tabs = _rope_tables(T)
    for l in range(DEPTH):
        xf, xb = _layer(xf, xb, l, p, tabs)
    return xf.reshape(B, T, D)
```

```python
import functools
import math

import numpy as np
import jax
import jax.numpy as jnp
from jax import lax
from jax.experimental import pallas as pl
from jax.experimental.pallas import tpu as pltpu

F32 = jnp.float32
BF16 = jnp.bfloat16
HIGHEST = lax.Precision.HIGHEST

D_MODEL = 2048
DEPTH = 2
DEEPNORM_ALPHA = (2 * DEPTH) ** 0.25
LN_EPS = 1e-5

RWKV_HEADS = 8
RWKV_HEAD = 64
RWKV_WIDTH = RWKV_HEADS * RWKV_HEAD
RWKV_DECAY_LORA = 96
RWKV_AAA_LORA = 96
RWKV_GATE_LORA = 256
RWKV_GN_EPS = 64e-5
RWKV_IN = 3 * RWKV_WIDTH + RWKV_DECAY_LORA + RWKV_AAA_LORA + RWKV_GATE_LORA
RWKV_CHUNK = 64
LORA_PAD = 128
RWKV_IN_PAD = 3 * RWKV_WIDTH + 2 * LORA_PAD + RWKV_GATE_LORA

DIFF_HEADS = 8
DIFF_HEAD = 64
DIFF_QK = DIFF_HEADS * 2 * DIFF_HEAD
DIFF_V = DIFF_HEADS * 2 * DIFF_HEAD
DIFF_IN = 2 * DIFF_QK + DIFF_V
ROPE_THETA = 500000.0
ROPE_DIM = DIFF_HEAD // 4
ROPE_HALF = ROPE_DIM // 2

GLA_HEADS = 4
GLA_KEY = 64
GLA_VALUE = 128
GLA_K_WIDTH = GLA_HEADS * GLA_KEY
GLA_V_WIDTH = GLA_HEADS * GLA_VALUE
GLA_GATE_LORA = 16
GLA_TAU = 16.0
GLA_SUB = 16
GLA_IN = 2 * GLA_K_WIDTH + 2 * GLA_V_WIDTH + GLA_GATE_LORA
GLA_IN_PAD = 2 * GLA_K_WIDTH + 2 * GLA_V_WIDTH + 128

N_BRANCHES = 3
MOE_GROUPS = 4
MOE_EXPERTS = 8
MOE_HIDDEN = 256
N_EXPERTS = MOE_GROUPS * MOE_EXPERTS
MOE_TILE = 1024
LANE = 128
V7X_VMEM_LIMIT = 56 * 1024 * 1024


def _cparams(sem):
    return pltpu.CompilerParams(dimension_semantics=sem, vmem_limit_bytes=V7X_VMEM_LIMIT)


def _nt(a, b, **kw):
    return lax.dot_general(a, b, (((1,), (1,)), ((), ())), preferred_element_type=F32, **kw)


def _tn(a, b, **kw):
    return lax.dot_general(a, b, (((0,), (0,)), ((), ())), preferred_element_type=F32, **kw)


def _dot(a, b, **kw):
    return jnp.dot(a, b, preferred_element_type=F32, **kw)


def _mm_kernel(a_ref, b_ref, o_ref, *, gate):
    acc = _dot(a_ref[...], b_ref[...])
    if gate:
        acc = jax.nn.sigmoid(acc)
    o_ref[...] = acc.astype(o_ref.dtype)


def _matmul(a, b, out_dtype, tm, tn, gate=False):
    M, K = a.shape
    N = b.shape[1]
    tm, tn = min(tm, M), min(tn, N)
    return pl.pallas_call(
        functools.partial(_mm_kernel, gate=gate),
        grid=(N // tn, M // tm),
        in_specs=[pl.BlockSpec((tm, K), lambda j, i: (i, 0)),
                  pl.BlockSpec((K, tn), lambda j, i: (0, j))],
        out_specs=pl.BlockSpec((tm, tn), lambda j, i: (i, j)),
        out_shape=jax.ShapeDtypeStruct((M, N), out_dtype),
        compiler_params=_cparams(("arbitrary", "arbitrary")),
        name="proj_mm",
    )(a, b)


def _mm_rope_kernel(a_ref, b_ref, c_ref, sa_ref, sb_ref, o_ref):
    acc = _dot(a_ref[...], b_ref[...])
    c, sa, sb = c_ref[...], sa_ref[...], sb_ref[...]
    for h in range(acc.shape[1] // LANE):
        blk = acc[:, h * LANE:(h + 1) * LANE]
        out = blk * c + pltpu.roll(blk, LANE - ROPE_HALF, 1) * sa + pltpu.roll(blk, ROPE_HALF, 1) * sb
        o_ref[:, h * LANE:(h + 1) * LANE] = out.astype(o_ref.dtype)


def _matmul_rope(a, b, tabs, tm, tn):
    M, K = a.shape
    N = b.shape[1]
    tm, tn = min(tm, M), min(tn, N)
    tab_spec = pl.BlockSpec((tm, LANE), lambda j, i: (i, 0))
    return pl.pallas_call(
        _mm_rope_kernel,
        grid=(N // tn, M // tm),
        in_specs=[pl.BlockSpec((tm, K), lambda j, i: (i, 0)),
                  pl.BlockSpec((K, tn), lambda j, i: (0, j)),
                  tab_spec, tab_spec, tab_spec],
        out_specs=pl.BlockSpec((tm, tn), lambda j, i: (i, j)),
        out_shape=jax.ShapeDtypeStruct((M, N), BF16),
        compiler_params=_cparams(("arbitrary", "arbitrary")),
        name="proj_qk_rope",
    )(a, b, *tabs)


def _mm_vt_kernel(wt_ref, x_ref, o_ref):
    o_ref[0] = _nt(wt_ref[...], x_ref[...]).astype(o_ref.dtype)


def _matmul_vt(wt, x, tk):
    N, K = wt.shape
    T = x.shape[0]
    return pl.pallas_call(
        _mm_vt_kernel,
        grid=(T // tk,),
        in_specs=[pl.BlockSpec((N, K), lambda i: (0, 0)),
                  pl.BlockSpec((tk, K), lambda i: (i, 0))],
        out_specs=pl.BlockSpec((1, N, tk), lambda i: (i, 0, 0)),
        out_shape=jax.ShapeDtypeStruct((T // tk, N, tk), BF16),
        compiler_params=_cparams(("arbitrary",)),
        name="proj_vt",
    )(wt, x)


def _rope_tables(T):
    pos = jnp.arange(T, dtype=F32)[:, None]
    inv_freq = ROPE_THETA ** (-jnp.arange(ROPE_HALF, dtype=F32) / ROPE_HALF)
    lane = np.arange(LANE) % DIFF_HEAD
    ang = pos * inv_freq[None, :]
    cos8, sin8 = jnp.cos(ang), jnp.sin(ang)
    idx = jnp.asarray(lane % ROPE_HALF)
    cos_l, sin_l = cos8[:, idx], sin8[:, idx]
    first = jnp.asarray(lane < ROPE_HALF)[None, :]
    second = jnp.asarray((lane >= ROPE_HALF) & (lane < ROPE_DIM))[None, :]
    c = jnp.where(first | second, cos_l, 1.0)
    sa = jnp.where(first, -sin_l, 0.0)
    sb = jnp.where(second, sin_l, 0.0)
    return c, sa, sb


def _rwkv_kernel(a_ref, mu_ref, w2_ref, w0_ref, a2_ref, a0_ref, g2_ref, kks_ref, kas_ref, rk_ref,
                 gng_ref, gnb_ref, seg_ref, tri_ref, o_ref,
                 prev_ref, h_ref, at_ref, rt_ref, bt_ref, kt_ref, bp_ref, kp_ref, v_ref, gc_ref, y_ref):
    TB = a_ref.shape[0]
    C, N, W = RWKV_CHUNK, RWKV_HEAD, RWKV_WIDTH

    @pl.when(pl.program_id(0) == 0)
    def _():
        prev_ref[...] = jnp.zeros_like(prev_ref)
        h_ref[...] = jnp.zeros_like(h_ref)

    a = a_ref[...]
    row = lax.broadcasted_iota(jnp.int32, a.shape, 0)
    shifted = jnp.where(row == 0, prev_ref[...], pltpu.roll(a, 1, 0))
    prev_ref[...] = a[TB - 1:TB, :]
    xs = a + (shifted - a) * mu_ref[...]

    r = xs[:, 0:W]
    k = xs[:, W:2 * W]
    v = xs[:, 2 * W:3 * W]
    w_lo = xs[:, 3 * W:3 * W + LORA_PAD]
    a_lo = xs[:, 3 * W + LORA_PAD:3 * W + 2 * LORA_PAD]
    g_lo = xs[:, 3 * W + 2 * LORA_PAD:]

    z = w0_ref[...] + _dot(jnp.tanh(w_lo).astype(BF16), w2_ref[...])
    w = -jax.nn.softplus(-z) - 0.5
    lw = -jnp.exp(w)
    aa = jax.nn.sigmoid(a0_ref[...] + _dot(a_lo.astype(BF16), a2_ref[...]))
    g = _dot(jax.nn.sigmoid(g_lo).astype(BF16), g2_ref[...])

    seg = seg_ref[...]
    kk = k * kks_ref[...]
    ss = _dot((kk * kk).astype(BF16), seg)
    kk = kk * lax.rsqrt(jnp.maximum(ss, 1e-24))
    km = k * (1.0 + (aa - 1.0) * kas_ref[...])
    beta = kk * aa

    tri = tri_ref[...]
    cum = _dot(tri, lw, precision=HIGHEST)
    rowc = lax.broadcasted_iota(jnp.int32, (TB, TB), 0) // C
    colc = lax.broadcasted_iota(jnp.int32, (TB, TB), 1) // C
    allc = jnp.where(rowc == colc, 1.0, 0.0).astype(F32)
    cum_c = _dot(allc, lw, precision=HIGHEST)

    e_in = jnp.exp(cum)
    e_neg = jnp.exp(-cum)
    e_end = jnp.exp(cum_c - cum)
    at_ref[...] = (-kk * jnp.exp(cum - lw)).astype(BF16)
    rt_ref[...] = (r * e_in).astype(BF16)
    bt_ref[...] = (beta * e_neg).astype(BF16)
    kt_ref[...] = (km * e_neg).astype(BF16)
    bp_ref[...] = (beta * e_end).astype(BF16)
    kp_ref[...] = (km * e_end).astype(BF16)
    v_ref[...] = v.astype(BF16)
    gc_ref[...] = jnp.exp(cum_c)

    ri = lax.broadcasted_iota(jnp.int32, (C, C), 0)
    ci = lax.broadcasted_iota(jnp.int32, (C, C), 1)
    strict = ci < ri
    incl = ci <= ri
    eye = ci == ri

    def chunk(c, carry):
        NC = 2
        rows = [pl.ds(pl.multiple_of((c * NC + ci) * C, C), C) for ci in range(NC)]
        HS = range(NC * RWKV_HEADS)
        rw = [rows[i // RWKV_HEADS] for i in HS]
        sl = [slice((i % RWKV_HEADS) * N, (i % RWKV_HEADS + 1) * N) for i in HS]
        cat = lambda a, b: jnp.concatenate([a, b], axis=0)
        at = [at_ref[rw[i], sl[i]] for i in HS]
        rt = [rt_ref[rw[i], sl[i]] for i in HS]
        bt = [bt_ref[rw[i], sl[i]] for i in HS]
        kt = [kt_ref[rw[i], sl[i]] for i in HS]
        bp = [bp_ref[rw[i], sl[i]] for i in HS]
        kp = [kp_ref[rw[i], sl[i]] for i in HS]
        vh = [v_ref[rw[i], sl[i]] for i in HS]
        ar = [cat(at[h], rt[h]) for h in HS]
        xb = [_nt(ar[h], bt[h]) for h in HS]
        xk = [_nt(ar[h], kt[h]) for h in HS]
        a_ab = [jnp.where(strict, xb[h][:C], 0.0) for h in HS]
        a_rb = [jnp.where(incl, xb[h][C:], 0.0).astype(BF16) for h in HS]
        a_ak = [jnp.where(strict, xk[h][:C], 0.0).astype(BF16) for h in HS]
        a_rk = [jnp.where(incl, xk[h][C:], 0.0).astype(BF16) for h in HS]
        ident = jnp.where(eye, 1.0, 0.0)
        t_inv = [ident + a_ab[h] for h in HS]
        ap = a_ab
        for _ in range(5):
            apb = [ap[h].astype(BF16) for h in HS]
            ap = [_dot(apb[h], apb[h]) for h in HS]
            t_inv = [t_inv[h] + _dot(t_inv[h].astype(BF16), ap[h].astype(BF16)) for h in HS]
        tb = [t_inv[h].astype(BF16) for h in HS]
        wv = [_dot(a_ak[h], vh[h]).astype(BF16) for h in HS]
        pb = [_dot(tb[h], at[h]).astype(BF16) for h in HS]
        u0b = [_dot(tb[h], wv[h]).astype(BF16) for h in HS]
        qm = [rt[h].astype(F32) + _dot(a_rb[h], pb[h]) for h in HS]
        y0 = [_dot(a_rb[h], u0b[h]) + _dot(a_rk[h], vh[h]) for h in HS]
        gm = [jnp.where(eye, gc_ref[rw[h], sl[h]][0:1, :], 0.0) + _tn(bp[h], pb[h]) for h in HS]
        hadd = [_tn(cat(bp[h], kp[h]), cat(u0b[h], vh[h])) for h in HS]
        qg = [cat(qm[h], gm[h]).astype(BF16) for h in HS]
        for i in HS:
            hd = i % RWKV_HEADS
            z = _dot(qg[i], h_ref[hd].astype(BF16))
            y_ref[rw[i], sl[i]] = z[:C] + y0[i]
            h_ref[hd] = z[C:] + hadd[i]
        return carry

    lax.fori_loop(0, TB // (2 * C), chunk, 0)

    y = y_ref[...]
    segm = seg_ref[...]
    mu = _dot(y.astype(BF16), segm) * (1.0 / N)
    yc = y - mu
    var = _dot((yc * yc).astype(BF16), segm) * (1.0 / N)
    yn = yc * lax.rsqrt(var + RWKV_GN_EPS) * gng_ref[...] + gnb_ref[...]
    bonus = _dot((r * km * rk_ref[...]).astype(BF16), segm) * v
    o_ref[...] = ((yn + bonus) * g).astype(o_ref.dtype)


def _rwkv(a_proj, mu, w2, w0, a2, a0, g2, kks, kas, rk, gng, gnb, tb):
    T = a_proj.shape[0]
    tb = min(tb, T)
    W = RWKV_WIDTH
    hid = np.arange(W) // RWKV_HEAD
    seg = jnp.asarray(hid[:, None] == hid[None, :], dtype=BF16)
    t_idx = np.arange(tb)
    tri = jnp.asarray((t_idx[:, None] // RWKV_CHUNK == t_idx[None, :] // RWKV_CHUNK)
                      & (t_idx[None, :] <= t_idx[:, None]), dtype=F32)
    full = lambda shp: pl.BlockSpec(shp, lambda i: (0,) * len(shp))
    row = lambda n: full((1, n))
    bf = lambda: pltpu.VMEM((tb, W), BF16)
    return pl.pallas_call(
        _rwkv_kernel,
        grid=(T // tb,),
        in_specs=[pl.BlockSpec((tb, RWKV_IN_PAD), lambda i: (i, 0)),
                  row(RWKV_IN_PAD), full((LORA_PAD, W)), row(W), full((LORA_PAD, W)), row(W),
                  full((RWKV_GATE_LORA, W)), row(W), row(W), row(W), row(W), row(W),
                  full((W, W)), full((tb, tb))],
        out_specs=pl.BlockSpec((tb, W), lambda i: (i, 0)),
        out_shape=jax.ShapeDtypeStruct((T, W), BF16),
        scratch_shapes=[pltpu.VMEM((1, RWKV_IN_PAD), F32),
                        pltpu.VMEM((RWKV_HEADS, RWKV_HEAD, RWKV_HEAD), F32),
                        bf(), bf(), bf(), bf(), bf(), bf(), bf(),
                        pltpu.VMEM((tb, W), F32), pltpu.VMEM((tb, W), F32)],
        compiler_params=_cparams(("arbitrary",)),
        name="rwkv7",
    )(a_proj, mu, w2, w0, a2, a0, g2, kks, kas, rk, gng, gnb, seg, tri)


def _diff_kernel(q_ref, k_ref, vt_ref, lq1_ref, lk1_ref, lq2_ref, lk2_ref, sg_ref, o_ref,
                 m_ref, l_ref, acc_ref, s_ref, p_ref, a_ref, *, lambda_init):
    tq = q_ref.shape[0]
    nkb, _, tk = vt_ref.shape
    i = pl.program_id(1)
    q = q_ref[...]
    lane = lax.broadcasted_iota(jnp.int32, q.shape, 1)
    qm = (jnp.where(lane < DIFF_HEAD, q, jnp.zeros_like(q)),
          jnp.where(lane >= DIFF_HEAD, q, jnp.zeros_like(q)))
    m_ref[...] = jnp.full(m_ref.shape, -jnp.inf, F32)
    l_ref[...] = jnp.zeros_like(l_ref)
    acc_ref[...] = jnp.zeros_like(acc_ref)
    p_ref[1] = jnp.zeros(p_ref.shape[1:], BF16)
    a_ref[1] = jnp.ones(a_ref.shape[1:], F32)

    def scores(j, slot):
        jc = jnp.minimum(j, nkb - 1)
        kj = k_ref[pl.ds(pl.multiple_of(jc * tk, tk), tk), :]
        for mi in range(2):
            s_ref[slot, mi] = _nt(kj, qm[mi])

    def values(j, slot):
        vj = vt_ref[jnp.clip(j, 0, nkb - 1)]
        for mi in range(2):
            acc_ref[mi] = a_ref[slot, mi] * acc_ref[mi] + _dot(vj, p_ref[slot, mi])

    def softmax(j, slot, masked):
        for mi in range(2):
            s = s_ref[slot, mi]
            if masked:
                d = (lax.broadcasted_iota(jnp.int32, (tk, tq), 0)
                     - lax.broadcasted_iota(jnp.int32, (tk, tq), 1))
                s = jnp.where(d <= i * tq - j * tk, s, -jnp.inf)
            m_old = m_ref[mi]
            m_new = jnp.maximum(m_old, jnp.max(s, axis=0, keepdims=True))
            alpha = jnp.exp2(m_old - m_new)
            p = jnp.exp2(s - m_new)
            l_ref[mi] = alpha * l_ref[mi] + jnp.sum(p, axis=0, keepdims=True)
            m_ref[mi] = m_new
            a_ref[slot, mi] = alpha
            p_ref[slot, mi] = p.astype(BF16)

    def half(j, cur, masked):
        nxt = 1 - cur
        scores(j + 1, nxt)
        values(j - 1, nxt)
        softmax(j, cur, masked)

    def pair(jj, carry):
        half(2 * jj, 0, False)
        half(2 * jj + 1, 1, False)
        return carry

    scores(0, 0)
    n_pairs = ((i + 1) * (tq // tk) + 1) // 2
    lax.fori_loop(0, n_pairs - 1, pair, 0)
    j_last = 2 * (n_pairs - 1)
    half(j_last, 0, True)
    half(j_last + 1, 1, True)
    values(j_last + 1, 1)

    lam = (jnp.exp(jnp.sum(lq1_ref[...] * lk1_ref[...])) - jnp.exp(jnp.sum(lq2_ref[...] * lk2_ref[...]))
           + lambda_init)
    o = acc_ref[0] / l_ref[0] - lam * (acc_ref[1] / l_ref[1])
    o = o * lax.rsqrt(jnp.mean(o * o, axis=0, keepdims=True) + LN_EPS) * (1.0 - lambda_init)
    o = o * sg_ref[...]
    o_ref[...] = o.T.astype(o_ref.dtype)


def _diff_attention(qk, vt, lq1, lk1, lq2, lk2, subln_g, lambda_init, tq):
    T = qk.shape[0]
    tk = vt.shape[2]
    assert tq % tk == 0 and T % tq == 0
    H, E = DIFF_HEADS, 2 * DIFF_HEAD
    vec = lambda: pl.BlockSpec((1, DIFF_HEAD), lambda h, i: (0, 0))
    return pl.pallas_call(
        functools.partial(_diff_kernel, lambda_init=lambda_init),
        grid=(H, T // tq),
        in_specs=[pl.BlockSpec((tq, E), lambda h, i: (i, h)),
                  pl.BlockSpec((T, E), lambda h, i: (0, H + h)),
                  pl.BlockSpec((T // tk, E, tk), lambda h, i: (0, h, 0)),
                  vec(), vec(), vec(), vec(),
                  pl.BlockSpec((E, 1), lambda h, i: (0, 0))],
        out_specs=pl.BlockSpec((tq, E), lambda h, i: (i, h)),
        out_shape=jax.ShapeDtypeStruct((T, DIFF_V), BF16),
        scratch_shapes=[pltpu.VMEM((2, 1, tq), F32), pltpu.VMEM((2, 1, tq), F32),
                        pltpu.VMEM((2, E, tq), F32), pltpu.VMEM((2, 2, tk, tq), F32),
                        pltpu.VMEM((2, 2, tk, tq), BF16), pltpu.VMEM((2, 2, 1, tq), F32)],
        compiler_params=_cparams(("arbitrary", "arbitrary")),
        name="diff_attn",
    )(qk, qk, vt, lq1, lk1, lq2, lk2, subln_g)


def _gla_kernel(c_ref, g2_ref, gb_ref, gng_ref, tri_ref, segv_ref, o_ref,
                st_ref, oacc_ref):
    TB = c_ref.shape[0]
    KW, VW, S = GLA_K_WIDTH, GLA_V_WIDTH, GLA_SUB
    DK, DV = GLA_KEY, GLA_VALUE

    @pl.when(pl.program_id(0) == 0)
    def _():
        st_ref[...] = jnp.zeros_like(st_ref)

    g_lo = c_ref[:, 2 * KW + 2 * VW:]
    gate = _dot(g_lo, g2_ref[...], precision=HIGHEST) + gb_ref[...]
    log_a = jax.nn.log_sigmoid(gate) * (1.0 / GLA_TAU)
    b = _dot(tri_ref[...], log_a, precision=HIGHEST)
    NS = TB // S
    b3 = b.reshape(NS, S, KW)
    q3 = c_ref[:, 0:KW].reshape(NS, S, KW)
    k3 = c_ref[:, KW:2 * KW].reshape(NS, S, KW)
    v = c_ref[:, 2 * KW:2 * KW + VW]
    v3 = v.reshape(NS, S, VW)
    bl3 = b3[:, S - 1:S, :]
    qb = (q3 * jnp.exp(b3)).astype(BF16).reshape(TB, KW)
    kb = (k3 * jnp.exp(bl3 - b3)).astype(BF16).reshape(TB, KW)
    dec = jnp.exp(bl3)
    vb = v.astype(BF16)

    ri = lax.broadcasted_iota(jnp.int32, (NS, S, 1), 1)
    segv = segv_ref[...]
    acc = jnp.zeros((NS, S, VW), F32)
    for j in range(S):
        e = jnp.exp(jnp.minimum(b3 - b3[:, j:j + 1, :], 0.0))
        t = (q3 * e * k3[:, j:j + 1, :]).astype(BF16).reshape(TB, KW)
        att = _dot(t, segv).reshape(NS, S, VW)
        acc = acc + jnp.where(ri >= j, att, 0.0) * v3[:, j:j + 1, :]
    oacc_ref[...] = acc.reshape(TB, VW)

    for h in range(GLA_HEADS):
        lk = slice(h * DK, (h + 1) * DK)
        lv = slice(h * DV, (h + 1) * DV)
        kv = [_tn(vb[s * S:(s + 1) * S, lv], kb[s * S:(s + 1) * S, lk]) for s in range(NS)]
        st = st_ref[h]
        for s in range(NS):
            rows = slice(s * S, (s + 1) * S)
            oacc_ref[rows, lv] += _nt(qb[rows, lk], st.astype(BF16))
            st = st * dec[s][:, lk] + kv[s]
        st_ref[h] = st

    o = oacc_ref[...]
    rgate = c_ref[:, 2 * KW + VW:2 * KW + 2 * VW]
    for h in range(GLA_HEADS):
        lv = slice(h * DV, (h + 1) * DV)
        oh = o[:, lv]
        oh = oh * lax.rsqrt(jnp.mean(oh * oh, axis=1, keepdims=True) + LN_EPS) * gng_ref[:, lv]
        o_ref[:, lv] = (oh * jax.nn.silu(rgate[:, lv])).astype(o_ref.dtype)


def _gla(c_proj, g2, gb, gng, tb):
    T = c_proj.shape[0]
    tb = min(tb, T)
    KW, VW = GLA_K_WIDTH, GLA_V_WIDTH
    t_idx = np.arange(tb)
    tri = jnp.asarray((t_idx[:, None] // GLA_SUB == t_idx[None, :] // GLA_SUB)
                      & (t_idx[None, :] <= t_idx[:, None]), dtype=F32)
    segv = jnp.asarray((np.arange(KW) // GLA_KEY)[:, None] == (np.arange(VW) // GLA_VALUE)[None, :], dtype=BF16)
    full = lambda shp: pl.BlockSpec(shp, lambda i: (0,) * len(shp))
    return pl.pallas_call(
        _gla_kernel,
        grid=(T // tb,),
        in_specs=[pl.BlockSpec((tb, GLA_IN_PAD), lambda i: (i, 0)),
                  full((LANE, KW)), full((1, KW)), full((1, VW)), full((tb, tb)), full((KW, VW))],
        out_specs=pl.BlockSpec((tb, VW), lambda i: (i, 0)),
        out_shape=jax.ShapeDtypeStruct((T, VW), BF16),
        scratch_shapes=[pltpu.VMEM((GLA_HEADS, GLA_VALUE, GLA_KEY), F32),
                        pltpu.VMEM((tb, VW), F32)],
        compiler_params=_cparams(("arbitrary",)),
        name="gla",
    )(c_proj, g2, gb, gng, tri, segv)


def _merge_kernel(oa_ref, ob_ref, oc_ref, g_ref, pa_ref, pb_ref, pc_ref, o_ref):
    D = D_MODEL
    m = g_ref[:, 0:D].astype(F32) * _dot(oa_ref[...], pa_ref[...])
    m = m + g_ref[:, D:2 * D].astype(F32) * _dot(ob_ref[...], pb_ref[...])
    m = m + g_ref[:, 2 * D:3 * D].astype(F32) * _dot(oc_ref[...], pc_ref[...])
    o_ref[...] = m.astype(o_ref.dtype)


def _merge(oa, ob, oc, gates, pa, pb, pc, tm):
    T = oa.shape[0]
    tm = min(tm, T)
    D = D_MODEL
    rowblk = lambda n: pl.BlockSpec((tm, n), lambda i: (i, 0))
    full = lambda shp: pl.BlockSpec(shp, lambda i: (0, 0))
    return pl.pallas_call(
        _merge_kernel,
        grid=(T // tm,),
        in_specs=[rowblk(RWKV_WIDTH), rowblk(DIFF_V), rowblk(GLA_V_WIDTH), rowblk(3 * D),
                  full((RWKV_WIDTH, D)), full((DIFF_V, D)), full((GLA_V_WIDTH, D))],
        out_specs=rowblk(D),
        out_shape=jax.ShapeDtypeStruct((T, D), BF16),
        compiler_params=_cparams(("arbitrary",)),
        name="merge",
    )(oa, ob, oc, gates, pa, pb, pc)


def _layer_norm(z, g, b):
    mu = jnp.mean(z, axis=1, keepdims=True)
    zc = z - mu
    var = jnp.mean(zc * zc, axis=1, keepdims=True)
    return zc * lax.rsqrt(var + LN_EPS) * g + b


def _mm_ln_kernel(a_ref, w_ref, res_ref, g_ref, b_ref, wrh_ref, wrl_ref, br_ref, o_ref, c_ref):
    z = DEEPNORM_ALPHA * res_ref[...] + _dot(a_ref[...], w_ref[...])
    out = _layer_norm(z, g_ref[...], b_ref[...])
    o_ref[...] = out
    c_ref[...] = _route(out, wrh_ref[...], wrl_ref[...], br_ref[...])


def _mm_ln(a, w, res, g, b, wr, br, tm):
    wr_hi = wr.astype(BF16)
    wr_lo = (wr - wr_hi.astype(F32)).astype(BF16)
    T, K = a.shape
    D = D_MODEL
    tm = min(tm, T)
    rowblk = lambda n: pl.BlockSpec((tm, n), lambda i: (i, 0))
    full = lambda shp: pl.BlockSpec(shp, lambda i: (0, 0))
    return pl.pallas_call(
        _mm_ln_kernel,
        grid=(T // tm,),
        in_specs=[rowblk(K), full((K, D)), rowblk(D), full((1, D)), full((1, D)), full((D, LANE)), full((D, LANE)),
                  full((1, LANE))],
        out_specs=[rowblk(D), rowblk(LANE)],
        out_shape=[jax.ShapeDtypeStruct((T, D), F32), jax.ShapeDtypeStruct((T, LANE), F32)],
        compiler_params=_cparams(("arbitrary",)),
        name="wo_ln_route",
    )(a, w, res, g, b, wr_hi, wr_lo, br)


def _route(x, w_hi, w_lo, b):
    G, E = MOE_GROUPS, MOE_EXPERTS
    x_hi = x.astype(BF16)
    x_lo = (x - x_hi.astype(F32)).astype(BF16)
    logits = _dot(x_hi, w_hi) + (_dot(x_lo, w_hi) + _dot(x_hi, w_lo)) + b
    lane = lax.broadcasted_iota(jnp.int32, logits.shape, 1)
    neg = -jnp.inf
    big = jnp.int32(1 << 20)
    is_g = lane < G
    lg = jnp.where(is_g, logits, neg)
    gmax = jnp.max(lg, axis=1, keepdims=True)
    gidx = jnp.min(jnp.where(is_g & (lg == gmax), lane, big), axis=1, keepdims=True)
    g_p = 1.0 / jnp.sum(jnp.exp(lg - gmax), axis=1, keepdims=True)
    lo = G + gidx * E
    in_grp = (lane >= lo) & (lane < lo + E)
    le = jnp.where(in_grp, logits, neg)
    m1 = jnp.max(le, axis=1, keepdims=True)
    i1 = jnp.min(jnp.where(in_grp & (le == m1), lane, big), axis=1, keepdims=True)
    le2 = jnp.where(lane == i1, neg, le)
    m2 = jnp.max(le2, axis=1, keepdims=True)
    i2 = jnp.min(jnp.where(in_grp & (le2 == m2), lane, big), axis=1, keepdims=True)
    e2 = jnp.exp(m2 - m1)
    p1 = 1.0 / (1.0 + e2)
    p2 = e2 / (1.0 + e2)
    comb = jnp.where(lane == i1, g_p * p1, 0.0) + jnp.where(lane == i2, g_p * p2, 0.0)
    return jnp.where(lane == 0, gidx.astype(F32), comb)


def _moe_tables(comb, tm):
    T = comb.shape[0]
    G = MOE_GROUPS
    nt = T // tm + G
    gid = comb[:, 0].astype(jnp.int32)
    order = jnp.argsort(gid).astype(jnp.int32)
    counts = jnp.sum((gid[:, None] == jnp.arange(G)[None, :]).astype(jnp.int32), axis=0)
    starts = jnp.cumsum(counts) - counts
    ntile = (counts + tm - 1) // tm
    tstart = jnp.cumsum(ntile) - ntile
    n = jnp.arange(nt, dtype=jnp.int32)
    tg = jnp.sum((n[:, None] >= tstart[None, 1:]).astype(jnp.int32), axis=1)
    k = n - tstart[tg]
    row0 = starts[tg] + k * tm
    nvalid = jnp.clip(counts[tg] - k * tm, 0, tm).astype(jnp.int32)
    rows = jnp.clip(row0[:, None] + jnp.arange(tm, dtype=jnp.int32)[None, :], 0, T - 1)
    idx = jnp.take(order, rows).reshape(nt, 1, tm)
    return tg.astype(jnp.int32), nvalid, idx


def _moe_kernel(tg_ref, nv_ref, idx_ref, idxn_ref, x_hbm, comb_hbm, wg_ref, wu_ref, wd_ref, g_ref, b_ref, y_hbm,
                xs_ref, cs_ref, acc_ref, pend_ref, gx_sem, gc_sem, sc_sem):
    n = pl.program_id(0)
    e = pl.program_id(1)
    nt = pl.num_programs(0)
    nv = nv_ref[n]
    slot = n % 2
    other = 1 - slot
    tm = acc_ref.shape[1]
    part = tm // MOE_EXPERTS

    def start_row(idx, dst_slot, r):
        tok = idx[0, 0, r]
        pltpu.make_async_copy(x_hbm.at[pl.ds(tok, 1)], xs_ref.at[dst_slot, pl.ds(r, 1)],
                              gx_sem.at[dst_slot]).start()
        pltpu.make_async_copy(comb_hbm.at[pl.ds(tok, 1)], cs_ref.at[dst_slot, pl.ds(r, 1)],
                              gc_sem.at[dst_slot]).start()

    def start_next_part():
        for r in range(part):
            start_row(idxn_ref, other, e * part + r)

    def wait_rows(s):
        pltpu.make_async_copy(x_hbm.at[pl.ds(0, tm)], xs_ref.at[s], gx_sem.at[s]).wait()
        pltpu.make_async_copy(comb_hbm.at[pl.ds(0, tm)], cs_ref.at[s], gc_sem.at[s]).wait()

    def drain(s):
        def body(r, c):
            pltpu.make_async_copy(acc_ref.at[s, pl.ds(0, 1)], y_hbm.at[pl.ds(0, 1)], sc_sem.at[s]).wait()
            return c
        lax.fori_loop(0, pend_ref[s], body, 0)
        pend_ref[s] = 0

    @pl.when((n == 0) & (e == 0))
    def _():
        pend_ref[0] = 0
        pend_ref[1] = 0

        def body(r, c):
            start_row(idx_ref, 0, r)
            return c
        lax.fori_loop(0, tm, body, 0)

    @pl.when(e == 0)
    def _():
        wait_rows(slot)

    @pl.when(nv > 0)
    def _():
        start_next_part()
        xb = xs_ref[slot].astype(BF16)
        hg = _dot(xb, wg_ref[0])
        hu = _dot(xb, wu_ref[0])
        comb = cs_ref[slot]
        lane = lax.broadcasted_iota(jnp.int32, comb.shape, 1)
        col = MOE_GROUPS + tg_ref[n] * MOE_EXPERTS + e
        cw = jnp.sum(jnp.where(lane == col, comb, 0.0), axis=1, keepdims=True)
        h = (jax.nn.silu(hg) * hu * cw).astype(BF16)
        acc_ref[slot] = _dot(h, wd_ref[0]) + jnp.where(e > 0, acc_ref[slot], 0.0)

    @pl.when(nv == 0)
    def _():
        start_next_part()

    @pl.when(e == MOE_EXPERTS - 1)
    def _():
        @pl.when(nv > 0)
        def _():
            z = DEEPNORM_ALPHA * xs_ref[slot] + acc_ref[slot]
            acc_ref[slot] = _layer_norm(z, g_ref[...], b_ref[...])

        drain(other)

        def body(r, c):
            tok = idx_ref[0, 0, r]
            pltpu.make_async_copy(acc_ref.at[slot, pl.ds(r, 1)], y_hbm.at[pl.ds(tok, 1)], sc_sem.at[slot]).start()
            return c
        lax.fori_loop(0, nv, body, 0)
        pend_ref[slot] = nv

        @pl.when(n == nt - 1)
        def _():
            drain(slot)
            wait_rows(other)


def _moe(x, comb, wg, wu, wd, g, b, tm):
    T, D = x.shape
    F = MOE_HIDDEN
    tm = min(tm, T)
    tg, nvalid, idx = _moe_tables(comb, tm)
    nt = tg.shape[0]
    wspec = lambda shp: pl.BlockSpec(shp, lambda n, e, tg_r, nv_r: (tg_r[n] * MOE_EXPERTS + e, 0, 0))
    ispec = lambda f: pl.BlockSpec((1, 1, tm), f, memory_space=pltpu.SMEM)
    vec = pl.BlockSpec((1, D), lambda n, e, tg_r, nv_r: (0, 0))
    grid_spec = pltpu.PrefetchScalarGridSpec(
        num_scalar_prefetch=2,
        grid=(nt, MOE_EXPERTS),
        in_specs=[ispec(lambda n, e, tg_r, nv_r: (n, 0, 0)),
                  ispec(lambda n, e, tg_r, nv_r: (jnp.minimum(n + 1, nt - 1), 0, 0)),
                  pl.BlockSpec(memory_space=pl.ANY), pl.BlockSpec(memory_space=pl.ANY),
                  wspec((1, D, F)), wspec((1, D, F)), wspec((1, F, D)), vec, vec],
        out_specs=pl.BlockSpec(memory_space=pl.ANY),
        scratch_shapes=[pltpu.VMEM((2, tm, D), F32), pltpu.VMEM((2, tm, LANE), F32), pltpu.VMEM((2, tm, D), F32),
                        pltpu.SMEM((2,), jnp.int32),
                        pltpu.SemaphoreType.DMA((2,)), pltpu.SemaphoreType.DMA((2,)),
                        pltpu.SemaphoreType.DMA((2,))],
    )
    return pl.pallas_call(
        _moe_kernel,
        grid_spec=grid_spec,
        out_shape=jax.ShapeDtypeStruct((T, D), F32),
        compiler_params=_cparams(("arbitrary", "arbitrary")),
        name="moe_experts_ln",
    )(tg, nvalid, idx, idx, x, comb, wg, wu, wd, g, b)


def _pad_cols(w, n):
    return jnp.pad(w, ((0, 0), (0, n - w.shape[1])))


def _layer(x, xb, l, p, rope_tabs):
    T = x.shape[0]
    W = RWKV_WIDTH
    KW, VW = GLA_K_WIDTH, GLA_V_WIDTH
    lambda_init = 0.8 - 0.6 * math.exp(-0.3 * l)

    w_in = p['w_in'][l]
    o_b = RWKV_IN
    o_c = o_b + DIFF_IN
    o_g = o_c + GLA_IN
    lo1 = 3 * W + RWKV_DECAY_LORA
    lo2 = lo1 + RWKV_AAA_LORA
    w_a = jnp.concatenate([_pad_cols(w_in[:, :lo1], 3 * W + LORA_PAD), _pad_cols(w_in[:, lo1:lo2], LORA_PAD),
                           w_in[:, lo2:o_b]], axis=1).astype(BF16)
    mu = p['rwkv_mu'][l]
    mu_a = jnp.concatenate([jnp.pad(mu[:lo1], (0, LORA_PAD - RWKV_DECAY_LORA)),
                            jnp.pad(mu[lo1:lo2], (0, LORA_PAD - RWKV_AAA_LORA)), mu[lo2:]])[None, :]
    qscale = DIFF_HEAD ** -0.5 * math.log2(math.e)
    w_qk = jnp.concatenate([w_in[:, o_b:o_b + DIFF_QK] * qscale, w_in[:, o_b + DIFF_QK:o_b + 2 * DIFF_QK]],
                           axis=1).astype(BF16)
    w_vt = w_in[:, o_b + 2 * DIFF_QK:o_c].T.astype(BF16)
    c0 = o_c
    w_c = jnp.concatenate([w_in[:, c0:c0 + KW] * (GLA_KEY ** -0.5), w_in[:, c0 + KW:c0 + 2 * KW + VW],
                           w_in[:, c0 + 2 * KW + VW + GLA_GATE_LORA:o_g],
                           _pad_cols(w_in[:, c0 + 2 * KW + VW:c0 + 2 * KW + VW + GLA_GATE_LORA], LANE)],
                          axis=1).astype(BF16)
    w_g = w_in[:, o_g:].astype(BF16)

    proj_a = _matmul(xb, w_a, F32, 1024, 1024)
    qk = _matmul_rope(xb, w_qk, rope_tabs, 1024, 1024)
    tk = min(512, T)
    vt = _matmul_vt(w_vt, xb, tk)
    proj_c = _matmul(xb, w_c, F32, 512, GLA_IN_PAD)
    gates = _matmul(xb, w_g, BF16, 1024, 1024, gate=True)

    padr = lambda w: jnp.pad(w, ((0, LORA_PAD - w.shape[0]), (0, 0)))
    r2 = lambda v: v.reshape(1, -1)
    o_a = _rwkv(proj_a, mu_a, padr(p['rwkv_w2'][l]).astype(BF16), r2(p['rwkv_w0'][l]),
                padr(p['rwkv_a2'][l]).astype(BF16), r2(p['rwkv_a0'][l]), p['rwkv_g2'][l].astype(BF16),
                r2(p['rwkv_kk_scale'][l]), r2(p['rwkv_ka_scale'][l]), r2(p['rwkv_rk'][l]),
                r2(p['rwkv_gn_g'][l]), r2(p['rwkv_gn_b'][l]), 256)
    o_bb = _diff_attention(qk, vt, r2(p['diff_lq1'][l]), r2(p['diff_lk1'][l]), r2(p['diff_lq2'][l]),
                           r2(p['diff_lk2'][l]), p['diff_subln_g'][l].reshape(-1, 1), lambda_init, min(2 * tk, T))
    g2p = jnp.pad(p['gla_g2'][l], ((0, LANE - GLA_GATE_LORA), (0, 0)))
    o_c = _gla(proj_c, g2p, r2(p['gla_gb'][l]), r2(p['gla_gn_g'][l]), 256)

    merged = _merge(o_a, o_bb, o_c, gates, p['proj_a'][l].astype(BF16), p['proj_b'][l].astype(BF16),
                    p['proj_c'][l].astype(BF16), 256)
    wr = _pad_cols(jnp.concatenate([p['router_group_w'][l], p['router_expert_w'][l]], axis=1), LANE)
    br = _pad_cols(jnp.concatenate([p['router_group_b'][l], p['router_expert_b'][l]])[None, :], LANE)
    x1, comb = _mm_ln(merged, p['w_o'][l].astype(BF16), x, r2(p['ln1_g'][l]), r2(p['ln1_b'][l]), wr, br, 512)
    F = MOE_HIDDEN
    wg = p['moe_w_gate'][l].reshape(N_EXPERTS, D_MODEL, F).astype(BF16)
    wu = p['moe_w_up'][l].reshape(N_EXPERTS, D_MODEL, F).astype(BF16)
    wd = p['moe_w_down'][l].reshape(N_EXPERTS, F, D_MODEL).astype(BF16)
    return _moe(x1, comb, wg, wu, wd, r2(p['ln2_g'][l]), r2(p['ln2_b'][l]), MOE_TILE)


def kernel(x, w_in, rwkv_mu, rwkv_w2, rwkv_w0, rwkv_a2, rwkv_a0, rwkv_g2, rwkv_kk_scale, rwkv_ka_scale, rwkv_rk, rwkv_gn_g, rwkv_gn_b, diff_lq1, diff_lk1, diff_lq2, diff_lk2, diff_subln_g, gla_g2, gla_gb, gla_gn_g, proj_a, proj_b, proj_c, w_o, ln1_g, ln1_b, router_group_w, router_group_b, router_expert_w, router_expert_b, moe_w_gate, moe_w_up, moe_w_down, ln2_g, ln2_b):
    p = dict(w_in=w_in, rwkv_mu=rwkv_mu, rwkv_w2=rwkv_w2, rwkv_w0=rwkv_w0, rwkv_a2=rwkv_a2, rwkv_a0=rwkv_a0,
             rwkv_g2=rwkv_g2, rwkv_kk_scale=rwkv_kk_scale, rwkv_ka_scale=rwkv_ka_scale, rwkv_rk=rwkv_rk,
             rwkv_gn_g=rwkv_gn_g, rwkv_gn_b=rwkv_gn_b, diff_lq1=diff_lq1, diff_lk1=diff_lk1, diff_lq2=diff_lq2,
             diff_lk2=diff_lk2, diff_subln_g=diff_subln_g, gla_g2=gla_g2, gla_gb=gla_gb, gla_gn_g=gla_gn_g,
             proj_a=proj_a, proj_b=proj_b, proj_c=proj_c, w_o=w_o, ln1_g=ln1_g, ln1_b=ln1_b,
             router_group_w=router_group_w, router_group_b=router_group_b, router_expert_w=router_expert_w,
             router_expert_b=router_expert_b, moe_w_gate=moe_w_gate, moe_w_up=moe_w_up, moe_w_down=moe_w_down,
             ln2_g=ln2_g, ln2_b=ln2_b)
    B, T, D = x.shape
    assert B == 1 and D == D_MODEL
    xf = x.reshape(T, D)
    xb = xf.astype(BF16)
    tabs = _rope_tables(T)
    for l in range(DEPTH):
        xf = _layer(xf, xb, l, p, tabs)
        xb = xf.astype(BF16)
    return xf.reshape(B, T, D)
```

```python
import functools
import math

import numpy as np
import jax
import jax.numpy as jnp
from jax import lax
from jax.experimental import pallas as pl
from jax.experimental.pallas import tpu as pltpu

F32 = jnp.float32
BF16 = jnp.bfloat16
HIGHEST = lax.Precision.HIGHEST

D_MODEL = 2048
DEPTH = 2
DEEPNORM_ALPHA = (2 * DEPTH) ** 0.25
LN_EPS = 1e-5

RWKV_HEADS = 8
RWKV_HEAD = 64
RWKV_WIDTH = RWKV_HEADS * RWKV_HEAD
RWKV_DECAY_LORA = 96
RWKV_AAA_LORA = 96
RWKV_GATE_LORA = 256
RWKV_GN_EPS = 64e-5
RWKV_IN = 3 * RWKV_WIDTH + RWKV_DECAY_LORA + RWKV_AAA_LORA + RWKV_GATE_LORA
RWKV_CHUNK = 64
LORA_PAD = 128
RWKV_IN_PAD = 3 * RWKV_WIDTH + 2 * LORA_PAD + RWKV_GATE_LORA

DIFF_HEADS = 8
DIFF_HEAD = 64
DIFF_QK = DIFF_HEADS * 2 * DIFF_HEAD
DIFF_V = DIFF_HEADS * 2 * DIFF_HEAD
DIFF_IN = 2 * DIFF_QK + DIFF_V
ROPE_THETA = 500000.0
ROPE_DIM = DIFF_HEAD // 4
ROPE_HALF = ROPE_DIM // 2

GLA_HEADS = 4
GLA_KEY = 64
GLA_VALUE = 128
GLA_K_WIDTH = GLA_HEADS * GLA_KEY
GLA_V_WIDTH = GLA_HEADS * GLA_VALUE
GLA_GATE_LORA = 16
GLA_TAU = 16.0
GLA_SUB = 16
GLA_IN = 2 * GLA_K_WIDTH + 2 * GLA_V_WIDTH + GLA_GATE_LORA
GLA_IN_PAD = 2 * GLA_K_WIDTH + 2 * GLA_V_WIDTH + 128

N_BRANCHES = 3
MOE_GROUPS = 4
MOE_EXPERTS = 8
MOE_HIDDEN = 256
N_EXPERTS = MOE_GROUPS * MOE_EXPERTS
MOE_TILE = 1024
LANE = 128
V7X_VMEM_LIMIT = 56 * 1024 * 1024


def _cparams(sem):
    return pltpu.CompilerParams(dimension_semantics=sem, vmem_limit_bytes=V7X_VMEM_LIMIT)


def _nt(a, b, **kw):
    return lax.dot_general(a, b, (((1,), (1,)), ((), ())), preferred_element_type=F32, **kw)


def _tn(a, b, **kw):
    return lax.dot_general(a, b, (((0,), (0,)), ((), ())), preferred_element_type=F32, **kw)


def _dot(a, b, **kw):
    return jnp.dot(a, b, preferred_element_type=F32, **kw)


def _mm_kernel(a_ref, b_ref, o_ref, *, gate):
    acc = _dot(a_ref[...], b_ref[...])
    if gate:
        acc = jax.nn.sigmoid(acc)
    o_ref[...] = acc.astype(o_ref.dtype)


def _matmul(a, b, out_dtype, tm, tn, gate=False):
    M, K = a.shape
    N = b.shape[1]
    tm, tn = min(tm, M), min(tn, N)
    return pl.pallas_call(
        functools.partial(_mm_kernel, gate=gate),
        grid=(N // tn, M // tm),
        in_specs=[pl.BlockSpec((tm, K), lambda j, i: (i, 0)),
                  pl.BlockSpec((K, tn), lambda j, i: (0, j))],
        out_specs=pl.BlockSpec((tm, tn), lambda j, i: (i, j)),
        out_shape=jax.ShapeDtypeStruct((M, N), out_dtype),
        compiler_params=_cparams(("arbitrary", "arbitrary")),
        name="proj_mm",
    )(a, b)


def _mm_rope_kernel(a_ref, b_ref, c_ref, sa_ref, sb_ref, o_ref):
    acc = _dot(a_ref[...], b_ref[...])
    c, sa, sb = c_ref[...], sa_ref[...], sb_ref[...]
    for h in range(acc.shape[1] // LANE):
        blk = acc[:, h * LANE:(h + 1) * LANE]
        out = blk * c + pltpu.roll(blk, LANE - ROPE_HALF, 1) * sa + pltpu.roll(blk, ROPE_HALF, 1) * sb
        o_ref[:, h * LANE:(h + 1) * LANE] = out.astype(o_ref.dtype)


def _matmul_rope(a, b, tabs, tm, tn):
    M, K = a.shape
    N = b.shape[1]
    tm, tn = min(tm, M), min(tn, N)
    tab_spec = pl.BlockSpec((tm, LANE), lambda j, i: (i, 0))
    return pl.pallas_call(
        _mm_rope_kernel,
        grid=(N // tn, M // tm),
        in_specs=[pl.BlockSpec((tm, K), lambda j, i: (i, 0)),
                  pl.BlockSpec((K, tn), lambda j, i: (0, j)),
                  tab_spec, tab_spec, tab_spec],
        out_specs=pl.BlockSpec((tm, tn), lambda j, i: (i, j)),
        out_shape=jax.ShapeDtypeStruct((M, N), BF16),
        compiler_params=_cparams(("arbitrary", "arbitrary")),
        name="proj_qk_rope",
    )(a, b, *tabs)


def _mm_vt_kernel(wt_ref, x_ref, o_ref):
    o_ref[0] = _nt(wt_ref[...], x_ref[...]).astype(o_ref.dtype)


def _matmul_vt(wt, x, tk):
    N, K = wt.shape
    T = x.shape[0]
    return pl.pallas_call(
        _mm_vt_kernel,
        grid=(T // tk,),
        in_specs=[pl.BlockSpec((N, K), lambda i: (0, 0)),
                  pl.BlockSpec((tk, K), lambda i: (i, 0))],
        out_specs=pl.BlockSpec((1, N, tk), lambda i: (i, 0, 0)),
        out_shape=jax.ShapeDtypeStruct((T // tk, N, tk), BF16),
        compiler_params=_cparams(("arbitrary",)),
        name="proj_vt",
    )(wt, x)


def _rope_tables(T):
    pos = jnp.arange(T, dtype=F32)[:, None]
    inv_freq = ROPE_THETA ** (-jnp.arange(ROPE_HALF, dtype=F32) / ROPE_HALF)
    lane = np.arange(LANE) % DIFF_HEAD
    ang = pos * inv_freq[None, :]
    cos8, sin8 = jnp.cos(ang), jnp.sin(ang)
    idx = jnp.asarray(lane % ROPE_HALF)
    cos_l, sin_l = cos8[:, idx], sin8[:, idx]
    first = jnp.asarray(lane < ROPE_HALF)[None, :]
    second = jnp.asarray((lane >= ROPE_HALF) & (lane < ROPE_DIM))[None, :]
    c = jnp.where(first | second, cos_l, 1.0)
    sa = jnp.where(first, -sin_l, 0.0)
    sb = jnp.where(second, sin_l, 0.0)
    return c, sa, sb


def _rwkv_kernel(a_ref, mu_ref, w2_ref, w0_ref, a2_ref, a0_ref, g2_ref, kks_ref, kas_ref, rk_ref,
                 gng_ref, gnb_ref, seg_ref, tri_ref, o_ref,
                 prev_ref, h_ref, at_ref, rt_ref, bt_ref, kt_ref, bp_ref, kp_ref, v_ref, gc_ref, y_ref):
    TB = a_ref.shape[0]
    C, N, W = RWKV_CHUNK, RWKV_HEAD, RWKV_WIDTH

    @pl.when(pl.program_id(0) == 0)
    def _():
        prev_ref[...] = jnp.zeros_like(prev_ref)
        h_ref[...] = jnp.zeros_like(h_ref)

    a = a_ref[...]
    row = lax.broadcasted_iota(jnp.int32, a.shape, 0)
    shifted = jnp.where(row == 0, prev_ref[...], pltpu.roll(a, 1, 0))
    prev_ref[...] = a[TB - 1:TB, :]
    xs = a + (shifted - a) * mu_ref[...]

    r = xs[:, 0:W]
    k = xs[:, W:2 * W]
    v = xs[:, 2 * W:3 * W]
    w_lo = xs[:, 3 * W:3 * W + LORA_PAD]
    a_lo = xs[:, 3 * W + LORA_PAD:3 * W + 2 * LORA_PAD]
    g_lo = xs[:, 3 * W + 2 * LORA_PAD:]

    z = w0_ref[...] + _dot(jnp.tanh(w_lo).astype(BF16), w2_ref[...])
    w = -jax.nn.softplus(-z) - 0.5
    lw = -jnp.exp(w)
    aa = jax.nn.sigmoid(a0_ref[...] + _dot(a_lo.astype(BF16), a2_ref[...]))
    g = _dot(jax.nn.sigmoid(g_lo).astype(BF16), g2_ref[...])

    seg = seg_ref[...]
    kk = k * kks_ref[...]
    ss = _dot((kk * kk).astype(BF16), seg)
    kk = kk * lax.rsqrt(jnp.maximum(ss, 1e-24))
    km = k * (1.0 + (aa - 1.0) * kas_ref[...])
    beta = kk * aa

    tri = tri_ref[...]
    cum = _dot(tri, lw, precision=HIGHEST)
    rowc = lax.broadcasted_iota(jnp.int32, (TB, TB), 0) // C
    colc = lax.broadcasted_iota(jnp.int32, (TB, TB), 1) // C
    allc = jnp.where(rowc == colc, 1.0, 0.0).astype(F32)
    cum_c = _dot(allc, lw, precision=HIGHEST)

    e_in = jnp.exp(cum)
    e_neg = jnp.exp(-cum)
    e_end = jnp.exp(cum_c - cum)
    at_ref[...] = (-kk * jnp.exp(cum - lw)).astype(BF16)
    rt_ref[...] = (r * e_in).astype(BF16)
    bt_ref[...] = (beta * e_neg).astype(BF16)
    kt_ref[...] = (km * e_neg).astype(BF16)
    bp_ref[...] = (beta * e_end).astype(BF16)
    kp_ref[...] = (km * e_end).astype(BF16)
    v_ref[...] = v.astype(BF16)
    gc_ref[...] = jnp.exp(cum_c)

    ri = lax.broadcasted_iota(jnp.int32, (C, C), 0)
    ci = lax.broadcasted_iota(jnp.int32, (C, C), 1)
    strict = ci < ri
    incl = ci <= ri
    eye = ci == ri

    def chunk(c, carry):
        NC = 2
        rows = [pl.ds(pl.multiple_of((c * NC + ci) * C, C), C) for ci in range(NC)]
        HS = range(NC * RWKV_HEADS)
        rw = [rows[i // RWKV_HEADS] for i in HS]
        sl = [slice((i % RWKV_HEADS) * N, (i % RWKV_HEADS + 1) * N) for i in HS]
        cat = lambda a, b: jnp.concatenate([a, b], axis=0)
        at = [at_ref[rw[i], sl[i]] for i in HS]
        rt = [rt_ref[rw[i], sl[i]] for i in HS]
        bt = [bt_ref[rw[i], sl[i]] for i in HS]
        kt = [kt_ref[rw[i], sl[i]] for i in HS]
        bp = [bp_ref[rw[i], sl[i]] for i in HS]
        kp = [kp_ref[rw[i], sl[i]] for i in HS]
        vh = [v_ref[rw[i], sl[i]] for i in HS]
        ar = [cat(at[h], rt[h]) for h in HS]
        xb = [_nt(ar[h], bt[h]) for h in HS]
        xk = [_nt(ar[h], kt[h]) for h in HS]
        a_ab = [jnp.where(strict, xb[h][:C], 0.0) for h in HS]
        a_rb = [jnp.where(incl, xb[h][C:], 0.0).astype(BF16) for h in HS]
        a_ak = [jnp.where(strict, xk[h][:C], 0.0).astype(BF16) for h in HS]
        a_rk = [jnp.where(incl, xk[h][C:], 0.0).astype(BF16) for h in HS]
        ident = jnp.where(eye, 1.0, 0.0)
        t_inv = [ident + a_ab[h] for h in HS]
        ap = a_ab
        for _ in range(5):
            apb = [ap[h].astype(BF16) for h in HS]
            ap = [_dot(apb[h], apb[h]) for h in HS]
            t_inv = [t_inv[h] + _dot(t_inv[h].astype(BF16), ap[h].astype(BF16)) for h in HS]
        tb = [t_inv[h].astype(BF16) for h in HS]
        wv = [_dot(a_ak[h], vh[h]).astype(BF16) for h in HS]
        pb = [_dot(tb[h], at[h]).astype(BF16) for h in HS]
        u0b = [_dot(tb[h], wv[h]).astype(BF16) for h in HS]
        qm = [rt[h].astype(F32) + _dot(a_rb[h], pb[h]) for h in HS]
        y0 = [_dot(a_rb[h], u0b[h]) + _dot(a_rk[h], vh[h]) for h in HS]
        gm = [jnp.where(eye, gc_ref[rw[h], sl[h]][0:1, :], 0.0) + _tn(bp[h], pb[h]) for h in HS]
        hadd = [_tn(cat(bp[h], kp[h]), cat(u0b[h], vh[h])) for h in HS]
        qg = [cat(qm[h], gm[h]).astype(BF16) for h in HS]
        for i in HS:
            hd = i % RWKV_HEADS
            z = _dot(qg[i], h_ref[hd].astype(BF16))
            y_ref[rw[i], sl[i]] = z[:C] + y0[i]
            h_ref[hd] = z[C:] + hadd[i]
        return carry

    lax.fori_loop(0, TB // (2 * C), chunk, 0)

    y = y_ref[...]
    segm = seg_ref[...]
    mu = _dot(y.astype(BF16), segm) * (1.0 / N)
    yc = y - mu
    var = _dot((yc * yc).astype(BF16), segm) * (1.0 / N)
    yn = yc * lax.rsqrt(var + RWKV_GN_EPS) * gng_ref[...] + gnb_ref[...]
    bonus = _dot((r * km * rk_ref[...]).astype(BF16), segm) * v
    o_ref[...] = ((yn + bonus) * g).astype(o_ref.dtype)


def _rwkv(a_proj, mu, w2, w0, a2, a0, g2, kks, kas, rk, gng, gnb, tb):
    T = a_proj.shape[0]
    tb = min(tb, T)
    W = RWKV_WIDTH
    hid = np.arange(W) // RWKV_HEAD
    seg = jnp.asarray(hid[:, None] == hid[None, :], dtype=BF16)
    t_idx = np.arange(tb)
    tri = jnp.asarray((t_idx[:, None] // RWKV_CHUNK == t_idx[None, :] // RWKV_CHUNK)
                      & (t_idx[None, :] <= t_idx[:, None]), dtype=F32)
    full = lambda shp: pl.BlockSpec(shp, lambda i: (0,) * len(shp))
    row = lambda n: full((1, n))
    bf = lambda: pltpu.VMEM((tb, W), BF16)
    return pl.pallas_call(
        _rwkv_kernel,
        grid=(T // tb,),
        in_specs=[pl.BlockSpec((tb, RWKV_IN_PAD), lambda i: (i, 0)),
                  row(RWKV_IN_PAD), full((LORA_PAD, W)), row(W), full((LORA_PAD, W)), row(W),
                  full((RWKV_GATE_LORA, W)), row(W), row(W), row(W), row(W), row(W),
                  full((W, W)), full((tb, tb))],
        out_specs=pl.BlockSpec((tb, W), lambda i: (i, 0)),
        out_shape=jax.ShapeDtypeStruct((T, W), BF16),
        scratch_shapes=[pltpu.VMEM((1, RWKV_IN_PAD), F32),
                        pltpu.VMEM((RWKV_HEADS, RWKV_HEAD, RWKV_HEAD), F32),
                        bf(), bf(), bf(), bf(), bf(), bf(), bf(),
                        pltpu.VMEM((tb, W), F32), pltpu.VMEM((tb, W), F32)],
        compiler_params=_cparams(("arbitrary",)),
        name="rwkv7",
    )(a_proj, mu, w2, w0, a2, a0, g2, kks, kas, rk, gng, gnb, seg, tri)


def _diff_kernel(q_ref, k_ref, vt_ref, lq1_ref, lk1_ref, lq2_ref, lk2_ref, sg_ref, o_ref,
                 m_ref, l_ref, acc_ref, s_ref, p_ref, a_ref, *, lambda_init):
    tq = q_ref.shape[0]
    nkb, _, tk = vt_ref.shape
    i = pl.program_id(1)
    q = q_ref[...]
    lane = lax.broadcasted_iota(jnp.int32, q.shape, 1)
    qm = (jnp.where(lane < DIFF_HEAD, q, jnp.zeros_like(q)),
          jnp.where(lane >= DIFF_HEAD, q, jnp.zeros_like(q)))
    m_ref[...] = jnp.full(m_ref.shape, -jnp.inf, F32)
    l_ref[...] = jnp.zeros_like(l_ref)
    acc_ref[...] = jnp.zeros_like(acc_ref)
    p_ref[1] = jnp.zeros(p_ref.shape[1:], BF16)
    a_ref[1] = jnp.ones(a_ref.shape[1:], F32)

    def scores(j, slot):
        jc = jnp.minimum(j, nkb - 1)
        kj = k_ref[pl.ds(pl.multiple_of(jc * tk, tk), tk), :]
        for mi in range(2):
            s_ref[slot, mi] = _nt(kj, qm[mi])

    def values(j, slot):
        vj = vt_ref[jnp.clip(j, 0, nkb - 1)]
        for mi in range(2):
            acc_ref[mi] = a_ref[slot, mi] * acc_ref[mi] + _dot(vj, p_ref[slot, mi])

    def softmax(j, slot, masked):
        for mi in range(2):
            s = s_ref[slot, mi]
            if masked:
                d = (lax.broadcasted_iota(jnp.int32, (tk, tq), 0)
                     - lax.broadcasted_iota(jnp.int32, (tk, tq), 1))
                s = jnp.where(d <= i * tq - j * tk, s, -jnp.inf)
            m_old = m_ref[mi]
            m_new = jnp.maximum(m_old, jnp.max(s, axis=0, keepdims=True))
            alpha = jnp.exp2(m_old - m_new)
            p = jnp.exp2(s - m_new)
            l_ref[mi] = alpha * l_ref[mi] + jnp.sum(p, axis=0, keepdims=True)
            m_ref[mi] = m_new
            a_ref[slot, mi] = alpha
            p_ref[slot, mi] = p.astype(BF16)

    def half(j, cur, masked):
        nxt = 1 - cur
        scores(j + 1, nxt)
        values(j - 1, nxt)
        softmax(j, cur, masked)

    def pair(jj, carry):
        half(2 * jj, 0, False)
        half(2 * jj + 1, 1, False)
        return carry

    scores(0, 0)
    n_pairs = ((i + 1) * (tq // tk) + 1) // 2
    lax.fori_loop(0, n_pairs - 1, pair, 0)
    j_last = 2 * (n_pairs - 1)
    half(j_last, 0, True)
    half(j_last + 1, 1, True)
    values(j_last + 1, 1)

    lam = (jnp.exp(jnp.sum(lq1_ref[...] * lk1_ref[...])) - jnp.exp(jnp.sum(lq2_ref[...] * lk2_ref[...]))
           + lambda_init)
    o = acc_ref[0] / l_ref[0] - lam * (acc_ref[1] / l_ref[1])
    o = o * lax.rsqrt(jnp.mean(o * o, axis=0, keepdims=True) + LN_EPS) * (1.0 - lambda_init)
    o = o * sg_ref[...]
    o_ref[...] = o.T.astype(o_ref.dtype)


def _diff_attention(qk, vt, lq1, lk1, lq2, lk2, subln_g, lambda_init, tq):
    T = qk.shape[0]
    tk = vt.shape[2]
    assert tq % tk == 0 and T % tq == 0
    H, E = DIFF_HEADS, 2 * DIFF_HEAD
    vec = lambda: pl.BlockSpec((1, DIFF_HEAD), lambda h, i: (0, 0))
    return pl.pallas_call(
        functools.partial(_diff_kernel, lambda_init=lambda_init),
        grid=(H, T // tq),
        in_specs=[pl.BlockSpec((tq, E), lambda h, i: (i, h)),
                  pl.BlockSpec((T, E), lambda h, i: (0, H + h)),
                  pl.BlockSpec((T // tk, E, tk), lambda h, i: (0, h, 0)),
                  vec(), vec(), vec(), vec(),
                  pl.BlockSpec((E, 1), lambda h, i: (0, 0))],
        out_specs=pl.BlockSpec((tq, E), lambda h, i: (i, h)),
        out_shape=jax.ShapeDtypeStruct((T, DIFF_V), BF16),
        scratch_shapes=[pltpu.VMEM((2, 1, tq), F32), pltpu.VMEM((2, 1, tq), F32),
                        pltpu.VMEM((2, E, tq), F32), pltpu.VMEM((2, 2, tk, tq), F32),
                        pltpu.VMEM((2, 2, tk, tq), BF16), pltpu.VMEM((2, 2, 1, tq), F32)],
        compiler_params=_cparams(("arbitrary", "arbitrary")),
        name="diff_attn",
    )(qk, qk, vt, lq1, lk1, lq2, lk2, subln_g)


def _gla_kernel(c_ref, g2_ref, gb_ref, gng_ref, tri_ref, segv_ref, o_ref,
                st_ref, oacc_ref):
    TB = c_ref.shape[0]
    KW, VW, S = GLA_K_WIDTH, GLA_V_WIDTH, GLA_SUB
    DK, DV = GLA_KEY, GLA_VALUE

    @pl.when(pl.program_id(0) == 0)
    def _():
        st_ref[...] = jnp.zeros_like(st_ref)

    g_lo = c_ref[:, 2 * KW + 2 * VW:]
    gate = _dot(g_lo, g2_ref[...], precision=HIGHEST) + gb_ref[...]
    log_a = jax.nn.log_sigmoid(gate) * (1.0 / GLA_TAU)
    b = _dot(tri_ref[...], log_a, precision=HIGHEST)
    NS = TB // S
    b3 = b.reshape(NS, S, KW)
    q3 = c_ref[:, 0:KW].reshape(NS, S, KW)
    k3 = c_ref[:, KW:2 * KW].reshape(NS, S, KW)
    v = c_ref[:, 2 * KW:2 * KW + VW]
    v3 = v.reshape(NS, S, VW)
    bl3 = b3[:, S - 1:S, :]
    qb = (q3 * jnp.exp(b3)).astype(BF16).reshape(TB, KW)
    kb = (k3 * jnp.exp(bl3 - b3)).astype(BF16).reshape(TB, KW)
    dec = jnp.exp(bl3)
    vb = v.astype(BF16)

    ri = lax.broadcasted_iota(jnp.int32, (NS, S, 1), 1)
    segv = segv_ref[...]
    acc = jnp.zeros((NS, S, VW), F32)
    for j in range(S):
        e = jnp.exp(jnp.minimum(b3 - b3[:, j:j + 1, :], 0.0))
        t = (q3 * e * k3[:, j:j + 1, :]).astype(BF16).reshape(TB, KW)
        att = _dot(t, segv).reshape(NS, S, VW)
        acc = acc + jnp.where(ri >= j, att, 0.0) * v3[:, j:j + 1, :]
    oacc_ref[...] = acc.reshape(TB, VW)

    for h in range(GLA_HEADS):
        lk = slice(h * DK, (h + 1) * DK)
        lv = slice(h * DV, (h + 1) * DV)
        kv = [_tn(vb[s * S:(s + 1) * S, lv], kb[s * S:(s + 1) * S, lk]) for s in range(NS)]
        st = st_ref[h]
        for s in range(NS):
            rows = slice(s * S, (s + 1) * S)
            oacc_ref[rows, lv] += _nt(qb[rows, lk], st.astype(BF16))
            st = st * dec[s][:, lk] + kv[s]
        st_ref[h] = st

    o = oacc_ref[...]
    rgate = c_ref[:, 2 * KW + VW:2 * KW + 2 * VW]
    for h in range(GLA_HEADS):
        lv = slice(h * DV, (h + 1) * DV)
        oh = o[:, lv]
        oh = oh * lax.rsqrt(jnp.mean(oh * oh, axis=1, keepdims=True) + LN_EPS) * gng_ref[:, lv]
        o_ref[:, lv] = (oh * jax.nn.silu(rgate[:, lv])).astype(o_ref.dtype)


def _gla(c_proj, g2, gb, gng, tb):
    T = c_proj.shape[0]
    tb = min(tb, T)
    KW, VW = GLA_K_WIDTH, GLA_V_WIDTH
    t_idx = np.arange(tb)
    tri = jnp.asarray((t_idx[:, None] // GLA_SUB == t_idx[None, :] // GLA_SUB)
                      & (t_idx[None, :] <= t_idx[:, None]), dtype=F32)
    segv = jnp.asarray((np.arange(KW) // GLA_KEY)[:, None] == (np.arange(VW) // GLA_VALUE)[None, :], dtype=BF16)
    full = lambda shp: pl.BlockSpec(shp, lambda i: (0,) * len(shp))
    return pl.pallas_call(
        _gla_kernel,
        grid=(T // tb,),
        in_specs=[pl.BlockSpec((tb, GLA_IN_PAD), lambda i: (i, 0)),
                  full((LANE, KW)), full((1, KW)), full((1, VW)), full((tb, tb)), full((KW, VW))],
        out_specs=pl.BlockSpec((tb, VW), lambda i: (i, 0)),
        out_shape=jax.ShapeDtypeStruct((T, VW), BF16),
        scratch_shapes=[pltpu.VMEM((GLA_HEADS, GLA_VALUE, GLA_KEY), F32),
                        pltpu.VMEM((tb, VW), F32)],
        compiler_params=_cparams(("arbitrary",)),
        name="gla",
    )(c_proj, g2, gb, gng, tri, segv)


def _merge_kernel(oa_ref, ob_ref, oc_ref, g_ref, pa_ref, pb_ref, pc_ref, o_ref):
    D = D_MODEL
    m = g_ref[:, 0:D].astype(F32) * _dot(oa_ref[...], pa_ref[...])
    m = m + g_ref[:, D:2 * D].astype(F32) * _dot(ob_ref[...], pb_ref[...])
    m = m + g_ref[:, 2 * D:3 * D].astype(F32) * _dot(oc_ref[...], pc_ref[...])
    o_ref[...] = m.astype(o_ref.dtype)


def _merge(oa, ob, oc, gates, pa, pb, pc, tm):
    T = oa.shape[0]
    tm = min(tm, T)
    D = D_MODEL
    rowblk = lambda n: pl.BlockSpec((tm, n), lambda i: (i, 0))
    full = lambda shp: pl.BlockSpec(shp, lambda i: (0, 0))
    return pl.pallas_call(
        _merge_kernel,
        grid=(T // tm,),
        in_specs=[rowblk(RWKV_WIDTH), rowblk(DIFF_V), rowblk(GLA_V_WIDTH), rowblk(3 * D),
                  full((RWKV_WIDTH, D)), full((DIFF_V, D)), full((GLA_V_WIDTH, D))],
        out_specs=rowblk(D),
        out_shape=jax.ShapeDtypeStruct((T, D), BF16),
        compiler_params=_cparams(("arbitrary",)),
        name="merge",
    )(oa, ob, oc, gates, pa, pb, pc)


def _layer_norm(z, g, b):
    mu = jnp.mean(z, axis=1, keepdims=True)
    zc = z - mu
    var = jnp.mean(zc * zc, axis=1, keepdims=True)
    return zc * lax.rsqrt(var + LN_EPS) * g + b


def _mm_ln_kernel(a_ref, w_ref, res_ref, g_ref, b_ref, wrh_ref, wrl_ref, br_ref, o_ref, c_ref):
    z = DEEPNORM_ALPHA * res_ref[...] + _dot(a_ref[...], w_ref[...])
    out = _layer_norm(z, g_ref[...], b_ref[...])
    o_ref[...] = out
    c_ref[...] = _route(out, wrh_ref[...], wrl_ref[...], br_ref[...])


def _mm_ln(a, w, res, g, b, wr, br, tm):
    wr_hi = wr.astype(BF16)
    wr_lo = (wr - wr_hi.astype(F32)).astype(BF16)
    T, K = a.shape
    D = D_MODEL
    tm = min(tm, T)
    rowblk = lambda n: pl.BlockSpec((tm, n), lambda i: (i, 0))
    full = lambda shp: pl.BlockSpec(shp, lambda i: (0, 0))
    return pl.pallas_call(
        _mm_ln_kernel,
        grid=(T // tm,),
        in_specs=[rowblk(K), full((K, D)), rowblk(D), full((1, D)), full((1, D)), full((D, LANE)), full((D, LANE)),
                  full((1, LANE))],
        out_specs=[rowblk(D), rowblk(LANE)],
        out_shape=[jax.ShapeDtypeStruct((T, D), F32), jax.ShapeDtypeStruct((T, LANE), F32)],
        compiler_params=_cparams(("arbitrary",)),
        name="wo_ln_route",
    )(a, w, res, g, b, wr_hi, wr_lo, br)


def _route(x, w_hi, w_lo, b):
    G, E = MOE_GROUPS, MOE_EXPERTS
    x_hi = x.astype(BF16)
    x_lo = (x - x_hi.astype(F32)).astype(BF16)
    logits = _dot(x_hi, w_hi) + (_dot(x_lo, w_hi) + _dot(x_hi, w_lo)) + b
    lane = lax.broadcasted_iota(jnp.int32, logits.shape, 1)
    neg = -jnp.inf
    big = jnp.int32(1 << 20)
    is_g = lane < G
    lg = jnp.where(is_g, logits, neg)
    gmax = jnp.max(lg, axis=1, keepdims=True)
    gidx = jnp.min(jnp.where(is_g & (lg == gmax), lane, big), axis=1, keepdims=True)
    g_p = 1.0 / jnp.sum(jnp.exp(lg - gmax), axis=1, keepdims=True)
    lo = G + gidx * E
    in_grp = (lane >= lo) & (lane < lo + E)
    le = jnp.where(in_grp, logits, neg)
    m1 = jnp.max(le, axis=1, keepdims=True)
    i1 = jnp.min(jnp.where(in_grp & (le == m1), lane, big), axis=1, keepdims=True)
    le2 = jnp.where(lane == i1, neg, le)
    m2 = jnp.max(le2, axis=1, keepdims=True)
    i2 = jnp.min(jnp.where(in_grp & (le2 == m2), lane, big), axis=1, keepdims=True)
    e2 = jnp.exp(m2 - m1)
    p1 = 1.0 / (1.0 + e2)
    p2 = e2 / (1.0 + e2)
    comb = jnp.where(lane == i1, g_p * p1, 0.0) + jnp.where(lane == i2, g_p * p2, 0.0)
    return jnp.where(lane == 0, gidx.astype(F32), comb)


def _moe_tables(comb, tm):
    T = comb.shape[0]
    G = MOE_GROUPS
    nt = T // tm + G
    gid = comb[:, 0].astype(jnp.int32)
    order = jnp.argsort(gid).astype(jnp.int32)
    counts = jnp.sum((gid[:, None] == jnp.arange(G)[None, :]).astype(jnp.int32), axis=0)
    starts = jnp.cumsum(counts) - counts
    ntile = (counts + tm - 1) // tm
    tstart = jnp.cumsum(ntile) - ntile
    n = jnp.arange(nt, dtype=jnp.int32)
    tg = jnp.sum((n[:, None] >= tstart[None, 1:]).astype(jnp.int32), axis=1)
    k = n - tstart[tg]
    row0 = starts[tg] + k * tm
    nvalid = jnp.clip(counts[tg] - k * tm, 0, tm).astype(jnp.int32)
    rows = jnp.clip(row0[:, None] + jnp.arange(tm, dtype=jnp.int32)[None, :], 0, T - 1)
    idx = jnp.take(order, rows).reshape(nt, 1, tm)
    return tg.astype(jnp.int32), nvalid, idx


def _moe_kernel(tg_ref, nv_ref, idx_ref, idxn_ref, x_hbm, comb_hbm, wg_ref, wu_ref, wd_ref, g_ref, b_ref, y_hbm,
                xs_ref, cs_ref, xb_ref, h_ref, out_ref, pend_ref, gx_sem, gc_sem, sc_sem):
    n = pl.program_id(0)
    e = pl.program_id(1)
    nt = pl.num_programs(0)
    nv = nv_ref[n]
    slot = n % 2
    other = 1 - slot
    tm = out_ref.shape[0]
    part = tm // MOE_EXPERTS
    F = MOE_HIDDEN

    def start_row(idx, dst_slot, r):
        tok = idx[0, 0, r]
        pltpu.make_async_copy(x_hbm.at[pl.ds(tok, 1)], xs_ref.at[dst_slot, pl.ds(r, 1)],
                              gx_sem.at[dst_slot]).start()
        pltpu.make_async_copy(comb_hbm.at[pl.ds(tok, 1)], cs_ref.at[dst_slot, pl.ds(r, 1)],
                              gc_sem.at[dst_slot]).start()

    def start_next_part():
        for r in range(part):
            start_row(idxn_ref, other, e * part + r)

    def wait_rows(s):
        pltpu.make_async_copy(x_hbm.at[pl.ds(0, tm)], xs_ref.at[s], gx_sem.at[s]).wait()
        pltpu.make_async_copy(comb_hbm.at[pl.ds(0, tm)], cs_ref.at[s], gc_sem.at[s]).wait()

    def drain():
        @pl.when(pend_ref[0] == tm)
        def _():
            pltpu.make_async_copy(out_ref, y_hbm.at[pl.ds(0, tm)], sc_sem.at[0]).wait()

        @pl.when(pend_ref[0] < tm)
        def _():
            def body(r, c):
                pltpu.make_async_copy(out_ref.at[pl.ds(0, 1)], y_hbm.at[pl.ds(0, 1)], sc_sem.at[0]).wait()
                return c
            lax.fori_loop(0, pend_ref[0], body, 0)
        pend_ref[0] = 0

    @pl.when((n == 0) & (e == 0))
    def _():
        pend_ref[0] = 0

        def body(r, c):
            start_row(idx_ref, 0, r)
            return c
        lax.fori_loop(0, tm, body, 0)

    @pl.when(e == 0)
    def _():
        wait_rows(slot)
        xb_ref[...] = xs_ref[slot].astype(BF16)

    @pl.when(nv > 0)
    def _():
        start_next_part()
        xb = xb_ref[...]
        hg = _dot(xb, wg_ref[0])
        hu = _dot(xb, wu_ref[0])
        comb = cs_ref[slot]
        lane = lax.broadcasted_iota(jnp.int32, comb.shape, 1)
        col = MOE_GROUPS + tg_ref[n] * MOE_EXPERTS + e
        cw = jnp.sum(jnp.where(lane == col, comb, 0.0), axis=1, keepdims=True)
        h_ref[e] = (jax.nn.silu(hg) * hu * cw).astype(BF16)

    @pl.when(nv == 0)
    def _():
        start_next_part()

    @pl.when(e == MOE_EXPERTS - 1)
    def _():
        drain()

        @pl.when(nv > 0)
        def _():
            y = _dot(h_ref[0], wd_ref[0, 0:F, :])
            for j in range(1, MOE_EXPERTS):
                y = y + _dot(h_ref[j], wd_ref[0, j * F:(j + 1) * F, :])
            z = DEEPNORM_ALPHA * xs_ref[slot] + y
            out_ref[...] = _layer_norm(z, g_ref[...], b_ref[...])

        def body(r, c):
            tok = idx_ref[0, 0, r]
            pltpu.make_async_copy(out_ref.at[pl.ds(r, 1)], y_hbm.at[pl.ds(tok, 1)], sc_sem.at[0]).start()
            return c
        lax.fori_loop(0, nv, body, 0)
        pend_ref[0] = nv

        @pl.when(n == nt - 1)
        def _():
            drain()
            wait_rows(other)


def _moe(x, comb, wg, wu, wd, g, b, tm):
    T, D = x.shape
    F = MOE_HIDDEN
    tm = min(tm, T)
    tg, nvalid, idx = _moe_tables(comb, tm)
    nt = tg.shape[0]
    wspec = lambda shp: pl.BlockSpec(shp, lambda n, e, tg_r, nv_r: (tg_r[n] * MOE_EXPERTS + e, 0, 0))
    ispec = lambda f: pl.BlockSpec((1, 1, tm), f, memory_space=pltpu.SMEM)
    vec = pl.BlockSpec((1, D), lambda n, e, tg_r, nv_r: (0, 0))
    grid_spec = pltpu.PrefetchScalarGridSpec(
        num_scalar_prefetch=2,
        grid=(nt, MOE_EXPERTS),
        in_specs=[ispec(lambda n, e, tg_r, nv_r: (n, 0, 0)),
                  ispec(lambda n, e, tg_r, nv_r: (jnp.minimum(n + 1, nt - 1), 0, 0)),
                  pl.BlockSpec(memory_space=pl.ANY), pl.BlockSpec(memory_space=pl.ANY),
                  wspec((1, D, F)), wspec((1, D, F)),
                  pl.BlockSpec((1, MOE_EXPERTS * F, D), lambda n, e, tg_r, nv_r: (tg_r[n], 0, 0),
                               pipeline_mode=pl.Buffered(1)),
                  vec, vec],
        out_specs=pl.BlockSpec(memory_space=pl.ANY),
        scratch_shapes=[pltpu.VMEM((2, tm, D), F32), pltpu.VMEM((2, tm, LANE), F32),
                        pltpu.VMEM((tm, D), BF16), pltpu.VMEM((MOE_EXPERTS, tm, F), BF16),
                        pltpu.VMEM((tm, D), F32), pltpu.SMEM((1,), jnp.int32),
                        pltpu.SemaphoreType.DMA((2,)), pltpu.SemaphoreType.DMA((2,)),
                        pltpu.SemaphoreType.DMA((1,))],
    )
    return pl.pallas_call(
        _moe_kernel,
        grid_spec=grid_spec,
        out_shape=jax.ShapeDtypeStruct((T, D), F32),
        compiler_params=_cparams(("arbitrary", "arbitrary")),
        name="moe_experts_ln",
    )(tg, nvalid, idx, idx, x, comb, wg, wu, wd, g, b)


def _pad_cols(w, n):
    return jnp.pad(w, ((0, 0), (0, n - w.shape[1])))


def _layer(x, xb, l, p, rope_tabs):
    T = x.shape[0]
    W = RWKV_WIDTH
    KW, VW = GLA_K_WIDTH, GLA_V_WIDTH
    lambda_init = 0.8 - 0.6 * math.exp(-0.3 * l)

    w_in = p['w_in'][l]
    o_b = RWKV_IN
    o_c = o_b + DIFF_IN
    o_g = o_c + GLA_IN
    lo1 = 3 * W + RWKV_DECAY_LORA
    lo2 = lo1 + RWKV_AAA_LORA
    w_a = jnp.concatenate([_pad_cols(w_in[:, :lo1], 3 * W + LORA_PAD), _pad_cols(w_in[:, lo1:lo2], LORA_PAD),
                           w_in[:, lo2:o_b]], axis=1).astype(BF16)
    mu = p['rwkv_mu'][l]
    mu_a = jnp.concatenate([jnp.pad(mu[:lo1], (0, LORA_PAD - RWKV_DECAY_LORA)),
                            jnp.pad(mu[lo1:lo2], (0, LORA_PAD - RWKV_AAA_LORA)), mu[lo2:]])[None, :]
    qscale = DIFF_HEAD ** -0.5 * math.log2(math.e)
    w_qk = jnp.concatenate([w_in[:, o_b:o_b + DIFF_QK] * qscale, w_in[:, o_b + DIFF_QK:o_b + 2 * DIFF_QK]],
                           axis=1).astype(BF16)
    w_vt = w_in[:, o_b + 2 * DIFF_QK:o_c].T.astype(BF16)
    c0 = o_c
    w_c = jnp.concatenate([w_in[:, c0:c0 + KW] * (GLA_KEY ** -0.5), w_in[:, c0 + KW:c0 + 2 * KW + VW],
                           w_in[:, c0 + 2 * KW + VW + GLA_GATE_LORA:o_g],
                           _pad_cols(w_in[:, c0 + 2 * KW + VW:c0 + 2 * KW + VW + GLA_GATE_LORA], LANE)],
                          axis=1).astype(BF16)
    w_g = w_in[:, o_g:].astype(BF16)

    proj_a = _matmul(xb, w_a, F32, 1024, 1024)
    qk = _matmul_rope(xb, w_qk, rope_tabs, 1024, 1024)
    tk = min(512, T)
    vt = _matmul_vt(w_vt, xb, tk)
    proj_c = _matmul(xb, w_c, F32, 512, GLA_IN_PAD)
    gates = _matmul(xb, w_g, BF16, 1024, 1024, gate=True)

    padr = lambda w: jnp.pad(w, ((0, LORA_PAD - w.shape[0]), (0, 0)))
    r2 = lambda v: v.reshape(1, -1)
    o_a = _rwkv(proj_a, mu_a, padr(p['rwkv_w2'][l]).astype(BF16), r2(p['rwkv_w0'][l]),
                padr(p['rwkv_a2'][l]).astype(BF16), r2(p['rwkv_a0'][l]), p['rwkv_g2'][l].astype(BF16),
                r2(p['rwkv_kk_scale'][l]), r2(p['rwkv_ka_scale'][l]), r2(p['rwkv_rk'][l]),
                r2(p['rwkv_gn_g'][l]), r2(p['rwkv_gn_b'][l]), 256)
    o_bb = _diff_attention(qk, vt, r2(p['diff_lq1'][l]), r2(p['diff_lk1'][l]), r2(p['diff_lq2'][l]),
                           r2(p['diff_lk2'][l]), p['diff_subln_g'][l].reshape(-1, 1), lambda_init, min(2 * tk, T))
    g2p = jnp.pad(p['gla_g2'][l], ((0, LANE - GLA_GATE_LORA), (0, 0)))
    o_c = _gla(proj_c, g2p, r2(p['gla_gb'][l]), r2(p['gla_gn_g'][l]), 256)

    merged = _merge(o_a, o_bb, o_c, gates, p['proj_a'][l].astype(BF16), p['proj_b'][l].astype(BF16),
                    p['proj_c'][l].astype(BF16), 256)
    wr = _pad_cols(jnp.concatenate([p['router_group_w'][l], p['router_expert_w'][l]], axis=1), LANE)
    br = _pad_cols(jnp.concatenate([p['router_group_b'][l], p['router_expert_b'][l]])[None, :], LANE)
    x1, comb = _mm_ln(merged, p['w_o'][l].astype(BF16), x, r2(p['ln1_g'][l]), r2(p['ln1_b'][l]), wr, br, 512)
    F = MOE_HIDDEN
    wg = p['moe_w_gate'][l].reshape(N_EXPERTS, D_MODEL, F).astype(BF16)
    wu = p['moe_w_up'][l].reshape(N_EXPERTS, D_MODEL, F).astype(BF16)
    wd = p['moe_w_down'][l].reshape(MOE_GROUPS, MOE_EXPERTS * F, D_MODEL).astype(BF16)
    return _moe(x1, comb, wg, wu, wd, r2(p['ln2_g'][l]), r2(p['ln2_b'][l]), MOE_TILE)


def kernel(x, w_in, rwkv_mu, rwkv_w2, rwkv_w0, rwkv_a2, rwkv_a0, rwkv_g2, rwkv_kk_scale, rwkv_ka_scale, rwkv_rk, rwkv_gn_g, rwkv_gn_b, diff_lq1, diff_lk1, diff_lq2, diff_lk2, diff_subln_g, gla_g2, gla_gb, gla_gn_g, proj_a, proj_b, proj_c, w_o, ln1_g, ln1_b, router_group_w, router_group_b, router_expert_w, router_expert_b, moe_w_gate, moe_w_up, moe_w_down, ln2_g, ln2_b):
    p = dict(w_in=w_in, rwkv_mu=rwkv_mu, rwkv_w2=rwkv_w2, rwkv_w0=rwkv_w0, rwkv_a2=rwkv_a2, rwkv_a0=rwkv_a0,
             rwkv_g2=rwkv_g2, rwkv_kk_scale=rwkv_kk_scale, rwkv_ka_scale=rwkv_ka_scale, rwkv_rk=rwkv_rk,
             rwkv_gn_g=rwkv_gn_g, rwkv_gn_b=rwkv_gn_b, diff_lq1=diff_lq1, diff_lk1=diff_lk1, diff_lq2=diff_lq2,
             diff_lk2=diff_lk2, diff_subln_g=diff_subln_g, gla_g2=gla_g2, gla_gb=gla_gb, gla_gn_g=gla_gn_g,
             proj_a=proj_a, proj_b=proj_b, proj_c=proj_c, w_o=w_o, ln1_g=ln1_g, ln1_b=ln1_b,
             router_group_w=router_group_w, router_group_b=router_group_b, router_expert_w=router_expert_w,
             router_expert_b=router_expert_b, moe_w_gate=moe_w_gate, moe_w_up=moe_w_up, moe_w_down=moe_w_down,
             ln2_g=ln2_g, ln2_b=ln2_b)
    B, T, D = x.shape
    assert B == 1 and D == D_MODEL
    xf = x.reshape(T, D)
    xb = xf.astype(BF16)
    tabs = _rope_tables(T)
    for l in range(DEPTH):
        xf = _layer(xf, xb, l, p, tabs)
        xb = xf.astype(BF16)
    return xf.reshape(B, T, D)
```

```python
import functools
import math

import numpy as np
import jax
import jax.numpy as jnp
from jax import lax
from jax.experimental import pallas as pl
from jax.experimental.pallas import tpu as pltpu

F32 = jnp.float32
BF16 = jnp.bfloat16
HIGHEST = lax.Precision.HIGHEST

D_MODEL = 2048
DEPTH = 2
DEEPNORM_ALPHA = (2 * DEPTH) ** 0.25
LN_EPS = 1e-5

RWKV_HEADS = 8
RWKV_HEAD = 64
RWKV_WIDTH = RWKV_HEADS * RWKV_HEAD
RWKV_DECAY_LORA = 96
RWKV_AAA_LORA = 96
RWKV_GATE_LORA = 256
RWKV_GN_EPS = 64e-5
RWKV_IN = 3 * RWKV_WIDTH + RWKV_DECAY_LORA + RWKV_AAA_LORA + RWKV_GATE_LORA
RWKV_CHUNK = 64
LORA_PAD = 128
RWKV_IN_PAD = 3 * RWKV_WIDTH + 2 * LORA_PAD + RWKV_GATE_LORA

DIFF_HEADS = 8
DIFF_HEAD = 64
DIFF_QK = DIFF_HEADS * 2 * DIFF_HEAD
DIFF_V = DIFF_HEADS * 2 * DIFF_HEAD
DIFF_IN = 2 * DIFF_QK + DIFF_V
ROPE_THETA = 500000.0
ROPE_DIM = DIFF_HEAD // 4
ROPE_HALF = ROPE_DIM // 2

GLA_HEADS = 4
GLA_KEY = 64
GLA_VALUE = 128
GLA_K_WIDTH = GLA_HEADS * GLA_KEY
GLA_V_WIDTH = GLA_HEADS * GLA_VALUE
GLA_GATE_LORA = 16
GLA_TAU = 16.0
GLA_SUB = 16
GLA_IN = 2 * GLA_K_WIDTH + 2 * GLA_V_WIDTH + GLA_GATE_LORA
GLA_IN_PAD = 2 * GLA_K_WIDTH + 2 * GLA_V_WIDTH + 128

N_BRANCHES = 3
MOE_GROUPS = 4
MOE_EXPERTS = 8
MOE_HIDDEN = 256
N_EXPERTS = MOE_GROUPS * MOE_EXPERTS
MOE_TILE = 1024
LANE = 128
V7X_VMEM_LIMIT = 56 * 1024 * 1024


def _cparams(sem):
    return pltpu.CompilerParams(dimension_semantics=sem, vmem_limit_bytes=V7X_VMEM_LIMIT)


def _nt(a, b, **kw):
    return lax.dot_general(a, b, (((1,), (1,)), ((), ())), preferred_element_type=F32, **kw)


def _tn(a, b, **kw):
    return lax.dot_general(a, b, (((0,), (0,)), ((), ())), preferred_element_type=F32, **kw)


def _dot(a, b, **kw):
    return jnp.dot(a, b, preferred_element_type=F32, **kw)


def _mm_kernel(a_ref, b_ref, o_ref, *, gate):
    acc = _dot(a_ref[...], b_ref[...])
    if gate:
        acc = jax.nn.sigmoid(acc)
    o_ref[...] = acc.astype(o_ref.dtype)


def _matmul(a, b, out_dtype, tm, tn, gate=False):
    M, K = a.shape
    N = b.shape[1]
    tm, tn = min(tm, M), min(tn, N)
    return pl.pallas_call(
        functools.partial(_mm_kernel, gate=gate),
        grid=(N // tn, M // tm),
        in_specs=[pl.BlockSpec((tm, K), lambda j, i: (i, 0)),
                  pl.BlockSpec((K, tn), lambda j, i: (0, j))],
        out_specs=pl.BlockSpec((tm, tn), lambda j, i: (i, j)),
        out_shape=jax.ShapeDtypeStruct((M, N), out_dtype),
        compiler_params=_cparams(("arbitrary", "arbitrary")),
        name="proj_mm",
    )(a, b)


def _mm_rope_kernel(a_ref, b_ref, c_ref, sa_ref, sb_ref, o_ref):
    acc = _dot(a_ref[...], b_ref[...])
    c, sa, sb = c_ref[...], sa_ref[...], sb_ref[...]
    for h in range(acc.shape[1] // LANE):
        blk = acc[:, h * LANE:(h + 1) * LANE]
        out = blk * c + pltpu.roll(blk, LANE - ROPE_HALF, 1) * sa + pltpu.roll(blk, ROPE_HALF, 1) * sb
        o_ref[:, h * LANE:(h + 1) * LANE] = out.astype(o_ref.dtype)


def _matmul_rope(a, b, tabs, tm, tn):
    M, K = a.shape
    N = b.shape[1]
    tm, tn = min(tm, M), min(tn, N)
    tab_spec = pl.BlockSpec((tm, LANE), lambda j, i: (i, 0))
    return pl.pallas_call(
        _mm_rope_kernel,
        grid=(N // tn, M // tm),
        in_specs=[pl.BlockSpec((tm, K), lambda j, i: (i, 0)),
                  pl.BlockSpec((K, tn), lambda j, i: (0, j)),
                  tab_spec, tab_spec, tab_spec],
        out_specs=pl.BlockSpec((tm, tn), lambda j, i: (i, j)),
        out_shape=jax.ShapeDtypeStruct((M, N), BF16),
        compiler_params=_cparams(("arbitrary", "arbitrary")),
        name="proj_qk_rope",
    )(a, b, *tabs)


def _mm_vt_kernel(wt_ref, x_ref, o_ref):
    o_ref[0] = _nt(wt_ref[...], x_ref[...]).astype(o_ref.dtype)


def _matmul_vt(wt, x, tk):
    N, K = wt.shape
    T = x.shape[0]
    return pl.pallas_call(
        _mm_vt_kernel,
        grid=(T // tk,),
        in_specs=[pl.BlockSpec((N, K), lambda i: (0, 0)),
                  pl.BlockSpec((tk, K), lambda i: (i, 0))],
        out_specs=pl.BlockSpec((1, N, tk), lambda i: (i, 0, 0)),
        out_shape=jax.ShapeDtypeStruct((T // tk, N, tk), BF16),
        compiler_params=_cparams(("arbitrary",)),
        name="proj_vt",
    )(wt, x)


def _rope_tables(T):
    pos = jnp.arange(T, dtype=F32)[:, None]
    inv_freq = ROPE_THETA ** (-jnp.arange(ROPE_HALF, dtype=F32) / ROPE_HALF)
    lane = np.arange(LANE) % DIFF_HEAD
    ang = pos * inv_freq[None, :]
    cos8, sin8 = jnp.cos(ang), jnp.sin(ang)
    idx = jnp.asarray(lane % ROPE_HALF)
    cos_l, sin_l = cos8[:, idx], sin8[:, idx]
    first = jnp.asarray(lane < ROPE_HALF)[None, :]
    second = jnp.asarray((lane >= ROPE_HALF) & (lane < ROPE_DIM))[None, :]
    c = jnp.where(first | second, cos_l, 1.0)
    sa = jnp.where(first, -sin_l, 0.0)
    sb = jnp.where(second, sin_l, 0.0)
    return c, sa, sb


def _rwkv_kernel(a_ref, mu_ref, w2_ref, w0_ref, a2_ref, a0_ref, g2_ref, kks_ref, kas_ref, rk_ref,
                 gng_ref, gnb_ref, seg_ref, tri_ref, o_ref,
                 prev_ref, h_ref, at_ref, rt_ref, bt_ref, kt_ref, bp_ref, kp_ref, v_ref, gc_ref, y_ref):
    TB = a_ref.shape[0]
    C, N, W = RWKV_CHUNK, RWKV_HEAD, RWKV_WIDTH

    @pl.when(pl.program_id(0) == 0)
    def _():
        prev_ref[...] = jnp.zeros_like(prev_ref)
        h_ref[...] = jnp.zeros_like(h_ref)

    a = a_ref[...]
    row = lax.broadcasted_iota(jnp.int32, a.shape, 0)
    shifted = jnp.where(row == 0, prev_ref[...], pltpu.roll(a, 1, 0))
    prev_ref[...] = a[TB - 1:TB, :]
    xs = a + (shifted - a) * mu_ref[...]

    r = xs[:, 0:W]
    k = xs[:, W:2 * W]
    v = xs[:, 2 * W:3 * W]
    w_lo = xs[:, 3 * W:3 * W + LORA_PAD]
    a_lo = xs[:, 3 * W + LORA_PAD:3 * W + 2 * LORA_PAD]
    g_lo = xs[:, 3 * W + 2 * LORA_PAD:]

    z = w0_ref[...] + _dot(jnp.tanh(w_lo).astype(BF16), w2_ref[...])
    w = -jax.nn.softplus(-z) - 0.5
    lw = -jnp.exp(w)
    aa = jax.nn.sigmoid(a0_ref[...] + _dot(a_lo.astype(BF16), a2_ref[...]))
    g = _dot(jax.nn.sigmoid(g_lo).astype(BF16), g2_ref[...])

    seg = seg_ref[...]
    kk = k * kks_ref[...]
    ss = _dot((kk * kk).astype(BF16), seg)
    kk = kk * lax.rsqrt(jnp.maximum(ss, 1e-24))
    km = k * (1.0 + (aa - 1.0) * kas_ref[...])
    beta = kk * aa

    tri = tri_ref[...]
    cum = _dot(tri, lw, precision=HIGHEST)
    rowc = lax.broadcasted_iota(jnp.int32, (TB, TB), 0) // C
    colc = lax.broadcasted_iota(jnp.int32, (TB, TB), 1) // C
    allc = jnp.where(rowc == colc, 1.0, 0.0).astype(F32)
    cum_c = _dot(allc, lw, precision=HIGHEST)

    e_in = jnp.exp(cum)
    e_neg = jnp.exp(-cum)
    e_end = jnp.exp(cum_c - cum)
    at_ref[...] = (-kk * jnp.exp(cum - lw)).astype(BF16)
    rt_ref[...] = (r * e_in).astype(BF16)
    bt_ref[...] = (beta * e_neg).astype(BF16)
    kt_ref[...] = (km * e_neg).astype(BF16)
    bp_ref[...] = (beta * e_end).astype(BF16)
    kp_ref[...] = (km * e_end).astype(BF16)
    v_ref[...] = v.astype(BF16)
    gc_ref[...] = jnp.exp(cum_c)

    ri = lax.broadcasted_iota(jnp.int32, (C, C), 0)
    ci = lax.broadcasted_iota(jnp.int32, (C, C), 1)
    strict = ci < ri
    incl = ci <= ri
    eye = ci == ri

    def chunk(c, carry):
        NC = 2
        rows = [pl.ds(pl.multiple_of((c * NC + ci) * C, C), C) for ci in range(NC)]
        HS = range(NC * RWKV_HEADS)
        rw = [rows[i // RWKV_HEADS] for i in HS]
        sl = [slice((i % RWKV_HEADS) * N, (i % RWKV_HEADS + 1) * N) for i in HS]
        cat = lambda a, b: jnp.concatenate([a, b], axis=0)
        at = [at_ref[rw[i], sl[i]] for i in HS]
        rt = [rt_ref[rw[i], sl[i]] for i in HS]
        bt = [bt_ref[rw[i], sl[i]] for i in HS]
        kt = [kt_ref[rw[i], sl[i]] for i in HS]
        bp = [bp_ref[rw[i], sl[i]] for i in HS]
        kp = [kp_ref[rw[i], sl[i]] for i in HS]
        vh = [v_ref[rw[i], sl[i]] for i in HS]
        ar = [cat(at[h], rt[h]) for h in HS]
        xb = [_nt(ar[h], bt[h]) for h in HS]
        xk = [_nt(ar[h], kt[h]) for h in HS]
        a_ab = [jnp.where(strict, xb[h][:C], 0.0) for h in HS]
        a_rb = [jnp.where(incl, xb[h][C:], 0.0).astype(BF16) for h in HS]
        a_ak = [jnp.where(strict, xk[h][:C], 0.0).astype(BF16) for h in HS]
        a_rk = [jnp.where(incl, xk[h][C:], 0.0).astype(BF16) for h in HS]
        ident = jnp.where(eye, 1.0, 0.0)
        t_inv = [ident + a_ab[h] for h in HS]
        ap = a_ab
        for _ in range(5):
            apb = [ap[h].astype(BF16) for h in HS]
            ap = [_dot(apb[h], apb[h]) for h in HS]
            t_inv = [t_inv[h] + _dot(t_inv[h].astype(BF16), ap[h].astype(BF16)) for h in HS]
        tb = [t_inv[h].astype(BF16) for h in HS]
        wv = [_dot(a_ak[h], vh[h]).astype(BF16) for h in HS]
        pb = [_dot(tb[h], at[h]).astype(BF16) for h in HS]
        u0b = [_dot(tb[h], wv[h]).astype(BF16) for h in HS]
        qm = [rt[h].astype(F32) + _dot(a_rb[h], pb[h]) for h in HS]
        y0 = [_dot(a_rb[h], u0b[h]) + _dot(a_rk[h], vh[h]) for h in HS]
        gm = [jnp.where(eye, gc_ref[rw[h], sl[h]][0:1, :], 0.0) + _tn(bp[h], pb[h]) for h in HS]
        hadd = [_tn(cat(bp[h], kp[h]), cat(u0b[h], vh[h])) for h in HS]
        qg = [cat(qm[h], gm[h]).astype(BF16) for h in HS]
        for i in HS:
            hd = i % RWKV_HEADS
            z = _dot(qg[i], h_ref[hd].astype(BF16))
            y_ref[rw[i], sl[i]] = z[:C] + y0[i]
            h_ref[hd] = z[C:] + hadd[i]
        return carry

    lax.fori_loop(0, TB // (2 * C), chunk, 0)

    y = y_ref[...]
    segm = seg_ref[...]
    mu = _dot(y.astype(BF16), segm) * (1.0 / N)
    yc = y - mu
    var = _dot((yc * yc).astype(BF16), segm) * (1.0 / N)
    yn = yc * lax.rsqrt(var + RWKV_GN_EPS) * gng_ref[...] + gnb_ref[...]
    bonus = _dot((r * km * rk_ref[...]).astype(BF16), segm) * v
    o_ref[...] = ((yn + bonus) * g).astype(o_ref.dtype)


def _rwkv(a_proj, mu, w2, w0, a2, a0, g2, kks, kas, rk, gng, gnb, tb):
    T = a_proj.shape[0]
    tb = min(tb, T)
    W = RWKV_WIDTH
    hid = np.arange(W) // RWKV_HEAD
    seg = jnp.asarray(hid[:, None] == hid[None, :], dtype=BF16)
    t_idx = np.arange(tb)
    tri = jnp.asarray((t_idx[:, None] // RWKV_CHUNK == t_idx[None, :] // RWKV_CHUNK)
                      & (t_idx[None, :] <= t_idx[:, None]), dtype=F32)
    full = lambda shp: pl.BlockSpec(shp, lambda i: (0,) * len(shp))
    row = lambda n: full((1, n))
    bf = lambda: pltpu.VMEM((tb, W), BF16)
    return pl.pallas_call(
        _rwkv_kernel,
        grid=(T // tb,),
        in_specs=[pl.BlockSpec((tb, RWKV_IN_PAD), lambda i: (i, 0)),
                  row(RWKV_IN_PAD), full((LORA_PAD, W)), row(W), full((LORA_PAD, W)), row(W),
                  full((RWKV_GATE_LORA, W)), row(W), row(W), row(W), row(W), row(W),
                  full((W, W)), full((tb, tb))],
        out_specs=pl.BlockSpec((tb, W), lambda i: (i, 0)),
        out_shape=jax.ShapeDtypeStruct((T, W), BF16),
        scratch_shapes=[pltpu.VMEM((1, RWKV_IN_PAD), F32),
                        pltpu.VMEM((RWKV_HEADS, RWKV_HEAD, RWKV_HEAD), F32),
                        bf(), bf(), bf(), bf(), bf(), bf(), bf(),
                        pltpu.VMEM((tb, W), F32), pltpu.VMEM((tb, W), F32)],
        compiler_params=_cparams(("arbitrary",)),
        name="rwkv7",
    )(a_proj, mu, w2, w0, a2, a0, g2, kks, kas, rk, gng, gnb, seg, tri)


def _diff_kernel(q_ref, k_ref, vt_ref, lq1_ref, lk1_ref, lq2_ref, lk2_ref, sg_ref, o_ref,
                 m_ref, l_ref, acc_ref, s_ref, p_ref, a_ref, *, lambda_init):
    tq = q_ref.shape[0]
    nkb, _, tk = vt_ref.shape
    i = pl.program_id(1)
    q = q_ref[...]
    lane = lax.broadcasted_iota(jnp.int32, q.shape, 1)
    qm = (jnp.where(lane < DIFF_HEAD, q, jnp.zeros_like(q)),
          jnp.where(lane >= DIFF_HEAD, q, jnp.zeros_like(q)))
    m_ref[...] = jnp.full(m_ref.shape, -jnp.inf, F32)
    l_ref[...] = jnp.zeros_like(l_ref)
    acc_ref[...] = jnp.zeros_like(acc_ref)
    p_ref[1] = jnp.zeros(p_ref.shape[1:], BF16)
    a_ref[1] = jnp.ones(a_ref.shape[1:], F32)

    def scores(j, slot):
        jc = jnp.minimum(j, nkb - 1)
        kj = k_ref[pl.ds(pl.multiple_of(jc * tk, tk), tk), :]
        for mi in range(2):
            s_ref[slot, mi] = _nt(kj, qm[mi])

    def values(j, slot):
        vj = vt_ref[jnp.clip(j, 0, nkb - 1)]
        for mi in range(2):
            acc_ref[mi] = a_ref[slot, mi] * acc_ref[mi] + _dot(vj, p_ref[slot, mi])

    def softmax(j, slot, masked):
        for mi in range(2):
            s = s_ref[slot, mi]
            if masked:
                d = (lax.broadcasted_iota(jnp.int32, (tk, tq), 0)
                     - lax.broadcasted_iota(jnp.int32, (tk, tq), 1))
                s = jnp.where(d <= i * tq - j * tk, s, -jnp.inf)
            m_old = m_ref[mi]
            m_new = jnp.maximum(m_old, jnp.max(s, axis=0, keepdims=True))
            alpha = jnp.exp2(m_old - m_new)
            p = jnp.exp2(s - m_new)
            l_ref[mi] = alpha * l_ref[mi] + jnp.sum(p, axis=0, keepdims=True)
            m_ref[mi] = m_new
            a_ref[slot, mi] = alpha
            p_ref[slot, mi] = p.astype(BF16)

    def half(j, cur, masked):
        nxt = 1 - cur
        scores(j + 1, nxt)
        values(j - 1, nxt)
        softmax(j, cur, masked)

    def pair(jj, carry):
        half(2 * jj, 0, False)
        half(2 * jj + 1, 1, False)
        return carry

    scores(0, 0)
    n_pairs = ((i + 1) * (tq // tk) + 1) // 2
    lax.fori_loop(0, n_pairs - 1, pair, 0)
    j_last = 2 * (n_pairs - 1)
    half(j_last, 0, True)
    half(j_last + 1, 1, True)
    values(j_last + 1, 1)

    lam = (jnp.exp(jnp.sum(lq1_ref[...] * lk1_ref[...])) - jnp.exp(jnp.sum(lq2_ref[...] * lk2_ref[...]))
           + lambda_init)
    o = acc_ref[0] / l_ref[0] - lam * (acc_ref[1] / l_ref[1])
    o = o * lax.rsqrt(jnp.mean(o * o, axis=0, keepdims=True) + LN_EPS) * (1.0 - lambda_init)
    o = o * sg_ref[...]
    o_ref[...] = o.T.astype(o_ref.dtype)


def _diff_attention(qk, vt, lq1, lk1, lq2, lk2, subln_g, lambda_init, tq):
    T = qk.shape[0]
    tk = vt.shape[2]
    assert tq % tk == 0 and T % tq == 0
    H, E = DIFF_HEADS, 2 * DIFF_HEAD
    vec = lambda: pl.BlockSpec((1, DIFF_HEAD), lambda h, i: (0, 0))
    return pl.pallas_call(
        functools.partial(_diff_kernel, lambda_init=lambda_init),
        grid=(H, T // tq),
        in_specs=[pl.BlockSpec((tq, E), lambda h, i: (i, h)),
                  pl.BlockSpec((T, E), lambda h, i: (0, H + h)),
                  pl.BlockSpec((T // tk, E, tk), lambda h, i: (0, h, 0)),
                  vec(), vec(), vec(), vec(),
                  pl.BlockSpec((E, 1), lambda h, i: (0, 0))],
        out_specs=pl.BlockSpec((tq, E), lambda h, i: (i, h)),
        out_shape=jax.ShapeDtypeStruct((T, DIFF_V), BF16),
        scratch_shapes=[pltpu.VMEM((2, 1, tq), F32), pltpu.VMEM((2, 1, tq), F32),
                        pltpu.VMEM((2, E, tq), F32), pltpu.VMEM((2, 2, tk, tq), F32),
                        pltpu.VMEM((2, 2, tk, tq), BF16), pltpu.VMEM((2, 2, 1, tq), F32)],
        compiler_params=_cparams(("arbitrary", "arbitrary")),
        name="diff_attn",
    )(qk, qk, vt, lq1, lk1, lq2, lk2, subln_g)


def _gla_kernel(c_ref, g2_ref, gb_ref, gng_ref, tri_ref, segv_ref, o_ref,
                st_ref, oacc_ref):
    TB = c_ref.shape[0]
    KW, VW, S = GLA_K_WIDTH, GLA_V_WIDTH, GLA_SUB
    DK, DV = GLA_KEY, GLA_VALUE

    @pl.when(pl.program_id(0) == 0)
    def _():
        st_ref[...] = jnp.zeros_like(st_ref)

    g_lo = c_ref[:, 2 * KW + 2 * VW:]
    gate = _dot(g_lo, g2_ref[...], precision=HIGHEST) + gb_ref[...]
    log_a = jax.nn.log_sigmoid(gate) * (1.0 / GLA_TAU)
    b = _dot(tri_ref[...], log_a, precision=HIGHEST)
    NS = TB // S
    b3 = b.reshape(NS, S, KW)
    q3 = c_ref[:, 0:KW].reshape(NS, S, KW)
    k3 = c_ref[:, KW:2 * KW].reshape(NS, S, KW)
    v = c_ref[:, 2 * KW:2 * KW + VW]
    v3 = v.reshape(NS, S, VW)
    bl3 = b3[:, S - 1:S, :]
    qb = (q3 * jnp.exp(b3)).astype(BF16).reshape(TB, KW)
    kb = (k3 * jnp.exp(bl3 - b3)).astype(BF16).reshape(TB, KW)
    dec = jnp.exp(bl3)
    vb = v.astype(BF16)

    ri = lax.broadcasted_iota(jnp.int32, (NS, S, 1), 1)
    segv = segv_ref[...]
    acc = jnp.zeros((NS, S, VW), F32)
    for j in range(S):
        e = jnp.exp(jnp.minimum(b3 - b3[:, j:j + 1, :], 0.0))
        t = (q3 * e * k3[:, j:j + 1, :]).astype(BF16).reshape(TB, KW)
        att = _dot(t, segv).reshape(NS, S, VW)
        acc = acc + jnp.where(ri >= j, att, 0.0) * v3[:, j:j + 1, :]
    oacc_ref[...] = acc.reshape(TB, VW)

    for h in range(GLA_HEADS):
        lk = slice(h * DK, (h + 1) * DK)
        lv = slice(h * DV, (h + 1) * DV)
        kv = [_tn(vb[s * S:(s + 1) * S, lv], kb[s * S:(s + 1) * S, lk]) for s in range(NS)]
        st = st_ref[h]
        for s in range(NS):
            rows = slice(s * S, (s + 1) * S)
            oacc_ref[rows, lv] += _nt(qb[rows, lk], st.astype(BF16))
            st = st * dec[s][:, lk] + kv[s]
        st_ref[h] = st

    o = oacc_ref[...]
    rgate = c_ref[:, 2 * KW + VW:2 * KW + 2 * VW]
    for h in range(GLA_HEADS):
        lv = slice(h * DV, (h + 1) * DV)
        oh = o[:, lv]
        oh = oh * lax.rsqrt(jnp.mean(oh * oh, axis=1, keepdims=True) + LN_EPS) * gng_ref[:, lv]
        o_ref[:, lv] = (oh * jax.nn.silu(rgate[:, lv])).astype(o_ref.dtype)


def _gla(c_proj, g2, gb, gng, tb):
    T = c_proj.shape[0]
    tb = min(tb, T)
    KW, VW = GLA_K_WIDTH, GLA_V_WIDTH
    t_idx = np.arange(tb)
    tri = jnp.asarray((t_idx[:, None] // GLA_SUB == t_idx[None, :] // GLA_SUB)
                      & (t_idx[None, :] <= t_idx[:, None]), dtype=F32)
    segv = jnp.asarray((np.arange(KW) // GLA_KEY)[:, None] == (np.arange(VW) // GLA_VALUE)[None, :], dtype=BF16)
    full = lambda shp: pl.BlockSpec(shp, lambda i: (0,) * len(shp))
    return pl.pallas_call(
        _gla_kernel,
        grid=(T // tb,),
        in_specs=[pl.BlockSpec((tb, GLA_IN_PAD), lambda i: (i, 0)),
                  full((LANE, KW)), full((1, KW)), full((1, VW)), full((tb, tb)), full((KW, VW))],
        out_specs=pl.BlockSpec((tb, VW), lambda i: (i, 0)),
        out_shape=jax.ShapeDtypeStruct((T, VW), BF16),
        scratch_shapes=[pltpu.VMEM((GLA_HEADS, GLA_VALUE, GLA_KEY), F32),
                        pltpu.VMEM((tb, VW), F32)],
        compiler_params=_cparams(("arbitrary",)),
        name="gla",
    )(c_proj, g2, gb, gng, tri, segv)


def _merge_kernel(oa_ref, ob_ref, oc_ref, g_ref, pa_ref, pb_ref, pc_ref, o_ref):
    D = D_MODEL
    m = g_ref[:, 0:D].astype(F32) * _dot(oa_ref[...], pa_ref[...])
    m = m + g_ref[:, D:2 * D].astype(F32) * _dot(ob_ref[...], pb_ref[...])
    m = m + g_ref[:, 2 * D:3 * D].astype(F32) * _dot(oc_ref[...], pc_ref[...])
    o_ref[...] = m.astype(o_ref.dtype)


def _merge(oa, ob, oc, gates, pa, pb, pc, tm):
    T = oa.shape[0]
    tm = min(tm, T)
    D = D_MODEL
    rowblk = lambda n: pl.BlockSpec((tm, n), lambda i: (i, 0))
    full = lambda shp: pl.BlockSpec(shp, lambda i: (0, 0))
    return pl.pallas_call(
        _merge_kernel,
        grid=(T // tm,),
        in_specs=[rowblk(RWKV_WIDTH), rowblk(DIFF_V), rowblk(GLA_V_WIDTH), rowblk(3 * D),
                  full((RWKV_WIDTH, D)), full((DIFF_V, D)), full((GLA_V_WIDTH, D))],
        out_specs=rowblk(D),
        out_shape=jax.ShapeDtypeStruct((T, D), BF16),
        compiler_params=_cparams(("arbitrary",)),
        name="merge",
    )(oa, ob, oc, gates, pa, pb, pc)


def _layer_norm(z, g, b):
    mu = jnp.mean(z, axis=1, keepdims=True)
    zc = z - mu
    var = jnp.mean(zc * zc, axis=1, keepdims=True)
    return zc * lax.rsqrt(var + LN_EPS) * g + b


def _mm_ln_kernel(a_ref, w_ref, res_ref, g_ref, b_ref, wrh_ref, wrl_ref, br_ref, o_ref):
    z = DEEPNORM_ALPHA * res_ref[...] + _dot(a_ref[...], w_ref[...])
    out = _layer_norm(z, g_ref[...], b_ref[...])
    o_ref[:, 0:D_MODEL] = out
    o_ref[:, D_MODEL:] = _route(out, wrh_ref[...], wrl_ref[...], br_ref[...])


def _mm_ln(a, w, res, g, b, wr, br, tm):
    wr_hi = wr.astype(BF16)
    wr_lo = (wr - wr_hi.astype(F32)).astype(BF16)
    T, K = a.shape
    D = D_MODEL
    tm = min(tm, T)
    rowblk = lambda n: pl.BlockSpec((tm, n), lambda i: (i, 0))
    full = lambda shp: pl.BlockSpec(shp, lambda i: (0, 0))
    return pl.pallas_call(
        _mm_ln_kernel,
        grid=(T // tm,),
        in_specs=[rowblk(K), full((K, D)), rowblk(D), full((1, D)), full((1, D)), full((D, LANE)), full((D, LANE)),
                  full((1, LANE))],
        out_specs=rowblk(D + LANE),
        out_shape=jax.ShapeDtypeStruct((T, D + LANE), F32),
        compiler_params=_cparams(("arbitrary",)),
        name="wo_ln_route",
    )(a, w, res, g, b, wr_hi, wr_lo, br)


def _route(x, w_hi, w_lo, b):
    G, E = MOE_GROUPS, MOE_EXPERTS
    x_hi = x.astype(BF16)
    x_lo = (x - x_hi.astype(F32)).astype(BF16)
    logits = _dot(x_hi, w_hi) + (_dot(x_lo, w_hi) + _dot(x_hi, w_lo)) + b
    lane = lax.broadcasted_iota(jnp.int32, logits.shape, 1)
    neg = -jnp.inf
    big = jnp.int32(1 << 20)
    is_g = lane < G
    lg = jnp.where(is_g, logits, neg)
    gmax = jnp.max(lg, axis=1, keepdims=True)
    gidx = jnp.min(jnp.where(is_g & (lg == gmax), lane, big), axis=1, keepdims=True)
    g_p = 1.0 / jnp.sum(jnp.exp(lg - gmax), axis=1, keepdims=True)
    lo = G + gidx * E
    in_grp = (lane >= lo) & (lane < lo + E)
    le = jnp.where(in_grp, logits, neg)
    m1 = jnp.max(le, axis=1, keepdims=True)
    i1 = jnp.min(jnp.where(in_grp & (le == m1), lane, big), axis=1, keepdims=True)
    le2 = jnp.where(lane == i1, neg, le)
    m2 = jnp.max(le2, axis=1, keepdims=True)
    i2 = jnp.min(jnp.where(in_grp & (le2 == m2), lane, big), axis=1, keepdims=True)
    e2 = jnp.exp(m2 - m1)
    p1 = 1.0 / (1.0 + e2)
    p2 = e2 / (1.0 + e2)
    comb = jnp.where(lane == i1, g_p * p1, 0.0) + jnp.where(lane == i2, g_p * p2, 0.0)
    return jnp.where(lane == 0, gidx.astype(F32), comb)


def _moe_tables(gid_f, tm):
    T = gid_f.shape[0]
    G = MOE_GROUPS
    nt = T // tm + G
    gid = gid_f.astype(jnp.int32)
    order = jnp.argsort(gid).astype(jnp.int32)
    counts = jnp.sum((gid[:, None] == jnp.arange(G)[None, :]).astype(jnp.int32), axis=0)
    starts = jnp.cumsum(counts) - counts
    ntile = (counts + tm - 1) // tm
    tstart = jnp.cumsum(ntile) - ntile
    n = jnp.arange(nt, dtype=jnp.int32)
    tg = jnp.sum((n[:, None] >= tstart[None, 1:]).astype(jnp.int32), axis=1)
    k = n - tstart[tg]
    row0 = starts[tg] + k * tm
    nvalid = jnp.clip(counts[tg] - k * tm, 0, tm).astype(jnp.int32)
    rows = jnp.clip(row0[:, None] + jnp.arange(tm, dtype=jnp.int32)[None, :], 0, T - 1)
    idx = jnp.take(order, rows).reshape(nt, 1, tm)
    return tg.astype(jnp.int32), nvalid, idx


def _moe_kernel(tg_ref, nv_ref, idx_ref, idxn_ref, x_hbm, wg_ref, wu_ref, wd_ref, g_ref, b_ref, y_hbm,
                xs_ref, xb_ref, h_ref, out_ref, pend_ref, gx_sem, sc_sem):
    n = pl.program_id(0)
    e = pl.program_id(1)
    nt = pl.num_programs(0)
    nv = nv_ref[n]
    slot = n % 2
    other = 1 - slot
    tm = out_ref.shape[0]
    part = tm // MOE_EXPERTS
    F = MOE_HIDDEN

    def start_row(idx, dst_slot, r, prio):
        tok = idx[0, 0, r]
        pltpu.make_async_copy(x_hbm.at[pl.ds(tok, 1)], xs_ref.at[dst_slot, pl.ds(r, 1)],
                              gx_sem.at[dst_slot]).start(priority=prio)

    def start_next_part():
        for r in range(part):
            start_row(idxn_ref, other, e * part + r, r % 2)

    def wait_rows(s):
        pltpu.make_async_copy(x_hbm.at[pl.ds(0, tm)], xs_ref.at[s], gx_sem.at[s]).wait()

    def drain():
        @pl.when(pend_ref[0] == tm)
        def _():
            pltpu.make_async_copy(out_ref, y_hbm.at[pl.ds(0, tm)], sc_sem.at[0]).wait()

        @pl.when(pend_ref[0] < tm)
        def _():
            def body(r, c):
                pltpu.make_async_copy(out_ref.at[pl.ds(0, 1)], y_hbm.at[pl.ds(0, 1)], sc_sem.at[0]).wait()
                return c
            lax.fori_loop(0, pend_ref[0], body, 0)
        pend_ref[0] = 0

    @pl.when((n == 0) & (e == 0))
    def _():
        pend_ref[0] = 0

        def body(r, c):
            start_row(idx_ref, 0, 2 * r, 0)
            start_row(idx_ref, 0, 2 * r + 1, 1)
            return c
        lax.fori_loop(0, tm // 2, body, 0)

    @pl.when(e == 0)
    def _():
        wait_rows(slot)
        xb_ref[...] = xs_ref[slot, :, 0:D_MODEL].astype(BF16)

    @pl.when(nv > 0)
    def _():
        start_next_part()
        xb = xb_ref[...]
        hg = _dot(xb, wg_ref[0])
        hu = _dot(xb, wu_ref[0])
        comb = xs_ref[slot, :, D_MODEL:]
        lane = lax.broadcasted_iota(jnp.int32, comb.shape, 1)
        col = MOE_GROUPS + tg_ref[n] * MOE_EXPERTS + e
        cw = jnp.sum(jnp.where(lane == col, comb, 0.0), axis=1, keepdims=True)
        h_ref[e] = (jax.nn.silu(hg) * hu * cw).astype(BF16)

    @pl.when(nv == 0)
    def _():
        start_next_part()

    @pl.when(e == MOE_EXPERTS - 1)
    def _():
        drain()

        @pl.when(nv > 0)
        def _():
            y = _dot(h_ref[0], wd_ref[0, 0:F, :])
            for j in range(1, MOE_EXPERTS):
                y = y + _dot(h_ref[j], wd_ref[0, j * F:(j + 1) * F, :])
            z = DEEPNORM_ALPHA * xs_ref[slot, :, 0:D_MODEL] + y
            out_ref[...] = _layer_norm(z, g_ref[...], b_ref[...])

        def put(r, prio):
            tok = idx_ref[0, 0, r]
            pltpu.make_async_copy(out_ref.at[pl.ds(r, 1)], y_hbm.at[pl.ds(tok, 1)],
                                  sc_sem.at[0]).start(priority=prio)

        def body(r, c):
            put(2 * r, 0)
            put(2 * r + 1, 1)
            return c
        lax.fori_loop(0, nv // 2, body, 0)

        @pl.when(nv % 2 == 1)
        def _():
            put(nv - 1, 0)
        pend_ref[0] = nv

        @pl.when(n == nt - 1)
        def _():
            drain()
            wait_rows(other)


def _moe(xa, wg, wu, wd, g, b, tm):
    T = xa.shape[0]
    D = D_MODEL
    F = MOE_HIDDEN
    tm = min(tm, T)
    tg, nvalid, idx = _moe_tables(xa[:, D], tm)
    nt = tg.shape[0]
    wspec = lambda shp: pl.BlockSpec(shp, lambda n, e, tg_r, nv_r: (tg_r[n] * MOE_EXPERTS + e, 0, 0))
    ispec = lambda f: pl.BlockSpec((1, 1, tm), f, memory_space=pltpu.SMEM)
    vec = pl.BlockSpec((1, D), lambda n, e, tg_r, nv_r: (0, 0))
    grid_spec = pltpu.PrefetchScalarGridSpec(
        num_scalar_prefetch=2,
        grid=(nt, MOE_EXPERTS),
        in_specs=[ispec(lambda n, e, tg_r, nv_r: (n, 0, 0)),
                  ispec(lambda n, e, tg_r, nv_r: (jnp.minimum(n + 1, nt - 1), 0, 0)),
                  pl.BlockSpec(memory_space=pl.ANY),
                  wspec((1, D, F)), wspec((1, D, F)),
                  pl.BlockSpec((1, MOE_EXPERTS * F, D), lambda n, e, tg_r, nv_r: (tg_r[n], 0, 0),
                               pipeline_mode=pl.Buffered(1)),
                  vec, vec],
        out_specs=pl.BlockSpec(memory_space=pl.ANY),
        scratch_shapes=[pltpu.VMEM((2, tm, D + LANE), F32),
                        pltpu.VMEM((tm, D), BF16), pltpu.VMEM((MOE_EXPERTS, tm, F), BF16),
                        pltpu.VMEM((tm, D), F32), pltpu.SMEM((1,), jnp.int32),
                        pltpu.SemaphoreType.DMA((2,)), pltpu.SemaphoreType.DMA((1,))],
    )
    return pl.pallas_call(
        _moe_kernel,
        grid_spec=grid_spec,
        out_shape=jax.ShapeDtypeStruct((T, D), F32),
        compiler_params=_cparams(("arbitrary", "arbitrary")),
        name="moe_experts_ln",
    )(tg, nvalid, idx, idx, xa, wg, wu, wd, g, b)


def _pad_cols(w, n):
    return jnp.pad(w, ((0, 0), (0, n - w.shape[1])))


def _layer(x, xb, l, p, rope_tabs):
    T = x.shape[0]
    W = RWKV_WIDTH
    KW, VW = GLA_K_WIDTH, GLA_V_WIDTH
    lambda_init = 0.8 - 0.6 * math.exp(-0.3 * l)

    w_in = p['w_in'][l]
    o_b = RWKV_IN
    o_c = o_b + DIFF_IN
    o_g = o_c + GLA_IN
    lo1 = 3 * W + RWKV_DECAY_LORA
    lo2 = lo1 + RWKV_AAA_LORA
    w_a = jnp.concatenate([_pad_cols(w_in[:, :lo1], 3 * W + LORA_PAD), _pad_cols(w_in[:, lo1:lo2], LORA_PAD),
                           w_in[:, lo2:o_b]], axis=1).astype(BF16)
    mu = p['rwkv_mu'][l]
    mu_a = jnp.concatenate([jnp.pad(mu[:lo1], (0, LORA_PAD - RWKV_DECAY_LORA)),
                            jnp.pad(mu[lo1:lo2], (0, LORA_PAD - RWKV_AAA_LORA)), mu[lo2:]])[None, :]
    qscale = DIFF_HEAD ** -0.5 * math.log2(math.e)
    w_qk = jnp.concatenate([w_in[:, o_b:o_b + DIFF_QK] * qscale, w_in[:, o_b + DIFF_QK:o_b + 2 * DIFF_QK]],
                           axis=1).astype(BF16)
    w_vt = w_in[:, o_b + 2 * DIFF_QK:o_c].T.astype(BF16)
    c0 = o_c
    w_c = jnp.concatenate([w_in[:, c0:c0 + KW] * (GLA_KEY ** -0.5), w_in[:, c0 + KW:c0 + 2 * KW + VW],
                           w_in[:, c0 + 2 * KW + VW + GLA_GATE_LORA:o_g],
                           _pad_cols(w_in[:, c0 + 2 * KW + VW:c0 + 2 * KW + VW + GLA_GATE_LORA], LANE)],
                          axis=1).astype(BF16)
    w_g = w_in[:, o_g:].astype(BF16)

    proj_a = _matmul(xb, w_a, F32, 1024, 1024)
    qk = _matmul_rope(xb, w_qk, rope_tabs, 1024, 1024)
    tk = min(512, T)
    vt = _matmul_vt(w_vt, xb, tk)
    proj_c = _matmul(xb, w_c, F32, 512, GLA_IN_PAD)
    gates = _matmul(xb, w_g, BF16, 1024, 1024, gate=True)

    padr = lambda w: jnp.pad(w, ((0, LORA_PAD - w.shape[0]), (0, 0)))
    r2 = lambda v: v.reshape(1, -1)
    o_a = _rwkv(proj_a, mu_a, padr(p['rwkv_w2'][l]).astype(BF16), r2(p['rwkv_w0'][l]),
                padr(p['rwkv_a2'][l]).astype(BF16), r2(p['rwkv_a0'][l]), p['rwkv_g2'][l].astype(BF16),
                r2(p['rwkv_kk_scale'][l]), r2(p['rwkv_ka_scale'][l]), r2(p['rwkv_rk'][l]),
                r2(p['rwkv_gn_g'][l]), r2(p['rwkv_gn_b'][l]), 256)
    o_bb = _diff_attention(qk, vt, r2(p['diff_lq1'][l]), r2(p['diff_lk1'][l]), r2(p['diff_lq2'][l]),
                           r2(p['diff_lk2'][l]), p['diff_subln_g'][l].reshape(-1, 1), lambda_init, min(2 * tk, T))
    g2p = jnp.pad(p['gla_g2'][l], ((0, LANE - GLA_GATE_LORA), (0, 0)))
    o_c = _gla(proj_c, g2p, r2(p['gla_gb'][l]), r2(p['gla_gn_g'][l]), 256)

    merged = _merge(o_a, o_bb, o_c, gates, p['proj_a'][l].astype(BF16), p['proj_b'][l].astype(BF16),
                    p['proj_c'][l].astype(BF16), 256)
    wr = _pad_cols(jnp.concatenate([p['router_group_w'][l], p['router_expert_w'][l]], axis=1), LANE)
    br = _pad_cols(jnp.concatenate([p['router_group_b'][l], p['router_expert_b'][l]])[None, :], LANE)
    xa = _mm_ln(merged, p['w_o'][l].astype(BF16), x, r2(p['ln1_g'][l]), r2(p['ln1_b'][l]), wr, br, 512)
    F = MOE_HIDDEN
    wg = p['moe_w_gate'][l].reshape(N_EXPERTS, D_MODEL, F).astype(BF16)
    wu = p['moe_w_up'][l].reshape(N_EXPERTS, D_MODEL, F).astype(BF16)
    wd = p['moe_w_down'][l].reshape(MOE_GROUPS, MOE_EXPERTS * F, D_MODEL).astype(BF16)
    return _moe(xa, wg, wu, wd, r2(p['ln2_g'][l]), r2(p['ln2_b'][l]), MOE_TILE)


def kernel(x, w_in, rwkv_mu, rwkv_w2, rwkv_w0, rwkv_a2, rwkv_a0, rwkv_g2, rwkv_kk_scale, rwkv_ka_scale, rwkv_rk, rwkv_gn_g, rwkv_gn_b, diff_lq1, diff_lk1, diff_lq2, diff_lk2, diff_subln_g, gla_g2, gla_gb, gla_gn_g, proj_a, proj_b, proj_c, w_o, ln1_g, ln1_b, router_group_w, router_group_b, router_expert_w, router_expert_b, moe_w_gate, moe_w_up, moe_w_down, ln2_g, ln2_b):
    p = dict(w_in=w_in, rwkv_mu=rwkv_mu, rwkv_w2=rwkv_w2, rwkv_w0=rwkv_w0, rwkv_a2=rwkv_a2, rwkv_a0=rwkv_a0,
             rwkv_g2=rwkv_g2, rwkv_kk_scale=rwkv_kk_scale, rwkv_ka_scale=rwkv_ka_scale, rwkv_rk=rwkv_rk,
             rwkv_gn_g=rwkv_gn_g, rwkv_gn_b=rwkv_gn_b, diff_lq1=diff_lq1, diff_lk1=diff_lk1, diff_lq2=diff_lq2,
             diff_lk2=diff_lk2, diff_subln_g=diff_subln_g, gla_g2=gla_g2, gla_gb=gla_gb, gla_gn_g=gla_gn_g,
             proj_a=proj_a, proj_b=proj_b, proj_c=proj_c, w_o=w_o, ln1_g=ln1_g, ln1_b=ln1_b,
             router_group_w=router_group_w, router_group_b=router_group_b, router_expert_w=router_expert_w,
             router_expert_b=router_expert_b, moe_w_gate=moe_w_gate, moe_w_up=moe_w_up, moe_w_down=moe_w_down,
             ln2_g=ln2_g, ln2_b=ln2_b)
    B, T, D = x.shape
    assert B == 1 and D == D_MODEL
    xf = x.reshape(T, D)
    xb = xf.astype(BF16)
    tabs = _rope_tables(T)
    for l in range(DEPTH):
        xf = _layer(xf, xb, l, p, tabs)
        xb = xf.astype(BF16)
    return xf.reshape(B, T, D)
```

```python
import functools
import math

import numpy as np
import jax
import jax.numpy as jnp
from jax import lax
from jax.experimental import pallas as pl
from jax.experimental.pallas import tpu as pltpu

F32 = jnp.float32
BF16 = jnp.bfloat16
HIGHEST = lax.Precision.HIGHEST

D_MODEL = 2048
DEPTH = 2
DEEPNORM_ALPHA = (2 * DEPTH) ** 0.25
LN_EPS = 1e-5

RWKV_HEADS = 8
RWKV_HEAD = 64
RWKV_WIDTH = RWKV_HEADS * RWKV_HEAD
RWKV_DECAY_LORA = 96
RWKV_AAA_LORA = 96
RWKV_GATE_LORA = 256
RWKV_GN_EPS = 64e-5
RWKV_IN = 3 * RWKV_WIDTH + RWKV_DECAY_LORA + RWKV_AAA_LORA + RWKV_GATE_LORA
RWKV_CHUNK = 64
LORA_PAD = 128
RWKV_IN_PAD = 3 * RWKV_WIDTH + 2 * LORA_PAD + RWKV_GATE_LORA

DIFF_HEADS = 8
DIFF_HEAD = 64
DIFF_QK = DIFF_HEADS * 2 * DIFF_HEAD
DIFF_V = DIFF_HEADS * 2 * DIFF_HEAD
DIFF_IN = 2 * DIFF_QK + DIFF_V
ROPE_THETA = 500000.0
ROPE_DIM = DIFF_HEAD // 4
ROPE_HALF = ROPE_DIM // 2

GLA_HEADS = 4
GLA_KEY = 64
GLA_VALUE = 128
GLA_K_WIDTH = GLA_HEADS * GLA_KEY
GLA_V_WIDTH = GLA_HEADS * GLA_VALUE
GLA_GATE_LORA = 16
GLA_TAU = 16.0
GLA_SUB = 16
GLA_IN = 2 * GLA_K_WIDTH + 2 * GLA_V_WIDTH + GLA_GATE_LORA
GLA_IN_PAD = 2 * GLA_K_WIDTH + 2 * GLA_V_WIDTH + 128

N_BRANCHES = 3
MOE_GROUPS = 4
MOE_EXPERTS = 8
MOE_HIDDEN = 256
N_EXPERTS = MOE_GROUPS * MOE_EXPERTS
MOE_TILE = 1024
LANE = 128
V7X_VMEM_LIMIT = 56 * 1024 * 1024


def _cparams(sem):
    return pltpu.CompilerParams(dimension_semantics=sem, vmem_limit_bytes=V7X_VMEM_LIMIT)


def _nt(a, b, **kw):
    return lax.dot_general(a, b, (((1,), (1,)), ((), ())), preferred_element_type=F32, **kw)


def _tn(a, b, **kw):
    return lax.dot_general(a, b, (((0,), (0,)), ((), ())), preferred_element_type=F32, **kw)


def _dot(a, b, **kw):
    return jnp.dot(a, b, preferred_element_type=F32, **kw)


def _mm_kernel(a_ref, b_ref, o_ref, *, gate):
    acc = _dot(a_ref[...], b_ref[0])
    if gate:
        acc = jax.nn.sigmoid(acc)
    o_ref[...] = acc.astype(o_ref.dtype)


def _matmul(a, b, l, out_dtype, tm, tn, gate=False):
    M, K = a.shape
    N = b.shape[2]
    tm, tn = min(tm, M), min(tn, N)
    return pl.pallas_call(
        functools.partial(_mm_kernel, gate=gate),
        grid=(N // tn, M // tm),
        in_specs=[pl.BlockSpec((tm, K), lambda j, i: (i, 0)),
                  pl.BlockSpec((1, K, tn), lambda j, i: (l, 0, j))],
        out_specs=pl.BlockSpec((tm, tn), lambda j, i: (i, j)),
        out_shape=jax.ShapeDtypeStruct((M, N), out_dtype),
        compiler_params=_cparams(("arbitrary", "arbitrary")),
        name="proj_mm",
    )(a, b)


def _mm_rope_kernel(a_ref, b_ref, c_ref, sa_ref, sb_ref, o_ref):
    acc = _dot(a_ref[...], b_ref[0])
    c, sa, sb = c_ref[...], sa_ref[...], sb_ref[...]
    for h in range(acc.shape[1] // LANE):
        blk = acc[:, h * LANE:(h + 1) * LANE]
        out = blk * c + pltpu.roll(blk, LANE - ROPE_HALF, 1) * sa + pltpu.roll(blk, ROPE_HALF, 1) * sb
        o_ref[:, h * LANE:(h + 1) * LANE] = out.astype(o_ref.dtype)


def _matmul_rope(a, b, l, tabs, tm, tn):
    M, K = a.shape
    N = b.shape[2]
    tm, tn = min(tm, M), min(tn, N)
    tab_spec = pl.BlockSpec((tm, LANE), lambda j, i: (i, 0))
    return pl.pallas_call(
        _mm_rope_kernel,
        grid=(N // tn, M // tm),
        in_specs=[pl.BlockSpec((tm, K), lambda j, i: (i, 0)),
                  pl.BlockSpec((1, K, tn), lambda j, i: (l, 0, j)),
                  tab_spec, tab_spec, tab_spec],
        out_specs=pl.BlockSpec((tm, tn), lambda j, i: (i, j)),
        out_shape=jax.ShapeDtypeStruct((M, N), BF16),
        compiler_params=_cparams(("arbitrary", "arbitrary")),
        name="proj_qk_rope",
    )(a, b, *tabs)


def _mm_vt_kernel(wt_ref, x_ref, o_ref):
    o_ref[0] = _nt(wt_ref[0], x_ref[...]).astype(o_ref.dtype)


def _matmul_vt(wt, l, x, tk):
    _, N, K = wt.shape
    T = x.shape[0]
    return pl.pallas_call(
        _mm_vt_kernel,
        grid=(T // tk,),
        in_specs=[pl.BlockSpec((1, N, K), lambda i: (l, 0, 0)),
                  pl.BlockSpec((tk, K), lambda i: (i, 0))],
        out_specs=pl.BlockSpec((1, N, tk), lambda i: (i, 0, 0)),
        out_shape=jax.ShapeDtypeStruct((T // tk, N, tk), BF16),
        compiler_params=_cparams(("arbitrary",)),
        name="proj_vt",
    )(wt, x)


def _rope_tables(T):
    pos = jnp.arange(T, dtype=F32)[:, None]
    inv_freq = ROPE_THETA ** (-jnp.arange(ROPE_HALF, dtype=F32) / ROPE_HALF)
    lane = np.arange(LANE) % DIFF_HEAD
    ang = pos * inv_freq[None, :]
    cos8, sin8 = jnp.cos(ang), jnp.sin(ang)
    idx = jnp.asarray(lane % ROPE_HALF)
    cos_l, sin_l = cos8[:, idx], sin8[:, idx]
    first = jnp.asarray(lane < ROPE_HALF)[None, :]
    second = jnp.asarray((lane >= ROPE_HALF) & (lane < ROPE_DIM))[None, :]
    c = jnp.where(first | second, cos_l, 1.0)
    sa = jnp.where(first, -sin_l, 0.0)
    sb = jnp.where(second, sin_l, 0.0)
    return c, sa, sb


def _rwkv_kernel(a_ref, mu_ref, w2_ref, w0_ref, a2_ref, a0_ref, g2_ref, kks_ref, kas_ref, rk_ref,
                 gng_ref, gnb_ref, seg_ref, tri_ref, o_ref,
                 prev_ref, h_ref, at_ref, rt_ref, bt_ref, kt_ref, bp_ref, kp_ref, v_ref, gc_ref, y_ref):
    TB = a_ref.shape[0]
    C, N, W = RWKV_CHUNK, RWKV_HEAD, RWKV_WIDTH

    @pl.when(pl.program_id(0) == 0)
    def _():
        prev_ref[...] = jnp.zeros_like(prev_ref)
        h_ref[...] = jnp.zeros_like(h_ref)

    a = a_ref[...]
    row = lax.broadcasted_iota(jnp.int32, a.shape, 0)
    shifted = jnp.where(row == 0, prev_ref[...], pltpu.roll(a, 1, 0))
    prev_ref[...] = a[TB - 1:TB, :]
    xs = a + (shifted - a) * mu_ref[...]

    r = xs[:, 0:W]
    k = xs[:, W:2 * W]
    v = xs[:, 2 * W:3 * W]
    w_lo = xs[:, 3 * W:3 * W + LORA_PAD]
    a_lo = xs[:, 3 * W + LORA_PAD:3 * W + 2 * LORA_PAD]
    g_lo = xs[:, 3 * W + 2 * LORA_PAD:]

    z = w0_ref[...] + _dot(jnp.tanh(w_lo).astype(BF16), w2_ref[...])
    w = -jax.nn.softplus(-z) - 0.5
    lw = -jnp.exp(w)
    aa = jax.nn.sigmoid(a0_ref[...] + _dot(a_lo.astype(BF16), a2_ref[...]))
    g = _dot(jax.nn.sigmoid(g_lo).astype(BF16), g2_ref[...])

    seg = seg_ref[...]
    kk = k * kks_ref[...]
    ss = _dot((kk * kk).astype(BF16), seg)
    kk = kk * lax.rsqrt(jnp.maximum(ss, 1e-24))
    km = k * (1.0 + (aa - 1.0) * kas_ref[...])
    beta = kk * aa

    tri = tri_ref[...]
    lw_hi = lw.astype(BF16)
    lw_r = lw - lw_hi.astype(F32)
    lw_mid = lw_r.astype(BF16)
    lw_lo = (lw_r - lw_mid.astype(F32)).astype(BF16)
    cum = _dot(tri, lw_hi) + (_dot(tri, lw_mid) + _dot(tri, lw_lo))
    cum3 = cum.reshape(TB // C, C, W)
    cum_c = jnp.broadcast_to(cum3[:, C - 1:C, :], cum3.shape).reshape(TB, W)

    e_in = jnp.exp(cum)
    e_neg = jnp.exp(-cum)
    e_end = jnp.exp(cum_c - cum)
    at_ref[...] = (-kk * jnp.exp(cum - lw)).astype(BF16)
    rt_ref[...] = (r * e_in).astype(BF16)
    bt_ref[...] = (beta * e_neg).astype(BF16)
    kt_ref[...] = (km * e_neg).astype(BF16)
    bp_ref[...] = (beta * e_end).astype(BF16)
    kp_ref[...] = (km * e_end).astype(BF16)
    v_ref[...] = v.astype(BF16)
    gc_ref[...] = jnp.exp(cum_c)

    ri = lax.broadcasted_iota(jnp.int32, (C, C), 0)
    ci = lax.broadcasted_iota(jnp.int32, (C, C), 1)
    strict = ci < ri
    incl = ci <= ri
    eye = ci == ri

    def chunk(c, carry):
        NC = 2
        rows = [pl.ds(pl.multiple_of((c * NC + ci) * C, C), C) for ci in range(NC)]
        HS = range(NC * RWKV_HEADS)
        rw = [rows[i // RWKV_HEADS] for i in HS]
        sl = [slice((i % RWKV_HEADS) * N, (i % RWKV_HEADS + 1) * N) for i in HS]
        cat = lambda a, b: jnp.concatenate([a, b], axis=0)
        at = [at_ref[rw[i], sl[i]] for i in HS]
        rt = [rt_ref[rw[i], sl[i]] for i in HS]
        bt = [bt_ref[rw[i], sl[i]] for i in HS]
        kt = [kt_ref[rw[i], sl[i]] for i in HS]
        bp = [bp_ref[rw[i], sl[i]] for i in HS]
        kp = [kp_ref[rw[i], sl[i]] for i in HS]
        vh = [v_ref[rw[i], sl[i]] for i in HS]
        ar = [cat(at[h], rt[h]) for h in HS]
        xb = [_nt(ar[h], bt[h]) for h in HS]
        xk = [_nt(ar[h], kt[h]) for h in HS]
        a_ab = [jnp.where(strict, xb[h][:C], 0.0) for h in HS]
        a_rb = [jnp.where(incl, xb[h][C:], 0.0).astype(BF16) for h in HS]
        a_ak = [jnp.where(strict, xk[h][:C], 0.0).astype(BF16) for h in HS]
        a_rk = [jnp.where(incl, xk[h][C:], 0.0).astype(BF16) for h in HS]
        ident = jnp.where(eye, 1.0, 0.0)
        t_inv = [ident + a_ab[h] for h in HS]
        ap = a_ab
        for _ in range(5):
            apb = [ap[h].astype(BF16) for h in HS]
            ap = [_dot(apb[h], apb[h]) for h in HS]
            t_inv = [t_inv[h] + _dot(t_inv[h].astype(BF16), ap[h].astype(BF16)) for h in HS]
        tb = [t_inv[h].astype(BF16) for h in HS]
        wv = [_dot(a_ak[h], vh[h]).astype(BF16) for h in HS]
        pb = [_dot(tb[h], at[h]).astype(BF16) for h in HS]
        u0b = [_dot(tb[h], wv[h]).astype(BF16) for h in HS]
        qm = [rt[h].astype(F32) + _dot(a_rb[h], pb[h]) for h in HS]
        y0 = [_dot(a_rb[h], u0b[h]) + _dot(a_rk[h], vh[h]) for h in HS]
        gm = [jnp.where(eye, gc_ref[rw[h], sl[h]][0:1, :], 0.0) + _tn(bp[h], pb[h]) for h in HS]
        hadd = [_tn(cat(bp[h], kp[h]), cat(u0b[h], vh[h])) for h in HS]
        qg = [cat(qm[h], gm[h]).astype(BF16) for h in HS]
        for i in HS:
            hd = i % RWKV_HEADS
            z = _dot(qg[i], h_ref[hd].astype(BF16))
            y_ref[rw[i], sl[i]] = z[:C] + y0[i]
            h_ref[hd] = z[C:] + hadd[i]
        return carry

    lax.fori_loop(0, TB // (2 * C), chunk, 0)

    y = y_ref[...]
    segm = seg_ref[...]
    mu = _dot(y.astype(BF16), segm) * (1.0 / N)
    yc = y - mu
    var = _dot((yc * yc).astype(BF16), segm) * (1.0 / N)
    yn = yc * lax.rsqrt(var + RWKV_GN_EPS) * gng_ref[...] + gnb_ref[...]
    bonus = _dot((r * km * rk_ref[...]).astype(BF16), segm) * v
    o_ref[...] = ((yn + bonus) * g).astype(o_ref.dtype)


def _rwkv(a_proj, mu, w2, w0, a2, a0, g2, kks, kas, rk, gng, gnb, tb):
    T = a_proj.shape[0]
    tb = min(tb, T)
    W = RWKV_WIDTH
    hid = np.arange(W) // RWKV_HEAD
    seg = jnp.asarray(hid[:, None] == hid[None, :], dtype=BF16)
    t_idx = np.arange(tb)
    tri = jnp.asarray((t_idx[:, None] // RWKV_CHUNK == t_idx[None, :] // RWKV_CHUNK)
                      & (t_idx[None, :] <= t_idx[:, None]), dtype=BF16)
    full = lambda shp: pl.BlockSpec(shp, lambda i: (0,) * len(shp))
    row = lambda n: full((1, n))
    bf = lambda: pltpu.VMEM((tb, W), BF16)
    return pl.pallas_call(
        _rwkv_kernel,
        grid=(T // tb,),
        in_specs=[pl.BlockSpec((tb, RWKV_IN_PAD), lambda i: (i, 0)),
                  row(RWKV_IN_PAD), full((LORA_PAD, W)), row(W), full((LORA_PAD, W)), row(W),
                  full((RWKV_GATE_LORA, W)), row(W), row(W), row(W), row(W), row(W),
                  full((W, W)), full((tb, tb))],
        out_specs=pl.BlockSpec((tb, W), lambda i: (i, 0)),
        out_shape=jax.ShapeDtypeStruct((T, W), BF16),
        scratch_shapes=[pltpu.VMEM((1, RWKV_IN_PAD), F32),
                        pltpu.VMEM((RWKV_HEADS, RWKV_HEAD, RWKV_HEAD), F32),
                        bf(), bf(), bf(), bf(), bf(), bf(), bf(),
                        pltpu.VMEM((tb, W), F32), pltpu.VMEM((tb, W), F32)],
        compiler_params=_cparams(("arbitrary",)),
        name="rwkv7",
    )(a_proj, mu, w2, w0, a2, a0, g2, kks, kas, rk, gng, gnb, seg, tri)


def _diff_kernel(q_ref, k_ref, vt_ref, lq1_ref, lk1_ref, lq2_ref, lk2_ref, sg_ref, o_ref,
                 m_ref, l_ref, acc_ref, s_ref, p_ref, a_ref, *, lambda_init):
    tq = q_ref.shape[0]
    nkb, _, tk = vt_ref.shape
    i = pl.program_id(1)
    q = q_ref[...]
    lane = lax.broadcasted_iota(jnp.int32, q.shape, 1)
    qm = (jnp.where(lane < DIFF_HEAD, q, jnp.zeros_like(q)),
          jnp.where(lane >= DIFF_HEAD, q, jnp.zeros_like(q)))
    m_ref[...] = jnp.full(m_ref.shape, -jnp.inf, F32)
    l_ref[...] = jnp.zeros_like(l_ref)
    acc_ref[...] = jnp.zeros_like(acc_ref)
    p_ref[1] = jnp.zeros(p_ref.shape[1:], BF16)
    a_ref[1] = jnp.ones(a_ref.shape[1:], F32)

    def scores(j, slot):
        jc = jnp.minimum(j, nkb - 1)
        kj = k_ref[pl.ds(pl.multiple_of(jc * tk, tk), tk), :]
        for mi in range(2):
            s_ref[slot, mi] = _nt(kj, qm[mi])

    def values(j, slot):
        vj = vt_ref[jnp.clip(j, 0, nkb - 1)]
        for mi in range(2):
            acc_ref[mi] = a_ref[slot, mi] * acc_ref[mi] + _dot(vj, p_ref[slot, mi])

    def softmax(j, slot, masked):
        for mi in range(2):
            s = s_ref[slot, mi]
            if masked:
                d = (lax.broadcasted_iota(jnp.int32, (tk, tq), 0)
                     - lax.broadcasted_iota(jnp.int32, (tk, tq), 1))
                s = jnp.where(d <= i * tq - j * tk, s, -jnp.inf)
            m_old = m_ref[mi]
            m_new = jnp.maximum(m_old, jnp.max(s, axis=0, keepdims=True))
            alpha = jnp.exp2(m_old - m_new)
            p = jnp.exp2(s - m_new)
            l_ref[mi] = alpha * l_ref[mi] + jnp.sum(p, axis=0, keepdims=True)
            m_ref[mi] = m_new
            a_ref[slot, mi] = alpha
            p_ref[slot, mi] = p.astype(BF16)

    def half(j, cur, masked):
        nxt = 1 - cur
        scores(j + 1, nxt)
        values(j - 1, nxt)
        softmax(j, cur, masked)

    def pair(jj, carry):
        half(2 * jj, 0, False)
        half(2 * jj + 1, 1, False)
        return carry

    scores(0, 0)
    n_pairs = ((i + 1) * (tq // tk) + 1) // 2
    lax.fori_loop(0, n_pairs - 1, pair, 0)
    j_last = 2 * (n_pairs - 1)
    half(j_last, 0, True)
    half(j_last + 1, 1, True)
    values(j_last + 1, 1)

    lam = (jnp.exp(jnp.sum(lq1_ref[...] * lk1_ref[...])) - jnp.exp(jnp.sum(lq2_ref[...] * lk2_ref[...]))
           + lambda_init)
    o = acc_ref[0] / l_ref[0] - lam * (acc_ref[1] / l_ref[1])
    o = o * lax.rsqrt(jnp.mean(o * o, axis=0, keepdims=True) + LN_EPS) * (1.0 - lambda_init)
    o = o * sg_ref[...]
    o_ref[...] = o.T.astype(o_ref.dtype)


def _diff_attention(qk, vt, lq1, lk1, lq2, lk2, subln_g, lambda_init, tq):
    T = qk.shape[0]
    tk = vt.shape[2]
    assert tq % tk == 0 and T % tq == 0
    H, E = DIFF_HEADS, 2 * DIFF_HEAD
    vec = lambda: pl.BlockSpec((1, DIFF_HEAD), lambda h, i: (0, 0))
    return pl.pallas_call(
        functools.partial(_diff_kernel, lambda_init=lambda_init),
        grid=(H, T // tq),
        in_specs=[pl.BlockSpec((tq, E), lambda h, i: (i, h)),
                  pl.BlockSpec((T, E), lambda h, i: (0, H + h)),
                  pl.BlockSpec((T // tk, E, tk), lambda h, i: (0, h, 0)),
                  vec(), vec(), vec(), vec(),
                  pl.BlockSpec((E, 1), lambda h, i: (0, 0))],
        out_specs=pl.BlockSpec((tq, E), lambda h, i: (i, h)),
        out_shape=jax.ShapeDtypeStruct((T, DIFF_V), BF16),
        scratch_shapes=[pltpu.VMEM((2, 1, tq), F32), pltpu.VMEM((2, 1, tq), F32),
                        pltpu.VMEM((2, E, tq), F32), pltpu.VMEM((2, 2, tk, tq), F32),
                        pltpu.VMEM((2, 2, tk, tq), BF16), pltpu.VMEM((2, 2, 1, tq), F32)],
        compiler_params=_cparams(("arbitrary", "arbitrary")),
        name="diff_attn",
    )(qk, qk, vt, lq1, lk1, lq2, lk2, subln_g)


def _gla_kernel(c_ref, g2_ref, gb_ref, gng_ref, tri_ref, segv_ref, o_ref,
                st_ref, oacc_ref):
    TB = c_ref.shape[0]
    KW, VW, S = GLA_K_WIDTH, GLA_V_WIDTH, GLA_SUB
    DK, DV = GLA_KEY, GLA_VALUE

    @pl.when(pl.program_id(0) == 0)
    def _():
        st_ref[...] = jnp.zeros_like(st_ref)

    g_lo = c_ref[:, 2 * KW + 2 * VW:]
    gate = _dot(g_lo, g2_ref[...], precision=HIGHEST) + gb_ref[...]
    log_a = jax.nn.log_sigmoid(gate) * (1.0 / GLA_TAU)
    b = _dot(tri_ref[...], log_a, precision=HIGHEST)
    NS = TB // S
    b3 = b.reshape(NS, S, KW)
    q3 = c_ref[:, 0:KW].reshape(NS, S, KW)
    k3 = c_ref[:, KW:2 * KW].reshape(NS, S, KW)
    v = c_ref[:, 2 * KW:2 * KW + VW]
    v3 = v.reshape(NS, S, VW)
    bl3 = b3[:, S - 1:S, :]
    qb = (q3 * jnp.exp(b3)).astype(BF16).reshape(TB, KW)
    kb = (k3 * jnp.exp(bl3 - b3)).astype(BF16).reshape(TB, KW)
    dec = jnp.exp(bl3)
    vb = v.astype(BF16)

    ri = lax.broadcasted_iota(jnp.int32, (NS, S, 1), 1)
    segv = segv_ref[...]
    acc = jnp.zeros((NS, S, VW), F32)
    for j in range(S):
        e = jnp.exp(jnp.minimum(b3 - b3[:, j:j + 1, :], 0.0))
        t = (q3 * e * k3[:, j:j + 1, :]).astype(BF16).reshape(TB, KW)
        att = _dot(t, segv).reshape(NS, S, VW)
        acc = acc + jnp.where(ri >= j, att, 0.0) * v3[:, j:j + 1, :]
    oacc_ref[...] = acc.reshape(TB, VW)

    for h in range(GLA_HEADS):
        lk = slice(h * DK, (h + 1) * DK)
        lv = slice(h * DV, (h + 1) * DV)
        kv = [_tn(vb[s * S:(s + 1) * S, lv], kb[s * S:(s + 1) * S, lk]) for s in range(NS)]
        st = st_ref[h]
        for s in range(NS):
            rows = slice(s * S, (s + 1) * S)
            oacc_ref[rows, lv] += _nt(qb[rows, lk], st.astype(BF16))
            st = st * dec[s][:, lk] + kv[s]
        st_ref[h] = st

    o = oacc_ref[...]
    rgate = c_ref[:, 2 * KW + VW:2 * KW + 2 * VW]
    for h in range(GLA_HEADS):
        lv = slice(h * DV, (h + 1) * DV)
        oh = o[:, lv]
        oh = oh * lax.rsqrt(jnp.mean(oh * oh, axis=1, keepdims=True) + LN_EPS) * gng_ref[:, lv]
        o_ref[:, lv] = (oh * jax.nn.silu(rgate[:, lv])).astype(o_ref.dtype)


def _gla(c_proj, g2, gb, gng, tb):
    T = c_proj.shape[0]
    tb = min(tb, T)
    KW, VW = GLA_K_WIDTH, GLA_V_WIDTH
    t_idx = np.arange(tb)
    tri = jnp.asarray((t_idx[:, None] // GLA_SUB == t_idx[None, :] // GLA_SUB)
                      & (t_idx[None, :] <= t_idx[:, None]), dtype=F32)
    segv = jnp.asarray((np.arange(KW) // GLA_KEY)[:, None] == (np.arange(VW) // GLA_VALUE)[None, :], dtype=BF16)
    full = lambda shp: pl.BlockSpec(shp, lambda i: (0,) * len(shp))
    return pl.pallas_call(
        _gla_kernel,
        grid=(T // tb,),
        in_specs=[pl.BlockSpec((tb, GLA_IN_PAD), lambda i: (i, 0)),
                  full((LANE, KW)), full((1, KW)), full((1, VW)), full((tb, tb)), full((KW, VW))],
        out_specs=pl.BlockSpec((tb, VW), lambda i: (i, 0)),
        out_shape=jax.ShapeDtypeStruct((T, VW), BF16),
        scratch_shapes=[pltpu.VMEM((GLA_HEADS, GLA_VALUE, GLA_KEY), F32),
                        pltpu.VMEM((tb, VW), F32)],
        compiler_params=_cparams(("arbitrary",)),
        name="gla",
    )(c_proj, g2, gb, gng, tri, segv)


def _merge_kernel(oa_ref, ob_ref, oc_ref, g_ref, pa_ref, pb_ref, pc_ref, o_ref):
    D = D_MODEL
    m = g_ref[:, 0:D].astype(F32) * _dot(oa_ref[...], pa_ref[...])
    m = m + g_ref[:, D:2 * D].astype(F32) * _dot(ob_ref[...], pb_ref[...])
    m = m + g_ref[:, 2 * D:3 * D].astype(F32) * _dot(oc_ref[...], pc_ref[...])
    o_ref[...] = m.astype(o_ref.dtype)


def _merge(oa, ob, oc, gates, pa, pb, pc, tm):
    T = oa.shape[0]
    tm = min(tm, T)
    D = D_MODEL
    rowblk = lambda n: pl.BlockSpec((tm, n), lambda i: (i, 0))
    full = lambda shp: pl.BlockSpec(shp, lambda i: (0, 0))
    return pl.pallas_call(
        _merge_kernel,
        grid=(T // tm,),
        in_specs=[rowblk(RWKV_WIDTH), rowblk(DIFF_V), rowblk(GLA_V_WIDTH), rowblk(3 * D),
                  full((RWKV_WIDTH, D)), full((DIFF_V, D)), full((GLA_V_WIDTH, D))],
        out_specs=rowblk(D),
        out_shape=jax.ShapeDtypeStruct((T, D), BF16),
        compiler_params=_cparams(("arbitrary",)),
        name="merge",
    )(oa, ob, oc, gates, pa, pb, pc)


def _layer_norm(z, g, b):
    mu = jnp.mean(z, axis=1, keepdims=True)
    zc = z - mu
    var = jnp.mean(zc * zc, axis=1, keepdims=True)
    return zc * lax.rsqrt(var + LN_EPS) * g + b


def _mm_ln_kernel(a_ref, w_ref, res_ref, g_ref, b_ref, wrh_ref, wrl_ref, br_ref, o_ref):
    z = DEEPNORM_ALPHA * res_ref[...] + _dot(a_ref[...], w_ref[...])
    out = _layer_norm(z, g_ref[...], b_ref[...])
    o_ref[:, 0:D_MODEL] = out
    o_ref[:, D_MODEL:] = _route(out, wrh_ref[...], wrl_ref[...], br_ref[...])


def _mm_ln(a, w, res, g, b, wr, br, tm):
    wr_hi = wr.astype(BF16)
    wr_lo = (wr - wr_hi.astype(F32)).astype(BF16)
    T, K = a.shape
    D = D_MODEL
    tm = min(tm, T)
    rowblk = lambda n: pl.BlockSpec((tm, n), lambda i: (i, 0))
    full = lambda shp: pl.BlockSpec(shp, lambda i: (0, 0))
    return pl.pallas_call(
        _mm_ln_kernel,
        grid=(T // tm,),
        in_specs=[rowblk(K), full((K, D)), rowblk(D), full((1, D)), full((1, D)), full((D, LANE)), full((D, LANE)),
                  full((1, LANE))],
        out_specs=rowblk(D + LANE),
        out_shape=jax.ShapeDtypeStruct((T, D + LANE), F32),
        compiler_params=_cparams(("arbitrary",)),
        name="wo_ln_route",
    )(a, w, res, g, b, wr_hi, wr_lo, br)


def _route(x, w_hi, w_lo, b):
    G, E = MOE_GROUPS, MOE_EXPERTS
    x_hi = x.astype(BF16)
    x_lo = (x - x_hi.astype(F32)).astype(BF16)
    logits = _dot(x_hi, w_hi) + (_dot(x_lo, w_hi) + _dot(x_hi, w_lo)) + b
    lane = lax.broadcasted_iota(jnp.int32, logits.shape, 1)
    neg = -jnp.inf
    big = jnp.int32(1 << 20)
    is_g = lane < G
    lg = jnp.where(is_g, logits, neg)
    gmax = jnp.max(lg, axis=1, keepdims=True)
    gidx = jnp.min(jnp.where(is_g & (lg == gmax), lane, big), axis=1, keepdims=True)
    g_p = 1.0 / jnp.sum(jnp.exp(lg - gmax), axis=1, keepdims=True)
    lo = G + gidx * E
    in_grp = (lane >= lo) & (lane < lo + E)
    le = jnp.where(in_grp, logits, neg)
    m1 = jnp.max(le, axis=1, keepdims=True)
    i1 = jnp.min(jnp.where(in_grp & (le == m1), lane, big), axis=1, keepdims=True)
    le2 = jnp.where(lane == i1, neg, le)
    m2 = jnp.max(le2, axis=1, keepdims=True)
    i2 = jnp.min(jnp.where(in_grp & (le2 == m2), lane, big), axis=1, keepdims=True)
    e2 = jnp.exp(m2 - m1)
    p1 = 1.0 / (1.0 + e2)
    p2 = e2 / (1.0 + e2)
    comb = jnp.where(lane == i1, g_p * p1, 0.0) + jnp.where(lane == i2, g_p * p2, 0.0)
    return jnp.where(lane == 0, gidx.astype(F32), comb)


def _moe_tables(gid_f, tm):
    T = gid_f.shape[0]
    G = MOE_GROUPS
    nt = T // tm + G
    gid = gid_f.astype(jnp.int32)
    order = jnp.argsort(gid).astype(jnp.int32)
    counts = jnp.sum((gid[:, None] == jnp.arange(G)[None, :]).astype(jnp.int32), axis=0)
    starts = jnp.cumsum(counts) - counts
    ntile = (counts + tm - 1) // tm
    tstart = jnp.cumsum(ntile) - ntile
    n = jnp.arange(nt, dtype=jnp.int32)
    tg = jnp.sum((n[:, None] >= tstart[None, 1:]).astype(jnp.int32), axis=1)
    k = n - tstart[tg]
    row0 = starts[tg] + k * tm
    nvalid = jnp.clip(counts[tg] - k * tm, 0, tm).astype(jnp.int32)
    rows = jnp.clip(row0[:, None] + jnp.arange(tm, dtype=jnp.int32)[None, :], 0, T - 1)
    idx = jnp.take(order, rows).reshape(nt, 1, tm)
    return tg.astype(jnp.int32), nvalid, idx


def _moe_kernel(tg_ref, nv_ref, idx_ref, idxn_ref, x_hbm, wg_ref, wu_ref, wd_ref, g_ref, b_ref, y_hbm,
                xs_ref, xb_ref, h_ref, out_ref, pend_ref, gx_sem, sc_sem):
    n = pl.program_id(0)
    e = pl.program_id(1)
    nt = pl.num_programs(0)
    nv = nv_ref[n]
    slot = n % 2
    other = 1 - slot
    tm = out_ref.shape[0]
    part = tm // MOE_EXPERTS
    F = MOE_HIDDEN

    def start_row(idx, dst_slot, r, prio):
        tok = idx[0, 0, r]
        pltpu.make_async_copy(x_hbm.at[pl.ds(tok, 1)], xs_ref.at[dst_slot, pl.ds(r, 1)],
                              gx_sem.at[dst_slot]).start(priority=prio)

    def start_next_part():
        for r in range(part):
            start_row(idxn_ref, other, e * part + r, r % 2)

    def wait_rows(s):
        pltpu.make_async_copy(x_hbm.at[pl.ds(0, tm)], xs_ref.at[s], gx_sem.at[s]).wait()

    def drain():
        @pl.when(pend_ref[0] == tm)
        def _():
            pltpu.make_async_copy(out_ref, y_hbm.at[pl.ds(0, tm)], sc_sem.at[0]).wait()

        @pl.when(pend_ref[0] < tm)
        def _():
            def body(r, c):
                pltpu.make_async_copy(out_ref.at[pl.ds(0, 1)], y_hbm.at[pl.ds(0, 1)], sc_sem.at[0]).wait()
                return c
            lax.fori_loop(0, pend_ref[0], body, 0)
        pend_ref[0] = 0

    @pl.when((n == 0) & (e == 0))
    def _():
        pend_ref[0] = 0

        def body(r, c):
            start_row(idx_ref, 0, 2 * r, 0)
            start_row(idx_ref, 0, 2 * r + 1, 1)
            return c
        lax.fori_loop(0, tm // 2, body, 0)

    @pl.when(e == 0)
    def _():
        wait_rows(slot)
        xb_ref[...] = xs_ref[slot, :, 0:D_MODEL].astype(BF16)

    @pl.when(nv > 0)
    def _():
        start_next_part()
        xb = xb_ref[...]
        hg = _dot(xb, wg_ref[0])
        hu = _dot(xb, wu_ref[0])
        comb = xs_ref[slot, :, D_MODEL:]
        lane = lax.broadcasted_iota(jnp.int32, comb.shape, 1)
        col = MOE_GROUPS + tg_ref[n] * MOE_EXPERTS + e
        cw = jnp.sum(jnp.where(lane == col, comb, 0.0), axis=1, keepdims=True)
        h_ref[e] = (jax.nn.silu(hg) * hu * cw).astype(BF16)

    @pl.when(nv == 0)
    def _():
        start_next_part()

    @pl.when(e == MOE_EXPERTS - 1)
    def _():
        drain()

        @pl.when(nv > 0)
        def _():
            y = _dot(h_ref[0], wd_ref[0, 0:F, :])
            for j in range(1, MOE_EXPERTS):
                y = y + _dot(h_ref[j], wd_ref[0, j * F:(j + 1) * F, :])
            z = DEEPNORM_ALPHA * xs_ref[slot, :, 0:D_MODEL] + y
            out_ref[...] = _layer_norm(z, g_ref[...], b_ref[...])

        def put(r, prio):
            tok = idx_ref[0, 0, r]
            pltpu.make_async_copy(out_ref.at[pl.ds(r, 1)], y_hbm.at[pl.ds(tok, 1)],
                                  sc_sem.at[0]).start(priority=prio)

        def body(r, c):
            put(2 * r, 0)
            put(2 * r + 1, 1)
            return c
        lax.fori_loop(0, nv // 2, body, 0)

        @pl.when(nv % 2 == 1)
        def _():
            put(nv - 1, 0)
        pend_ref[0] = nv

        @pl.when(n == nt - 1)
        def _():
            drain()
            wait_rows(other)


def _moe(xa, wg, wu, wd, l, g, b, tm):
    T = xa.shape[0]
    D = D_MODEL
    F = MOE_HIDDEN
    tm = min(tm, T)
    tg, nvalid, idx = _moe_tables(xa[:, D], tm)
    nt = tg.shape[0]
    wspec = lambda shp: pl.BlockSpec(shp, lambda n, e, tg_r, nv_r: (l * N_EXPERTS + tg_r[n] * MOE_EXPERTS + e, 0, 0))
    ispec = lambda f: pl.BlockSpec((1, 1, tm), f, memory_space=pltpu.SMEM)
    vec = pl.BlockSpec((1, D), lambda n, e, tg_r, nv_r: (0, 0))
    grid_spec = pltpu.PrefetchScalarGridSpec(
        num_scalar_prefetch=2,
        grid=(nt, MOE_EXPERTS),
        in_specs=[ispec(lambda n, e, tg_r, nv_r: (n, 0, 0)),
                  ispec(lambda n, e, tg_r, nv_r: (jnp.minimum(n + 1, nt - 1), 0, 0)),
                  pl.BlockSpec(memory_space=pl.ANY),
                  wspec((1, D, F)), wspec((1, D, F)),
                  pl.BlockSpec((1, MOE_EXPERTS * F, D), lambda n, e, tg_r, nv_r: (l * MOE_GROUPS + tg_r[n], 0, 0),
                               pipeline_mode=pl.Buffered(1)),
                  vec, vec],
        out_specs=pl.BlockSpec(memory_space=pl.ANY),
        scratch_shapes=[pltpu.VMEM((2, tm, D + LANE), F32),
                        pltpu.VMEM((tm, D), BF16), pltpu.VMEM((MOE_EXPERTS, tm, F), BF16),
                        pltpu.VMEM((tm, D), F32), pltpu.SMEM((1,), jnp.int32),
                        pltpu.SemaphoreType.DMA((2,)), pltpu.SemaphoreType.DMA((1,))],
    )
    return pl.pallas_call(
        _moe_kernel,
        grid_spec=grid_spec,
        out_shape=jax.ShapeDtypeStruct((T, D), F32),
        compiler_params=_cparams(("arbitrary", "arbitrary")),
        name="moe_experts_ln",
    )(tg, nvalid, idx, idx, xa, wg, wu, wd, g, b)


def _pad_cols(w, n):
    return jnp.pad(w, ((0, 0), (0, n - w.shape[1])))


def _prep_weights(p):
    W = RWKV_WIDTH
    KW, VW = GLA_K_WIDTH, GLA_V_WIDTH
    w_in = p['w_in']
    o_b = RWKV_IN
    o_c = o_b + DIFF_IN
    o_g = o_c + GLA_IN
    lo1 = 3 * W + RWKV_DECAY_LORA
    lo2 = lo1 + RWKV_AAA_LORA
    padc = lambda w, n: jnp.pad(w, ((0, 0), (0, 0), (0, n - w.shape[2])))
    w_a = jnp.concatenate([padc(w_in[:, :, :lo1], 3 * W + LORA_PAD), padc(w_in[:, :, lo1:lo2], LORA_PAD),
                           w_in[:, :, lo2:o_b]], axis=2).astype(BF16)
    qscale = DIFF_HEAD ** -0.5 * math.log2(math.e)
    w_qk = jnp.concatenate([w_in[:, :, o_b:o_b + DIFF_QK] * qscale, w_in[:, :, o_b + DIFF_QK:o_b + 2 * DIFF_QK]],
                           axis=2).astype(BF16)
    w_vt = jnp.swapaxes(w_in[:, :, o_b + 2 * DIFF_QK:o_c], 1, 2).astype(BF16)
    c0 = o_c
    w_c = jnp.concatenate([w_in[:, :, c0:c0 + KW] * (GLA_KEY ** -0.5), w_in[:, :, c0 + KW:c0 + 2 * KW + VW],
                           w_in[:, :, c0 + 2 * KW + VW + GLA_GATE_LORA:o_g],
                           padc(w_in[:, :, c0 + 2 * KW + VW:c0 + 2 * KW + VW + GLA_GATE_LORA], LANE)],
                          axis=2).astype(BF16)
    w_g = w_in[:, :, o_g:].astype(BF16)
    L = w_in.shape[0]
    F = MOE_HIDDEN
    return dict(
        w_a=w_a, w_qk=w_qk, w_vt=w_vt, w_c=w_c, w_g=w_g,
        wg=p['moe_w_gate'].astype(BF16).reshape(L * N_EXPERTS, D_MODEL, F),
        wu=p['moe_w_up'].astype(BF16).reshape(L * N_EXPERTS, D_MODEL, F),
        wd=p['moe_w_down'].astype(BF16).reshape(L * MOE_GROUPS, MOE_EXPERTS * F, D_MODEL))


def _layer(x, xb, l, p, wts, rope_tabs):
    T = x.shape[0]
    W = RWKV_WIDTH
    lambda_init = 0.8 - 0.6 * math.exp(-0.3 * l)
    lo1 = 3 * W + RWKV_DECAY_LORA
    lo2 = lo1 + RWKV_AAA_LORA
    mu = p['rwkv_mu'][l]
    mu_a = jnp.concatenate([jnp.pad(mu[:lo1], (0, LORA_PAD - RWKV_DECAY_LORA)),
                            jnp.pad(mu[lo1:lo2], (0, LORA_PAD - RWKV_AAA_LORA)), mu[lo2:]])[None, :]

    proj_a = _matmul(xb, wts['w_a'], l, F32, 1024, 1024)
    qk = _matmul_rope(xb, wts['w_qk'], l, rope_tabs, 1024, 1024)
    tk = min(512, T)
    vt = _matmul_vt(wts['w_vt'], l, xb, tk)
    proj_c = _matmul(xb, wts['w_c'], l, F32, 512, GLA_IN_PAD)
    gates = _matmul(xb, wts['w_g'], l, BF16, 1024, 1024, gate=True)

    padr = lambda w: jnp.pad(w, ((0, LORA_PAD - w.shape[0]), (0, 0)))
    r2 = lambda v: v.reshape(1, -1)
    o_a = _rwkv(proj_a, mu_a, padr(p['rwkv_w2'][l]).astype(BF16), r2(p['rwkv_w0'][l]),
                padr(p['rwkv_a2'][l]).astype(BF16), r2(p['rwkv_a0'][l]), p['rwkv_g2'][l].astype(BF16),
                r2(p['rwkv_kk_scale'][l]), r2(p['rwkv_ka_scale'][l]), r2(p['rwkv_rk'][l]),
                r2(p['rwkv_gn_g'][l]), r2(p['rwkv_gn_b'][l]), 256)
    o_bb = _diff_attention(qk, vt, r2(p['diff_lq1'][l]), r2(p['diff_lk1'][l]), r2(p['diff_lq2'][l]),
                           r2(p['diff_lk2'][l]), p['diff_subln_g'][l].reshape(-1, 1), lambda_init, min(2 * tk, T))
    g2p = jnp.pad(p['gla_g2'][l], ((0, LANE - GLA_GATE_LORA), (0, 0)))
    o_c = _gla(proj_c, g2p, r2(p['gla_gb'][l]), r2(p['gla_gn_g'][l]), 256)

    merged = _merge(o_a, o_bb, o_c, gates, p['proj_a'][l].astype(BF16), p['proj_b'][l].astype(BF16),
                    p['proj_c'][l].astype(BF16), 256)
    wr = _pad_cols(jnp.concatenate([p['router_group_w'][l], p['router_expert_w'][l]], axis=1), LANE)
    br = _pad_cols(jnp.concatenate([p['router_group_b'][l], p['router_expert_b'][l]])[None, :], LANE)
    xa = _mm_ln(merged, p['w_o'][l].astype(BF16), x, r2(p['ln1_g'][l]), r2(p['ln1_b'][l]), wr, br, 512)
    return _moe(xa, wts['wg'], wts['wu'], wts['wd'], l, r2(p['ln2_g'][l]), r2(p['ln2_b'][l]), MOE_TILE)


def kernel(x, w_in, rwkv_mu, rwkv_w2, rwkv_w0, rwkv_a2, rwkv_a0, rwkv_g2, rwkv_kk_scale, rwkv_ka_scale, rwkv_rk, rwkv_gn_g, rwkv_gn_b, diff_lq1, diff_lk1, diff_lq2, diff_lk2, diff_subln_g, gla_g2, gla_gb, gla_gn_g, proj_a, proj_b, proj_c, w_o, ln1_g, ln1_b, router_group_w, router_group_b, router_expert_w, router_expert_b, moe_w_gate, moe_w_up, moe_w_down, ln2_g, ln2_b):
    p = dict(w_in=w_in, rwkv_mu=rwkv_mu, rwkv_w2=rwkv_w2, rwkv_w0=rwkv_w0, rwkv_a2=rwkv_a2, rwkv_a0=rwkv_a0,
             rwkv_g2=rwkv_g2, rwkv_kk_scale=rwkv_kk_scale, rwkv_ka_scale=rwkv_ka_scale, rwkv_rk=rwkv_rk,
             rwkv_gn_g=rwkv_gn_g, rwkv_gn_b=rwkv_gn_b, diff_lq1=diff_lq1, diff_lk1=diff_lk1, diff_lq2=diff_lq2,
             diff_lk2=diff_lk2, diff_subln_g=diff_subln_g, gla_g2=gla_g2, gla_gb=gla_gb, gla_gn_g=gla_gn_g,
             proj_a=proj_a, proj_b=proj_b, proj_c=proj_c, w_o=w_o, ln1_g=ln1_g, ln1_b=ln1_b,
             router_group_w=router_group_w, router_group_b=router_group_b, router_expert_w=router_expert_w,
             router_expert_b=router_expert_b, moe_w_gate=moe_w_gate, moe_w_up=moe_w_up, moe_w_down=moe_w_down,
             ln2_g=ln2_g, ln2_b=ln2_b)
    B, T, D = x.shape
    assert B == 1 and D == D_MODEL
    xf = x.reshape(T, D)
    xb = xf.astype(BF16)
    tabs = _rope_tables(T)
    wts = _prep_weights(p)
    for l in range(DEPTH):
        xf = _layer(xf, xb, l, p, wts, tabs)
        xb = xf.astype(BF16)
    return xf.reshape(B, T, D)
```

```python
import functools
import math

import numpy as np
import jax
import jax.numpy as jnp
from jax import lax
from jax.experimental import pallas as pl
from jax.experimental.pallas import tpu as pltpu

F32 = jnp.float32
BF16 = jnp.bfloat16
HIGHEST = lax.Precision.HIGHEST

D_MODEL = 2048
DEPTH = 2
DEEPNORM_ALPHA = (2 * DEPTH) ** 0.25
LN_EPS = 1e-5

RWKV_HEADS = 8
RWKV_HEAD = 64
RWKV_WIDTH = RWKV_HEADS * RWKV_HEAD
RWKV_DECAY_LORA = 96
RWKV_AAA_LORA = 96
RWKV_GATE_LORA = 256
RWKV_GN_EPS = 64e-5
RWKV_IN = 3 * RWKV_WIDTH + RWKV_DECAY_LORA + RWKV_AAA_LORA + RWKV_GATE_LORA
RWKV_CHUNK = 64
LORA_PAD = 128
RWKV_IN_PAD = 3 * RWKV_WIDTH + 2 * LORA_PAD + RWKV_GATE_LORA

DIFF_HEADS = 8
DIFF_HEAD = 64
DIFF_QK = DIFF_HEADS * 2 * DIFF_HEAD
DIFF_V = DIFF_HEADS * 2 * DIFF_HEAD
DIFF_IN = 2 * DIFF_QK + DIFF_V
ROPE_THETA = 500000.0
ROPE_DIM = DIFF_HEAD // 4
ROPE_HALF = ROPE_DIM // 2

GLA_HEADS = 4
GLA_KEY = 64
GLA_VALUE = 128
GLA_K_WIDTH = GLA_HEADS * GLA_KEY
GLA_V_WIDTH = GLA_HEADS * GLA_VALUE
GLA_GATE_LORA = 16
GLA_TAU = 16.0
GLA_SUB = 16
GLA_IN = 2 * GLA_K_WIDTH + 2 * GLA_V_WIDTH + GLA_GATE_LORA
GLA_IN_PAD = 2 * GLA_K_WIDTH + 2 * GLA_V_WIDTH + 128

N_BRANCHES = 3
MOE_GROUPS = 4
MOE_EXPERTS = 8
MOE_HIDDEN = 256
N_EXPERTS = MOE_GROUPS * MOE_EXPERTS
MOE_TILE = 1024
LANE = 128
V7X_VMEM_LIMIT = 56 * 1024 * 1024


def _cparams(sem):
    return pltpu.CompilerParams(dimension_semantics=sem, vmem_limit_bytes=V7X_VMEM_LIMIT)


def _nt(a, b, **kw):
    return lax.dot_general(a, b, (((1,), (1,)), ((), ())), preferred_element_type=F32, **kw)


def _tn(a, b, **kw):
    return lax.dot_general(a, b, (((0,), (0,)), ((), ())), preferred_element_type=F32, **kw)


def _dot(a, b, **kw):
    return jnp.dot(a, b, preferred_element_type=F32, **kw)


def _mm_kernel(a_ref, b_ref, o_ref, *, gate):
    acc = _nt(a_ref[...], b_ref[0])
    if gate:
        acc = jax.nn.sigmoid(acc)
    o_ref[...] = acc.astype(o_ref.dtype)


def _matmul(a, b, l, out_dtype, tm, tn, gate=False):
    M, K = a.shape
    N = b.shape[1]
    tm, tn = min(tm, M), min(tn, N)
    return pl.pallas_call(
        functools.partial(_mm_kernel, gate=gate),
        grid=(N // tn, M // tm),
        in_specs=[pl.BlockSpec((tm, K), lambda j, i: (i, 0)),
                  pl.BlockSpec((1, tn, K), lambda j, i: (l, j, 0))],
        out_specs=pl.BlockSpec((tm, tn), lambda j, i: (i, j)),
        out_shape=jax.ShapeDtypeStruct((M, N), out_dtype),
        compiler_params=_cparams(("arbitrary", "arbitrary")),
        name="proj_mm",
    )(a, b)


def _mm_rope_kernel(a_ref, b_ref, c_ref, sa_ref, sb_ref, o_ref):
    acc = _nt(a_ref[...], b_ref[0])
    c, sa, sb = c_ref[...], sa_ref[...], sb_ref[...]
    for h in range(acc.shape[1] // LANE):
        blk = acc[:, h * LANE:(h + 1) * LANE]
        out = blk * c + pltpu.roll(blk, LANE - ROPE_HALF, 1) * sa + pltpu.roll(blk, ROPE_HALF, 1) * sb
        o_ref[:, h * LANE:(h + 1) * LANE] = out.astype(o_ref.dtype)


def _matmul_rope(a, b, l, tabs, tm, tn):
    M, K = a.shape
    N = b.shape[1]
    tm, tn = min(tm, M), min(tn, N)
    tab_spec = pl.BlockSpec((tm, LANE), lambda j, i: (i, 0))
    return pl.pallas_call(
        _mm_rope_kernel,
        grid=(N // tn, M // tm),
        in_specs=[pl.BlockSpec((tm, K), lambda j, i: (i, 0)),
                  pl.BlockSpec((1, tn, K), lambda j, i: (l, j, 0)),
                  tab_spec, tab_spec, tab_spec],
        out_specs=pl.BlockSpec((tm, tn), lambda j, i: (i, j)),
        out_shape=jax.ShapeDtypeStruct((M, N), BF16),
        compiler_params=_cparams(("arbitrary", "arbitrary")),
        name="proj_qk_rope",
    )(a, b, *tabs)


def _mm_vt_kernel(wt_ref, x_ref, o_ref):
    o_ref[0] = _nt(wt_ref[0], x_ref[...]).astype(o_ref.dtype)


def _matmul_vt(wt, l, x, tk):
    _, N, K = wt.shape
    T = x.shape[0]
    return pl.pallas_call(
        _mm_vt_kernel,
        grid=(T // tk,),
        in_specs=[pl.BlockSpec((1, N, K), lambda i: (l, 0, 0)),
                  pl.BlockSpec((tk, K), lambda i: (i, 0))],
        out_specs=pl.BlockSpec((1, N, tk), lambda i: (i, 0, 0)),
        out_shape=jax.ShapeDtypeStruct((T // tk, N, tk), BF16),
        compiler_params=_cparams(("arbitrary",)),
        name="proj_vt",
    )(wt, x)


def _rope_tables(T):
    pos = jnp.arange(T, dtype=F32)[:, None]
    inv_freq = ROPE_THETA ** (-jnp.arange(ROPE_HALF, dtype=F32) / ROPE_HALF)
    lane = np.arange(LANE) % DIFF_HEAD
    ang = pos * inv_freq[None, :]
    cos8, sin8 = jnp.cos(ang), jnp.sin(ang)
    idx = jnp.asarray(lane % ROPE_HALF)
    cos_l, sin_l = cos8[:, idx], sin8[:, idx]
    first = jnp.asarray(lane < ROPE_HALF)[None, :]
    second = jnp.asarray((lane >= ROPE_HALF) & (lane < ROPE_DIM))[None, :]
    c = jnp.where(first | second, cos_l, 1.0)
    sa = jnp.where(first, -sin_l, 0.0)
    sb = jnp.where(second, sin_l, 0.0)
    return c, sa, sb


def _rwkv_kernel(a_ref, mu_ref, w2_ref, w0_ref, a2_ref, a0_ref, g2_ref, kks_ref, kas_ref, rk_ref,
                 gng_ref, gnb_ref, seg_ref, tri_ref, o_ref,
                 prev_ref, h_ref, at_ref, rt_ref, bt_ref, kt_ref, bp_ref, kp_ref, v_ref, gc_ref, y_ref):
    TB = a_ref.shape[0]
    C, N, W = RWKV_CHUNK, RWKV_HEAD, RWKV_WIDTH

    @pl.when(pl.program_id(0) == 0)
    def _():
        prev_ref[...] = jnp.zeros_like(prev_ref)
        h_ref[...] = jnp.zeros_like(h_ref)

    a = a_ref[...]
    row = lax.broadcasted_iota(jnp.int32, a.shape, 0)
    shifted = jnp.where(row == 0, prev_ref[...], pltpu.roll(a, 1, 0))
    prev_ref[...] = a[TB - 1:TB, :]
    xs = a + (shifted - a) * mu_ref[...]

    r = xs[:, 0:W]
    k = xs[:, W:2 * W]
    v = xs[:, 2 * W:3 * W]
    w_lo = xs[:, 3 * W:3 * W + LORA_PAD]
    a_lo = xs[:, 3 * W + LORA_PAD:3 * W + 2 * LORA_PAD]
    g_lo = xs[:, 3 * W + 2 * LORA_PAD:]

    z = w0_ref[...] + _dot(jnp.tanh(w_lo).astype(BF16), w2_ref[...])
    w = -jax.nn.softplus(-z) - 0.5
    lw = -jnp.exp(w)
    aa = jax.nn.sigmoid(a0_ref[...] + _dot(a_lo.astype(BF16), a2_ref[...]))
    g = _dot(jax.nn.sigmoid(g_lo).astype(BF16), g2_ref[...])

    seg = seg_ref[...]
    kk = k * kks_ref[...]
    ss = _dot((kk * kk).astype(BF16), seg)
    kk = kk * lax.rsqrt(jnp.maximum(ss, 1e-24))
    km = k * (1.0 + (aa - 1.0) * kas_ref[...])
    beta = kk * aa

    tri = tri_ref[...]
    lw_hi = lw.astype(BF16)
    lw_r = lw - lw_hi.astype(F32)
    lw_mid = lw_r.astype(BF16)
    lw_lo = (lw_r - lw_mid.astype(F32)).astype(BF16)
    cum = _dot(tri, lw_hi) + (_dot(tri, lw_mid) + _dot(tri, lw_lo))
    cum3 = cum.reshape(TB // C, C, W)
    cum_c = jnp.broadcast_to(cum3[:, C - 1:C, :], cum3.shape).reshape(TB, W)

    e_in = jnp.exp(cum)
    e_neg = jnp.exp(-cum)
    e_end = jnp.exp(cum_c - cum)
    at_ref[...] = (-kk * jnp.exp(cum - lw)).astype(BF16)
    rt_ref[...] = (r * e_in).astype(BF16)
    bt_ref[...] = (beta * e_neg).astype(BF16)
    kt_ref[...] = (km * e_neg).astype(BF16)
    bp_ref[...] = (beta * e_end).astype(BF16)
    kp_ref[...] = (km * e_end).astype(BF16)
    v_ref[...] = v.astype(BF16)
    gc_ref[...] = jnp.exp(cum_c)

    ri = lax.broadcasted_iota(jnp.int32, (C, C), 0)
    ci = lax.broadcasted_iota(jnp.int32, (C, C), 1)
    strict = ci < ri
    incl = ci <= ri
    eye = ci == ri

    def chunk(c, carry):
        NC = 2
        rows = [pl.ds(pl.multiple_of((c * NC + ci) * C, C), C) for ci in range(NC)]
        HS = range(NC * RWKV_HEADS)
        rw = [rows[i // RWKV_HEADS] for i in HS]
        sl = [slice((i % RWKV_HEADS) * N, (i % RWKV_HEADS + 1) * N) for i in HS]
        cat = lambda a, b: jnp.concatenate([a, b], axis=0)
        at = [at_ref[rw[i], sl[i]] for i in HS]
        rt = [rt_ref[rw[i], sl[i]] for i in HS]
        bt = [bt_ref[rw[i], sl[i]] for i in HS]
        kt = [kt_ref[rw[i], sl[i]] for i in HS]
        bp = [bp_ref[rw[i], sl[i]] for i in HS]
        kp = [kp_ref[rw[i], sl[i]] for i in HS]
        vh = [v_ref[rw[i], sl[i]] for i in HS]
        ar = [cat(at[h], rt[h]) for h in HS]
        xb = [_nt(ar[h], bt[h]) for h in HS]
        xk = [_nt(ar[h], kt[h]) for h in HS]
        a_ab = [jnp.where(strict, xb[h][:C], 0.0) for h in HS]
        a_rb = [jnp.where(incl, xb[h][C:], 0.0).astype(BF16) for h in HS]
        a_ak = [jnp.where(strict, xk[h][:C], 0.0).astype(BF16) for h in HS]
        a_rk = [jnp.where(incl, xk[h][C:], 0.0).astype(BF16) for h in HS]
        ident = jnp.where(eye, 1.0, 0.0)
        t_inv = [ident + a_ab[h] for h in HS]
        ap = a_ab
        for _ in range(5):
            apb = [ap[h].astype(BF16) for h in HS]
            ap = [_dot(apb[h], apb[h]) for h in HS]
            t_inv = [t_inv[h] + _dot(t_inv[h].astype(BF16), ap[h].astype(BF16)) for h in HS]
        tb = [t_inv[h].astype(BF16) for h in HS]
        wv = [_dot(a_ak[h], vh[h]).astype(BF16) for h in HS]
        pb = [_dot(tb[h], at[h]).astype(BF16) for h in HS]
        u0b = [_dot(tb[h], wv[h]).astype(BF16) for h in HS]
        qm = [rt[h].astype(F32) + _dot(a_rb[h], pb[h]) for h in HS]
        y0 = [_dot(a_rb[h], u0b[h]) + _dot(a_rk[h], vh[h]) for h in HS]
        gm = [jnp.where(eye, gc_ref[rw[h], sl[h]][0:1, :], 0.0) + _tn(bp[h], pb[h]) for h in HS]
        hadd = [_tn(cat(bp[h], kp[h]), cat(u0b[h], vh[h])) for h in HS]
        qg = [cat(qm[h], gm[h]).astype(BF16) for h in HS]
        for i in HS:
            hd = i % RWKV_HEADS
            z = _dot(qg[i], h_ref[hd].astype(BF16))
            y_ref[rw[i], sl[i]] = z[:C] + y0[i]
            h_ref[hd] = z[C:] + hadd[i]
        return carry

    lax.fori_loop(0, TB // (2 * C), chunk, 0)

    y = y_ref[...]
    segm = seg_ref[...]
    mu = _dot(y.astype(BF16), segm) * (1.0 / N)
    yc = y - mu
    var = _dot((yc * yc).astype(BF16), segm) * (1.0 / N)
    yn = yc * lax.rsqrt(var + RWKV_GN_EPS) * gng_ref[...] + gnb_ref[...]
    bonus = _dot((r * km * rk_ref[...]).astype(BF16), segm) * v
    o_ref[...] = ((yn + bonus) * g).astype(o_ref.dtype)


def _rwkv(a_proj, mu, w2, w0, a2, a0, g2, kks, kas, rk, gng, gnb, tb):
    T = a_proj.shape[0]
    tb = min(tb, T)
    W = RWKV_WIDTH
    hid = np.arange(W) // RWKV_HEAD
    seg = jnp.asarray(hid[:, None] == hid[None, :], dtype=BF16)
    t_idx = np.arange(tb)
    tri = jnp.asarray((t_idx[:, None] // RWKV_CHUNK == t_idx[None, :] // RWKV_CHUNK)
                      & (t_idx[None, :] <= t_idx[:, None]), dtype=BF16)
    full = lambda shp: pl.BlockSpec(shp, lambda i: (0,) * len(shp))
    row = lambda n: full((1, n))
    bf = lambda: pltpu.VMEM((tb, W), BF16)
    return pl.pallas_call(
        _rwkv_kernel,
        grid=(T // tb,),
        in_specs=[pl.BlockSpec((tb, RWKV_IN_PAD), lambda i: (i, 0)),
                  row(RWKV_IN_PAD), full((LORA_PAD, W)), row(W), full((LORA_PAD, W)), row(W),
                  full((RWKV_GATE_LORA, W)), row(W), row(W), row(W), row(W), row(W),
                  full((W, W)), full((tb, tb))],
        out_specs=pl.BlockSpec((tb, W), lambda i: (i, 0)),
        out_shape=jax.ShapeDtypeStruct((T, W), BF16),
        scratch_shapes=[pltpu.VMEM((1, RWKV_IN_PAD), F32),
                        pltpu.VMEM((RWKV_HEADS, RWKV_HEAD, RWKV_HEAD), F32),
                        bf(), bf(), bf(), bf(), bf(), bf(), bf(),
                        pltpu.VMEM((tb, W), F32), pltpu.VMEM((tb, W), F32)],
        compiler_params=_cparams(("arbitrary",)),
        name="rwkv7",
    )(a_proj, mu, w2, w0, a2, a0, g2, kks, kas, rk, gng, gnb, seg, tri)


def _diff_kernel(q_ref, k_ref, vt_ref, lq1_ref, lk1_ref, lq2_ref, lk2_ref, sg_ref, o_ref,
                 m_ref, l_ref, acc_ref, s_ref, p_ref, a_ref, *, lambda_init):
    tq = q_ref.shape[0]
    nkb, _, tk = vt_ref.shape
    i = pl.program_id(1)
    q = q_ref[...]
    lane = lax.broadcasted_iota(jnp.int32, q.shape, 1)
    qm = (jnp.where(lane < DIFF_HEAD, q, jnp.zeros_like(q)),
          jnp.where(lane >= DIFF_HEAD, q, jnp.zeros_like(q)))
    m_ref[...] = jnp.full(m_ref.shape, -jnp.inf, F32)
    l_ref[...] = jnp.zeros_like(l_ref)
    acc_ref[...] = jnp.zeros_like(acc_ref)
    p_ref[1] = jnp.zeros(p_ref.shape[1:], BF16)
    a_ref[1] = jnp.ones(a_ref.shape[1:], F32)

    def scores(j, slot):
        jc = jnp.minimum(j, nkb - 1)
        kj = k_ref[pl.ds(pl.multiple_of(jc * tk, tk), tk), :]
        for mi in range(2):
            s_ref[slot, mi] = _nt(kj, qm[mi])

    def values(j, slot):
        vj = vt_ref[jnp.clip(j, 0, nkb - 1)]
        for mi in range(2):
            acc_ref[mi] = a_ref[slot, mi] * acc_ref[mi] + _dot(vj, p_ref[slot, mi])

    def softmax(j, slot, masked):
        for mi in range(2):
            s = s_ref[slot, mi]
            if masked:
                d = (lax.broadcasted_iota(jnp.int32, (tk, tq), 0)
                     - lax.broadcasted_iota(jnp.int32, (tk, tq), 1))
                s = jnp.where(d <= i * tq - j * tk, s, -jnp.inf)
            m_old = m_ref[mi]
            m_new = jnp.maximum(m_old, jnp.max(s, axis=0, keepdims=True))
            alpha = jnp.exp2(m_old - m_new)
            p = jnp.exp2(s - m_new)
            l_ref[mi] = alpha * l_ref[mi] + jnp.sum(p, axis=0, keepdims=True)
            m_ref[mi] = m_new
            a_ref[slot, mi] = alpha
            p_ref[slot, mi] = p.astype(BF16)

    def half(j, cur, masked):
        nxt = 1 - cur
        scores(j + 1, nxt)
        values(j - 1, nxt)
        softmax(j, cur, masked)

    def pair(jj, carry):
        half(2 * jj, 0, False)
        half(2 * jj + 1, 1, False)
        return carry

    scores(0, 0)
    n_pairs = ((i + 1) * (tq // tk) + 1) // 2
    lax.fori_loop(0, n_pairs - 1, pair, 0)
    j_last = 2 * (n_pairs - 1)
    half(j_last, 0, True)
    half(j_last + 1, 1, True)
    values(j_last + 1, 1)

    lam = (jnp.exp(jnp.sum(lq1_ref[...] * lk1_ref[...])) - jnp.exp(jnp.sum(lq2_ref[...] * lk2_ref[...]))
           + lambda_init)
    o = acc_ref[0] / l_ref[0] - lam * (acc_ref[1] / l_ref[1])
    o = o * lax.rsqrt(jnp.mean(o * o, axis=0, keepdims=True) + LN_EPS) * (1.0 - lambda_init)
    o = o * sg_ref[...]
    o_ref[...] = o.T.astype(o_ref.dtype)


def _diff_attention(qk, vt, lq1, lk1, lq2, lk2, subln_g, lambda_init, tq):
    T = qk.shape[0]
    tk = vt.shape[2]
    assert tq % tk == 0 and T % tq == 0
    H, E = DIFF_HEADS, 2 * DIFF_HEAD
    vec = lambda: pl.BlockSpec((1, DIFF_HEAD), lambda h, i: (0, 0))
    return pl.pallas_call(
        functools.partial(_diff_kernel, lambda_init=lambda_init),
        grid=(H, T // tq),
        in_specs=[pl.BlockSpec((tq, E), lambda h, i: (i, h)),
                  pl.BlockSpec((T, E), lambda h, i: (0, H + h)),
                  pl.BlockSpec((T // tk, E, tk), lambda h, i: (0, h, 0)),
                  vec(), vec(), vec(), vec(),
                  pl.BlockSpec((E, 1), lambda h, i: (0, 0))],
        out_specs=pl.BlockSpec((tq, E), lambda h, i: (i, h)),
        out_shape=jax.ShapeDtypeStruct((T, DIFF_V), BF16),
        scratch_shapes=[pltpu.VMEM((2, 1, tq), F32), pltpu.VMEM((2, 1, tq), F32),
                        pltpu.VMEM((2, E, tq), F32), pltpu.VMEM((2, 2, tk, tq), F32),
                        pltpu.VMEM((2, 2, tk, tq), BF16), pltpu.VMEM((2, 2, 1, tq), F32)],
        compiler_params=_cparams(("arbitrary", "arbitrary")),
        name="diff_attn",
    )(qk, qk, vt, lq1, lk1, lq2, lk2, subln_g)


def _gla_kernel(c_ref, g2_ref, gb_ref, gng_ref, tri_ref, segv_ref, o_ref,
                st_ref, oacc_ref):
    TB = c_ref.shape[0]
    KW, VW, S = GLA_K_WIDTH, GLA_V_WIDTH, GLA_SUB
    DK, DV = GLA_KEY, GLA_VALUE

    @pl.when(pl.program_id(0) == 0)
    def _():
        st_ref[...] = jnp.zeros_like(st_ref)

    g_lo = c_ref[:, 2 * KW + 2 * VW:]
    gate = _dot(g_lo, g2_ref[...], precision=HIGHEST) + gb_ref[...]
    log_a = jax.nn.log_sigmoid(gate) * (1.0 / GLA_TAU)
    b = _dot(tri_ref[...], log_a, precision=HIGHEST)
    NS = TB // S
    b3 = b.reshape(NS, S, KW)
    q3 = c_ref[:, 0:KW].reshape(NS, S, KW)
    k3 = c_ref[:, KW:2 * KW].reshape(NS, S, KW)
    v = c_ref[:, 2 * KW:2 * KW + VW]
    v3 = v.reshape(NS, S, VW)
    bl3 = b3[:, S - 1:S, :]
    qb = (q3 * jnp.exp(b3)).astype(BF16).reshape(TB, KW)
    kb = (k3 * jnp.exp(bl3 - b3)).astype(BF16).reshape(TB, KW)
    dec = jnp.exp(bl3)
    vb = v.astype(BF16)

    ri = lax.broadcasted_iota(jnp.int32, (NS, S, 1), 1)
    segv = segv_ref[...]
    acc = jnp.zeros((NS, S, VW), F32)
    for j in range(S):
        e = jnp.exp(jnp.minimum(b3 - b3[:, j:j + 1, :], 0.0))
        t = (q3 * e * k3[:, j:j + 1, :]).astype(BF16).reshape(TB, KW)
        att = _dot(t, segv).reshape(NS, S, VW)
        acc = acc + jnp.where(ri >= j, att, 0.0) * v3[:, j:j + 1, :]
    oacc_ref[...] = acc.reshape(TB, VW)

    for h in range(GLA_HEADS):
        lk = slice(h * DK, (h + 1) * DK)
        lv = slice(h * DV, (h + 1) * DV)
        kv = [_tn(vb[s * S:(s + 1) * S, lv], kb[s * S:(s + 1) * S, lk]) for s in range(NS)]
        st = st_ref[h]
        for s in range(NS):
            rows = slice(s * S, (s + 1) * S)
            oacc_ref[rows, lv] += _nt(qb[rows, lk], st.astype(BF16))
            st = st * dec[s][:, lk] + kv[s]
        st_ref[h] = st

    o = oacc_ref[...]
    rgate = c_ref[:, 2 * KW + VW:2 * KW + 2 * VW]
    for h in range(GLA_HEADS):
        lv = slice(h * DV, (h + 1) * DV)
        oh = o[:, lv]
        oh = oh * lax.rsqrt(jnp.mean(oh * oh, axis=1, keepdims=True) + LN_EPS) * gng_ref[:, lv]
        o_ref[:, lv] = (oh * jax.nn.silu(rgate[:, lv])).astype(o_ref.dtype)


def _gla(c_proj, g2, gb, gng, tb):
    T = c_proj.shape[0]
    tb = min(tb, T)
    KW, VW = GLA_K_WIDTH, GLA_V_WIDTH
    t_idx = np.arange(tb)
    tri = jnp.asarray((t_idx[:, None] // GLA_SUB == t_idx[None, :] // GLA_SUB)
                      & (t_idx[None, :] <= t_idx[:, None]), dtype=F32)
    segv = jnp.asarray((np.arange(KW) // GLA_KEY)[:, None] == (np.arange(VW) // GLA_VALUE)[None, :], dtype=BF16)
    full = lambda shp: pl.BlockSpec(shp, lambda i: (0,) * len(shp))
    return pl.pallas_call(
        _gla_kernel,
        grid=(T // tb,),
        in_specs=[pl.BlockSpec((tb, GLA_IN_PAD), lambda i: (i, 0)),
                  full((LANE, KW)), full((1, KW)), full((1, VW)), full((tb, tb)), full((KW, VW))],
        out_specs=pl.BlockSpec((tb, VW), lambda i: (i, 0)),
        out_shape=jax.ShapeDtypeStruct((T, VW), BF16),
        scratch_shapes=[pltpu.VMEM((GLA_HEADS, GLA_VALUE, GLA_KEY), F32),
                        pltpu.VMEM((tb, VW), F32)],
        compiler_params=_cparams(("arbitrary",)),
        name="gla",
    )(c_proj, g2, gb, gng, tri, segv)


def _merge_kernel(oa_ref, ob_ref, oc_ref, g_ref, pa_ref, pb_ref, pc_ref, o_ref):
    D = D_MODEL
    m = g_ref[:, 0:D].astype(F32) * _dot(oa_ref[...], pa_ref[...])
    m = m + g_ref[:, D:2 * D].astype(F32) * _dot(ob_ref[...], pb_ref[...])
    m = m + g_ref[:, 2 * D:3 * D].astype(F32) * _dot(oc_ref[...], pc_ref[...])
    o_ref[...] = m.astype(o_ref.dtype)


def _merge(oa, ob, oc, gates, pa, pb, pc, tm):
    T = oa.shape[0]
    tm = min(tm, T)
    D = D_MODEL
    rowblk = lambda n: pl.BlockSpec((tm, n), lambda i: (i, 0))
    full = lambda shp: pl.BlockSpec(shp, lambda i: (0, 0))
    return pl.pallas_call(
        _merge_kernel,
        grid=(T // tm,),
        in_specs=[rowblk(RWKV_WIDTH), rowblk(DIFF_V), rowblk(GLA_V_WIDTH), rowblk(3 * D),
                  full((RWKV_WIDTH, D)), full((DIFF_V, D)), full((GLA_V_WIDTH, D))],
        out_specs=rowblk(D),
        out_shape=jax.ShapeDtypeStruct((T, D), BF16),
        compiler_params=_cparams(("arbitrary",)),
        name="merge",
    )(oa, ob, oc, gates, pa, pb, pc)


def _layer_norm(z, g, b):
    mu = jnp.mean(z, axis=1, keepdims=True)
    zc = z - mu
    var = jnp.mean(zc * zc, axis=1, keepdims=True)
    return zc * lax.rsqrt(var + LN_EPS) * g + b


def _mm_ln_kernel(a_ref, w_ref, res_ref, g_ref, b_ref, wrh_ref, wrl_ref, br_ref, o_ref):
    z = DEEPNORM_ALPHA * res_ref[...] + _dot(a_ref[...], w_ref[...])
    out = _layer_norm(z, g_ref[...], b_ref[...])
    o_ref[:, 0:D_MODEL] = out
    o_ref[:, D_MODEL:] = _route(out, wrh_ref[...], wrl_ref[...], br_ref[...])


def _mm_ln(a, w, res, g, b, wr, br, tm):
    wr_hi = wr.astype(BF16)
    wr_lo = (wr - wr_hi.astype(F32)).astype(BF16)
    T, K = a.shape
    D = D_MODEL
    tm = min(tm, T)
    rowblk = lambda n: pl.BlockSpec((tm, n), lambda i: (i, 0))
    full = lambda shp: pl.BlockSpec(shp, lambda i: (0, 0))
    return pl.pallas_call(
        _mm_ln_kernel,
        grid=(T // tm,),
        in_specs=[rowblk(K), full((K, D)), rowblk(D), full((1, D)), full((1, D)), full((D, LANE)), full((D, LANE)),
                  full((1, LANE))],
        out_specs=rowblk(D + LANE),
        out_shape=jax.ShapeDtypeStruct((T, D + LANE), F32),
        compiler_params=_cparams(("arbitrary",)),
        name="wo_ln_route",
    )(a, w, res, g, b, wr_hi, wr_lo, br)


def _route(x, w_hi, w_lo, b):
    G, E = MOE_GROUPS, MOE_EXPERTS
    x_hi = x.astype(BF16)
    x_lo = (x - x_hi.astype(F32)).astype(BF16)
    logits = _dot(x_hi, w_hi) + (_dot(x_lo, w_hi) + _dot(x_hi, w_lo)) + b
    lane = lax.broadcasted_iota(jnp.int32, logits.shape, 1)
    neg = -jnp.inf
    big = jnp.int32(1 << 20)
    is_g = lane < G
    lg = jnp.where(is_g, logits, neg)
    gmax = jnp.max(lg, axis=1, keepdims=True)
    gidx = jnp.min(jnp.where(is_g & (lg == gmax), lane, big), axis=1, keepdims=True)
    g_p = 1.0 / jnp.sum(jnp.exp(lg - gmax), axis=1, keepdims=True)
    lo = G + gidx * E
    in_grp = (lane >= lo) & (lane < lo + E)
    le = jnp.where(in_grp, logits, neg)
    m1 = jnp.max(le, axis=1, keepdims=True)
    i1 = jnp.min(jnp.where(in_grp & (le == m1), lane, big), axis=1, keepdims=True)
    le2 = jnp.where(lane == i1, neg, le)
    m2 = jnp.max(le2, axis=1, keepdims=True)
    i2 = jnp.min(jnp.where(in_grp & (le2 == m2), lane, big), axis=1, keepdims=True)
    e2 = jnp.exp(m2 - m1)
    p1 = 1.0 / (1.0 + e2)
    p2 = e2 / (1.0 + e2)
    comb = jnp.where(lane == i1, g_p * p1, 0.0) + jnp.where(lane == i2, g_p * p2, 0.0)
    return jnp.where(lane == 0, gidx.astype(F32), comb)


def _moe_tables(gid_f, tm):
    T = gid_f.shape[0]
    G = MOE_GROUPS
    nt = T // tm + G
    gid = gid_f.astype(jnp.int32)
    order = jnp.argsort(gid).astype(jnp.int32)
    counts = jnp.sum((gid[:, None] == jnp.arange(G)[None, :]).astype(jnp.int32), axis=0)
    starts = jnp.cumsum(counts) - counts
    ntile = (counts + tm - 1) // tm
    tstart = jnp.cumsum(ntile) - ntile
    n = jnp.arange(nt, dtype=jnp.int32)
    tg = jnp.sum((n[:, None] >= tstart[None, 1:]).astype(jnp.int32), axis=1)
    k = n - tstart[tg]
    row0 = starts[tg] + k * tm
    nvalid = jnp.clip(counts[tg] - k * tm, 0, tm).astype(jnp.int32)
    rows = jnp.clip(row0[:, None] + jnp.arange(tm, dtype=jnp.int32)[None, :], 0, T - 1)
    idx = jnp.take(order, rows).reshape(nt, 1, tm)
    return tg.astype(jnp.int32), nvalid, idx


def _moe_kernel(tg_ref, nv_ref, idx_ref, idxn_ref, x_hbm, wg_ref, wu_ref, wd_ref, g_ref, b_ref, y_hbm,
                xs_ref, xb_ref, h_ref, out_ref, pend_ref, gx_sem, sc_sem):
    n = pl.program_id(0)
    e = pl.program_id(1)
    nt = pl.num_programs(0)
    nv = nv_ref[n]
    slot = n % 2
    other = 1 - slot
    tm = out_ref.shape[0]
    part = tm // MOE_EXPERTS
    F = MOE_HIDDEN

    def start_row(idx, dst_slot, r, prio):
        tok = idx[0, 0, r]
        pltpu.make_async_copy(x_hbm.at[pl.ds(tok, 1)], xs_ref.at[dst_slot, pl.ds(r, 1)],
                              gx_sem.at[dst_slot]).start(priority=prio)

    def start_next_part():
        for r in range(part):
            start_row(idxn_ref, other, e * part + r, r % 2)

    def wait_rows(s):
        pltpu.make_async_copy(x_hbm.at[pl.ds(0, tm)], xs_ref.at[s], gx_sem.at[s]).wait()

    def drain():
        @pl.when(pend_ref[0] == tm)
        def _():
            pltpu.make_async_copy(out_ref, y_hbm.at[pl.ds(0, tm)], sc_sem.at[0]).wait()

        @pl.when(pend_ref[0] < tm)
        def _():
            def body(r, c):
                pltpu.make_async_copy(out_ref.at[pl.ds(0, 1)], y_hbm.at[pl.ds(0, 1)], sc_sem.at[0]).wait()
                return c
            lax.fori_loop(0, pend_ref[0], body, 0)
        pend_ref[0] = 0

    @pl.when((n == 0) & (e == 0))
    def _():
        pend_ref[0] = 0

        def body(r, c):
            start_row(idx_ref, 0, 2 * r, 0)
            start_row(idx_ref, 0, 2 * r + 1, 1)
            return c
        lax.fori_loop(0, tm // 2, body, 0)

    @pl.when(e == 0)
    def _():
        wait_rows(slot)
        xb_ref[...] = xs_ref[slot, :, 0:D_MODEL].astype(BF16)

    @pl.when(nv > 0)
    def _():
        start_next_part()
        xb = xb_ref[...]
        hg = _dot(xb, wg_ref[0])
        hu = _dot(xb, wu_ref[0])
        comb = xs_ref[slot, :, D_MODEL:]
        lane = lax.broadcasted_iota(jnp.int32, comb.shape, 1)
        col = MOE_GROUPS + tg_ref[n] * MOE_EXPERTS + e
        cw = jnp.sum(jnp.where(lane == col, comb, 0.0), axis=1, keepdims=True)
        h_ref[e] = (jax.nn.silu(hg) * hu * cw).astype(BF16)

    @pl.when(nv == 0)
    def _():
        start_next_part()

    @pl.when(e == MOE_EXPERTS - 1)
    def _():
        drain()

        @pl.when(nv > 0)
        def _():
            y = _dot(h_ref[0], wd_ref[0, 0:F, :])
            for j in range(1, MOE_EXPERTS):
                y = y + _dot(h_ref[j], wd_ref[0, j * F:(j + 1) * F, :])
            z = DEEPNORM_ALPHA * xs_ref[slot, :, 0:D_MODEL] + y
            out_ref[...] = _layer_norm(z, g_ref[...], b_ref[...])

        def put(r, prio):
            tok = idx_ref[0, 0, r]
            pltpu.make_async_copy(out_ref.at[pl.ds(r, 1)], y_hbm.at[pl.ds(tok, 1)],
                                  sc_sem.at[0]).start(priority=prio)

        def body(r, c):
            put(2 * r, 0)
            put(2 * r + 1, 1)
            return c
        lax.fori_loop(0, nv // 2, body, 0)

        @pl.when(nv % 2 == 1)
        def _():
            put(nv - 1, 0)
        pend_ref[0] = nv

        @pl.when(n == nt - 1)
        def _():
            drain()
            wait_rows(other)


def _moe(xa, wg, wu, wd, l, g, b, tm):
    T = xa.shape[0]
    D = D_MODEL
    F = MOE_HIDDEN
    tm = min(tm, T)
    tg, nvalid, idx = _moe_tables(xa[:, D], tm)
    nt = tg.shape[0]
    wspec = lambda shp: pl.BlockSpec(shp, lambda n, e, tg_r, nv_r: (l * N_EXPERTS + tg_r[n] * MOE_EXPERTS + e, 0, 0))
    ispec = lambda f: pl.BlockSpec((1, 1, tm), f, memory_space=pltpu.SMEM)
    vec = pl.BlockSpec((1, D), lambda n, e, tg_r, nv_r: (0, 0))
    grid_spec = pltpu.PrefetchScalarGridSpec(
        num_scalar_prefetch=2,
        grid=(nt, MOE_EXPERTS),
        in_specs=[ispec(lambda n, e, tg_r, nv_r: (n, 0, 0)),
                  ispec(lambda n, e, tg_r, nv_r: (jnp.minimum(n + 1, nt - 1), 0, 0)),
                  pl.BlockSpec(memory_space=pl.ANY),
                  wspec((1, D, F)), wspec((1, D, F)),
                  pl.BlockSpec((1, MOE_EXPERTS * F, D), lambda n, e, tg_r, nv_r: (l * MOE_GROUPS + tg_r[n], 0, 0),
                               pipeline_mode=pl.Buffered(1)),
                  vec, vec],
        out_specs=pl.BlockSpec(memory_space=pl.ANY),
        scratch_shapes=[pltpu.VMEM((2, tm, D + LANE), F32),
                        pltpu.VMEM((tm, D), BF16), pltpu.VMEM((MOE_EXPERTS, tm, F), BF16),
                        pltpu.VMEM((tm, D), F32), pltpu.SMEM((1,), jnp.int32),
                        pltpu.SemaphoreType.DMA((2,)), pltpu.SemaphoreType.DMA((1,))],
    )
    return pl.pallas_call(
        _moe_kernel,
        grid_spec=grid_spec,
        out_shape=jax.ShapeDtypeStruct((T, D), F32),
        compiler_params=_cparams(("arbitrary", "arbitrary")),
        name="moe_experts_ln",
    )(tg, nvalid, idx, idx, xa, wg, wu, wd, g, b)


def _pad_cols(w, n):
    return jnp.pad(w, ((0, 0), (0, n - w.shape[1])))


def _prep_weights(p):
    W = RWKV_WIDTH
    KW, VW = GLA_K_WIDTH, GLA_V_WIDTH
    wt = jnp.swapaxes(p['w_in'], 1, 2)
    o_b = RWKV_IN
    o_c = o_b + DIFF_IN
    o_g = o_c + GLA_IN
    lo1 = 3 * W + RWKV_DECAY_LORA
    lo2 = lo1 + RWKV_AAA_LORA
    padr = lambda w, n: jnp.pad(w, ((0, 0), (0, n - w.shape[1]), (0, 0)))
    w_a = jnp.concatenate([padr(wt[:, :lo1], 3 * W + LORA_PAD), padr(wt[:, lo1:lo2], LORA_PAD),
                           wt[:, lo2:o_b]], axis=1).astype(BF16)
    qscale = DIFF_HEAD ** -0.5 * math.log2(math.e)
    w_qk = jnp.concatenate([wt[:, o_b:o_b + DIFF_QK] * qscale, wt[:, o_b + DIFF_QK:o_b + 2 * DIFF_QK]],
                           axis=1).astype(BF16)
    w_vt = wt[:, o_b + 2 * DIFF_QK:o_c].astype(BF16)
    c0 = o_c
    w_c = jnp.concatenate([wt[:, c0:c0 + KW] * (GLA_KEY ** -0.5), wt[:, c0 + KW:c0 + 2 * KW + VW],
                           wt[:, c0 + 2 * KW + VW + GLA_GATE_LORA:o_g],
                           padr(wt[:, c0 + 2 * KW + VW:c0 + 2 * KW + VW + GLA_GATE_LORA], LANE)],
                          axis=1).astype(BF16)
    w_g = wt[:, o_g:].astype(BF16)
    w_in = p['w_in']
    L = w_in.shape[0]
    F = MOE_HIDDEN
    return dict(
        w_a=w_a, w_qk=w_qk, w_vt=w_vt, w_c=w_c, w_g=w_g,
        wg=p['moe_w_gate'].astype(BF16).reshape(L * N_EXPERTS, D_MODEL, F),
        wu=p['moe_w_up'].astype(BF16).reshape(L * N_EXPERTS, D_MODEL, F),
        wd=p['moe_w_down'].astype(BF16).reshape(L * MOE_GROUPS, MOE_EXPERTS * F, D_MODEL))


def _layer(x, xb, l, p, wts, rope_tabs):
    T = x.shape[0]
    W = RWKV_WIDTH
    lambda_init = 0.8 - 0.6 * math.exp(-0.3 * l)
    lo1 = 3 * W + RWKV_DECAY_LORA
    lo2 = lo1 + RWKV_AAA_LORA
    mu = p['rwkv_mu'][l]
    mu_a = jnp.concatenate([jnp.pad(mu[:lo1], (0, LORA_PAD - RWKV_DECAY_LORA)),
                            jnp.pad(mu[lo1:lo2], (0, LORA_PAD - RWKV_AAA_LORA)), mu[lo2:]])[None, :]

    proj_a = _matmul(xb, wts['w_a'], l, F32, 1024, 1024)
    qk = _matmul_rope(xb, wts['w_qk'], l, rope_tabs, 1024, 1024)
    tk = min(512, T)
    vt = _matmul_vt(wts['w_vt'], l, xb, tk)
    proj_c = _matmul(xb, wts['w_c'], l, F32, 512, GLA_IN_PAD)
    gates = _matmul(xb, wts['w_g'], l, BF16, 1024, 1024, gate=True)

    padr = lambda w: jnp.pad(w, ((0, LORA_PAD - w.shape[0]), (0, 0)))
    r2 = lambda v: v.reshape(1, -1)
    o_a = _rwkv(proj_a, mu_a, padr(p['rwkv_w2'][l]).astype(BF16), r2(p['rwkv_w0'][l]),
                padr(p['rwkv_a2'][l]).astype(BF16), r2(p['rwkv_a0'][l]), p['rwkv_g2'][l].astype(BF16),
                r2(p['rwkv_kk_scale'][l]), r2(p['rwkv_ka_scale'][l]), r2(p['rwkv_rk'][l]),
                r2(p['rwkv_gn_g'][l]), r2(p['rwkv_gn_b'][l]), 256)
    o_bb = _diff_attention(qk, vt, r2(p['diff_lq1'][l]), r2(p['diff_lk1'][l]), r2(p['diff_lq2'][l]),
                           r2(p['diff_lk2'][l]), p['diff_subln_g'][l].reshape(-1, 1), lambda_init, min(2 * tk, T))
    g2p = jnp.pad(p['gla_g2'][l], ((0, LANE - GLA_GATE_LORA), (0, 0)))
    o_c = _gla(proj_c, g2p, r2(p['gla_gb'][l]), r2(p['gla_gn_g'][l]), 256)

    merged = _merge(o_a, o_bb, o_c, gates, p['proj_a'][l].astype(BF16), p['proj_b'][l].astype(BF16),
                    p['proj_c'][l].astype(BF16), 256)
    wr = _pad_cols(jnp.concatenate([p['router_group_w'][l], p['router_expert_w'][l]], axis=1), LANE)
    br = _pad_cols(jnp.concatenate([p['router_group_b'][l], p['router_expert_b'][l]])[None, :], LANE)
    xa = _mm_ln(merged, p['w_o'][l].astype(BF16), x, r2(p['ln1_g'][l]), r2(p['ln1_b'][l]), wr, br, 512)
    return _moe(xa, wts['wg'], wts['wu'], wts['wd'], l, r2(p['ln2_g'][l]), r2(p['ln2_b'][l]), MOE_TILE)


def kernel(x, w_in, rwkv_mu, rwkv_w2, rwkv_w0, rwkv_a2, rwkv_a0, rwkv_g2, rwkv_kk_scale, rwkv_ka_scale, rwkv_rk, rwkv_gn_g, rwkv_gn_b, diff_lq1, diff_lk1, diff_lq2, diff_lk2, diff_subln_g, gla_g2, gla_gb, gla_gn_g, proj_a, proj_b, proj_c, w_o, ln1_g, ln1_b, router_group_w, router_group_b, router_expert_w, router_expert_b, moe_w_gate, moe_w_up, moe_w_down, ln2_g, ln2_b):
    p = dict(w_in=w_in, rwkv_mu=rwkv_mu, rwkv_w2=rwkv_w2, rwkv_w0=rwkv_w0, rwkv_a2=rwkv_a2, rwkv_a0=rwkv_a0,
             rwkv_g2=rwkv_g2, rwkv_kk_scale=rwkv_kk_scale, rwkv_ka_scale=rwkv_ka_scale, rwkv_rk=rwkv_rk,
             rwkv_gn_g=rwkv_gn_g, rwkv_gn_b=rwkv_gn_b, diff_lq1=diff_lq1, diff_lk1=diff_lk1, diff_lq2=diff_lq2,
             diff_lk2=diff_lk2, diff_subln_g=diff_subln_g, gla_g2=gla_g2, gla_gb=gla_gb, gla_gn_g=gla_gn_g,
             proj_a=proj_a, proj_b=proj_b, proj_c=proj_c, w_o=w_o, ln1_g=ln1_g, ln1_b=ln1_b,
             router_group_w=router_group_w, router_group_b=router_group_b, router_expert_w=router_expert_w,
             router_expert_b=router_expert_b, moe_w_gate=moe_w_gate, moe_w_up=moe_w_up, moe_w_down=moe_w_down,
             ln2_g=ln2_g, ln2_b=ln2_b)
    B, T, D = x.shape
    assert B == 1 and D == D_MODEL
    xf = x.reshape(T, D)
    xb = xf.astype(BF16)
    tabs = _rope_tables(T)
    wts = _prep_weights(p)
    for l in range(DEPTH):
        xf = _layer(xf, xb, l, p, wts, tabs)
        xb = xf.astype(BF16)
    return xf.reshape(B, T, D)
```

```python
import functools
import math

import numpy as np
import jax
import jax.numpy as jnp
from jax import lax
from jax.experimental import pallas as pl
from jax.experimental.pallas import tpu as pltpu

F32 = jnp.float32
BF16 = jnp.bfloat16
HIGHEST = lax.Precision.HIGHEST

D_MODEL = 2048
DEPTH = 2
DEEPNORM_ALPHA = (2 * DEPTH) ** 0.25
LN_EPS = 1e-5

RWKV_HEADS = 8
RWKV_HEAD = 64
RWKV_WIDTH = RWKV_HEADS * RWKV_HEAD
RWKV_DECAY_LORA = 96
RWKV_AAA_LORA = 96
RWKV_GATE_LORA = 256
RWKV_GN_EPS = 64e-5
RWKV_IN = 3 * RWKV_WIDTH + RWKV_DECAY_LORA + RWKV_AAA_LORA + RWKV_GATE_LORA
RWKV_CHUNK = 64
LORA_PAD = 128
RWKV_IN_PAD = 3 * RWKV_WIDTH + 2 * LORA_PAD + RWKV_GATE_LORA

DIFF_HEADS = 8
DIFF_HEAD = 64
DIFF_QK = DIFF_HEADS * 2 * DIFF_HEAD
DIFF_V = DIFF_HEADS * 2 * DIFF_HEAD
DIFF_IN = 2 * DIFF_QK + DIFF_V
ROPE_THETA = 500000.0
ROPE_DIM = DIFF_HEAD // 4
ROPE_HALF = ROPE_DIM // 2

GLA_HEADS = 4
GLA_KEY = 64
GLA_VALUE = 128
GLA_K_WIDTH = GLA_HEADS * GLA_KEY
GLA_V_WIDTH = GLA_HEADS * GLA_VALUE
GLA_GATE_LORA = 16
GLA_TAU = 16.0
GLA_SUB = 16
GLA_IN = 2 * GLA_K_WIDTH + 2 * GLA_V_WIDTH + GLA_GATE_LORA
GLA_IN_PAD = 2 * GLA_K_WIDTH + 2 * GLA_V_WIDTH + 128

N_BRANCHES = 3
MOE_GROUPS = 4
MOE_EXPERTS = 8
MOE_HIDDEN = 256
N_EXPERTS = MOE_GROUPS * MOE_EXPERTS
MOE_TILE = 1024
LANE = 128
V7X_VMEM_LIMIT = 56 * 1024 * 1024


def _cparams(sem):
    return pltpu.CompilerParams(dimension_semantics=sem, vmem_limit_bytes=V7X_VMEM_LIMIT)


def _nt(a, b, **kw):
    return lax.dot_general(a, b, (((1,), (1,)), ((), ())), preferred_element_type=F32, **kw)


def _tn(a, b, **kw):
    return lax.dot_general(a, b, (((0,), (0,)), ((), ())), preferred_element_type=F32, **kw)


def _dot(a, b, **kw):
    return jnp.dot(a, b, preferred_element_type=F32, **kw)


def _mm_kernel(a_ref, b_ref, o_ref, *, gate):
    acc = _dot(a_ref[...], b_ref[0])
    if gate:
        acc = jax.nn.sigmoid(acc)
    o_ref[...] = acc.astype(o_ref.dtype)


def _matmul(a, b, l, out_dtype, tm, tn, gate=False):
    M, K = a.shape
    N = b.shape[2]
    tm, tn = min(tm, M), min(tn, N)
    return pl.pallas_call(
        functools.partial(_mm_kernel, gate=gate),
        grid=(N // tn, M // tm),
        in_specs=[pl.BlockSpec((tm, K), lambda j, i: (i, 0)),
                  pl.BlockSpec((1, K, tn), lambda j, i: (l, 0, j))],
        out_specs=pl.BlockSpec((tm, tn), lambda j, i: (i, j)),
        out_shape=jax.ShapeDtypeStruct((M, N), out_dtype),
        compiler_params=_cparams(("arbitrary", "arbitrary")),
        name="proj_mm",
    )(a, b)


def _mm_rope_kernel(a_ref, b_ref, c_ref, sa_ref, sb_ref, o_ref):
    acc = _dot(a_ref[...], b_ref[0])
    c, sa, sb = c_ref[...], sa_ref[...], sb_ref[...]
    for h in range(acc.shape[1] // LANE):
        blk = acc[:, h * LANE:(h + 1) * LANE]
        out = blk * c + pltpu.roll(blk, LANE - ROPE_HALF, 1) * sa + pltpu.roll(blk, ROPE_HALF, 1) * sb
        o_ref[:, h * LANE:(h + 1) * LANE] = out.astype(o_ref.dtype)


def _matmul_rope(a, b, l, tabs, tm, tn):
    M, K = a.shape
    N = b.shape[2]
    tm, tn = min(tm, M), min(tn, N)
    tab_spec = pl.BlockSpec((tm, LANE), lambda j, i: (i, 0))
    return pl.pallas_call(
        _mm_rope_kernel,
        grid=(N // tn, M // tm),
        in_specs=[pl.BlockSpec((tm, K), lambda j, i: (i, 0)),
                  pl.BlockSpec((1, K, tn), lambda j, i: (l, 0, j)),
                  tab_spec, tab_spec, tab_spec],
        out_specs=pl.BlockSpec((tm, tn), lambda j, i: (i, j)),
        out_shape=jax.ShapeDtypeStruct((M, N), BF16),
        compiler_params=_cparams(("arbitrary", "arbitrary")),
        name="proj_qk_rope",
    )(a, b, *tabs)


def _mm_vt_kernel(wt_ref, x_ref, o_ref):
    o_ref[0] = _nt(wt_ref[0], x_ref[...]).astype(o_ref.dtype)


def _matmul_vt(wt, l, x, tk):
    _, N, K = wt.shape
    T = x.shape[0]
    return pl.pallas_call(
        _mm_vt_kernel,
        grid=(T // tk,),
        in_specs=[pl.BlockSpec((1, N, K), lambda i: (l, 0, 0)),
                  pl.BlockSpec((tk, K), lambda i: (i, 0))],
        out_specs=pl.BlockSpec((1, N, tk), lambda i: (i, 0, 0)),
        out_shape=jax.ShapeDtypeStruct((T // tk, N, tk), BF16),
        compiler_params=_cparams(("arbitrary",)),
        name="proj_vt",
    )(wt, x)


def _rope_tables(T):
    pos = jnp.arange(T, dtype=F32)[:, None]
    inv_freq = ROPE_THETA ** (-jnp.arange(ROPE_HALF, dtype=F32) / ROPE_HALF)
    lane = np.arange(LANE) % DIFF_HEAD
    ang = pos * inv_freq[None, :]
    cos8, sin8 = jnp.cos(ang), jnp.sin(ang)
    idx = jnp.asarray(lane % ROPE_HALF)
    cos_l, sin_l = cos8[:, idx], sin8[:, idx]
    first = jnp.asarray(lane < ROPE_HALF)[None, :]
    second = jnp.asarray((lane >= ROPE_HALF) & (lane < ROPE_DIM))[None, :]
    c = jnp.where(first | second, cos_l, 1.0)
    sa = jnp.where(first, -sin_l, 0.0)
    sb = jnp.where(second, sin_l, 0.0)
    return c, sa, sb


def _rwkv_kernel(a_ref, mu_ref, w2_ref, w0_ref, a2_ref, a0_ref, g2_ref, kks_ref, kas_ref, rk_ref,
                 gng_ref, gnb_ref, seg_ref, tri_ref, o_ref,
                 prev_ref, h_ref, at_ref, rt_ref, bt_ref, kt_ref, bp_ref, kp_ref, v_ref, gc_ref, y_ref):
    TB = a_ref.shape[0]
    C, N, W = RWKV_CHUNK, RWKV_HEAD, RWKV_WIDTH

    @pl.when(pl.program_id(0) == 0)
    def _():
        prev_ref[...] = jnp.zeros_like(prev_ref)
        h_ref[...] = jnp.zeros_like(h_ref)

    a = a_ref[...]
    row = lax.broadcasted_iota(jnp.int32, a.shape, 0)
    shifted = jnp.where(row == 0, prev_ref[...], pltpu.roll(a, 1, 0))
    prev_ref[...] = a[TB - 1:TB, :]
    xs = a + (shifted - a) * mu_ref[...]

    r = xs[:, 0:W]
    k = xs[:, W:2 * W]
    v = xs[:, 2 * W:3 * W]
    w_lo = xs[:, 3 * W:3 * W + LORA_PAD]
    a_lo = xs[:, 3 * W + LORA_PAD:3 * W + 2 * LORA_PAD]
    g_lo = xs[:, 3 * W + 2 * LORA_PAD:]

    z = w0_ref[...] + _dot(jnp.tanh(w_lo).astype(BF16), w2_ref[...])
    w = -jax.nn.softplus(-z) - 0.5
    lw = -jnp.exp(w)
    aa = jax.nn.sigmoid(a0_ref[...] + _dot(a_lo.astype(BF16), a2_ref[...]))
    g = _dot(jax.nn.sigmoid(g_lo).astype(BF16), g2_ref[...])

    seg = seg_ref[...]
    kk = k * kks_ref[...]
    ss = _dot((kk * kk).astype(BF16), seg)
    kk = kk * lax.rsqrt(jnp.maximum(ss, 1e-24))
    km = k * (1.0 + (aa - 1.0) * kas_ref[...])
    beta = kk * aa

    tri = tri_ref[...]
    lw_hi = lw.astype(BF16)
    lw_r = lw - lw_hi.astype(F32)
    lw_mid = lw_r.astype(BF16)
    lw_lo = (lw_r - lw_mid.astype(F32)).astype(BF16)
    cum = _dot(tri, lw_hi) + (_dot(tri, lw_mid) + _dot(tri, lw_lo))
    cum3 = cum.reshape(TB // C, C, W)
    cum_c = jnp.broadcast_to(cum3[:, C - 1:C, :], cum3.shape).reshape(TB, W)

    e_in = jnp.exp(cum)
    e_neg = jnp.exp(-cum)
    e_end = jnp.exp(cum_c - cum)
    at_ref[...] = (-kk * jnp.exp(cum - lw)).astype(BF16)
    rt_ref[...] = (r * e_in).astype(BF16)
    bt_ref[...] = (beta * e_neg).astype(BF16)
    kt_ref[...] = (km * e_neg).astype(BF16)
    bp_ref[...] = (beta * e_end).astype(BF16)
    kp_ref[...] = (km * e_end).astype(BF16)
    v_ref[...] = v.astype(BF16)
    gc_ref[...] = jnp.exp(cum_c)

    ri = lax.broadcasted_iota(jnp.int32, (C, C), 0)
    ci = lax.broadcasted_iota(jnp.int32, (C, C), 1)
    strict = ci < ri
    incl = ci <= ri
    eye = ci == ri

    def chunk(c, carry):
        NC = TB // C
        rows = [pl.ds(pl.multiple_of((c * NC + ci) * C, C), C) for ci in range(NC)]
        HS = range(NC * RWKV_HEADS)
        rw = [rows[i // RWKV_HEADS] for i in HS]
        sl = [slice((i % RWKV_HEADS) * N, (i % RWKV_HEADS + 1) * N) for i in HS]
        cat = lambda a, b: jnp.concatenate([a, b], axis=0)
        at = [at_ref[rw[i], sl[i]] for i in HS]
        rt = [rt_ref[rw[i], sl[i]] for i in HS]
        bt = [bt_ref[rw[i], sl[i]] for i in HS]
        kt = [kt_ref[rw[i], sl[i]] for i in HS]
        bp = [bp_ref[rw[i], sl[i]] for i in HS]
        kp = [kp_ref[rw[i], sl[i]] for i in HS]
        vh = [v_ref[rw[i], sl[i]] for i in HS]
        ar = [cat(at[h], rt[h]) for h in HS]
        xb = [_nt(ar[h], bt[h]) for h in HS]
        xk = [_nt(ar[h], kt[h]) for h in HS]
        a_ab = [jnp.where(strict, xb[h][:C], 0.0) for h in HS]
        a_rb = [jnp.where(incl, xb[h][C:], 0.0).astype(BF16) for h in HS]
        a_ak = [jnp.where(strict, xk[h][:C], 0.0).astype(BF16) for h in HS]
        a_rk = [jnp.where(incl, xk[h][C:], 0.0).astype(BF16) for h in HS]
        ident = jnp.where(eye, 1.0, 0.0)
        t_inv = [ident + a_ab[h] for h in HS]
        ap = a_ab
        for _ in range(5):
            apb = [ap[h].astype(BF16) for h in HS]
            ap = [_dot(apb[h], apb[h]) for h in HS]
            t_inv = [t_inv[h] + _dot(t_inv[h].astype(BF16), ap[h].astype(BF16)) for h in HS]
        tb = [t_inv[h].astype(BF16) for h in HS]
        wv = [_dot(a_ak[h], vh[h]).astype(BF16) for h in HS]
        pb = [_dot(tb[h], at[h]).astype(BF16) for h in HS]
        u0b = [_dot(tb[h], wv[h]).astype(BF16) for h in HS]
        qm = [rt[h].astype(F32) + _dot(a_rb[h], pb[h]) for h in HS]
        y0 = [_dot(a_rb[h], u0b[h]) + _dot(a_rk[h], vh[h]) for h in HS]
        gm = [jnp.where(eye, gc_ref[rw[h], sl[h]][0:1, :], 0.0) + _tn(bp[h], pb[h]) for h in HS]
        hadd = [_tn(cat(bp[h], kp[h]), cat(u0b[h], vh[h])) for h in HS]
        qg = [cat(qm[h], gm[h]).astype(BF16) for h in HS]
        for i in HS:
            hd = i % RWKV_HEADS
            z = _dot(qg[i], h_ref[hd].astype(BF16))
            y_ref[rw[i], sl[i]] = z[:C] + y0[i]
            h_ref[hd] = z[C:] + hadd[i]
        return carry

    chunk(0, 0)

    y = y_ref[...]
    segm = seg_ref[...]
    mu = _dot(y.astype(BF16), segm) * (1.0 / N)
    yc = y - mu
    var = _dot((yc * yc).astype(BF16), segm) * (1.0 / N)
    yn = yc * lax.rsqrt(var + RWKV_GN_EPS) * gng_ref[...] + gnb_ref[...]
    bonus = _dot((r * km * rk_ref[...]).astype(BF16), segm) * v
    o_ref[...] = ((yn + bonus) * g).astype(o_ref.dtype)


def _rwkv(a_proj, mu, w2, w0, a2, a0, g2, kks, kas, rk, gng, gnb, tb):
    T = a_proj.shape[0]
    tb = min(tb, T)
    W = RWKV_WIDTH
    hid = np.arange(W) // RWKV_HEAD
    seg = jnp.asarray(hid[:, None] == hid[None, :], dtype=BF16)
    t_idx = np.arange(tb)
    tri = jnp.asarray((t_idx[:, None] // RWKV_CHUNK == t_idx[None, :] // RWKV_CHUNK)
                      & (t_idx[None, :] <= t_idx[:, None]), dtype=BF16)
    full = lambda shp: pl.BlockSpec(shp, lambda i: (0,) * len(shp))
    row = lambda n: full((1, n))
    bf = lambda: pltpu.VMEM((tb, W), BF16)
    return pl.pallas_call(
        _rwkv_kernel,
        grid=(T // tb,),
        in_specs=[pl.BlockSpec((tb, RWKV_IN_PAD), lambda i: (i, 0)),
                  row(RWKV_IN_PAD), full((LORA_PAD, W)), row(W), full((LORA_PAD, W)), row(W),
                  full((RWKV_GATE_LORA, W)), row(W), row(W), row(W), row(W), row(W),
                  full((W, W)), full((tb, tb))],
        out_specs=pl.BlockSpec((tb, W), lambda i: (i, 0)),
        out_shape=jax.ShapeDtypeStruct((T, W), BF16),
        scratch_shapes=[pltpu.VMEM((1, RWKV_IN_PAD), F32),
                        pltpu.VMEM((RWKV_HEADS, RWKV_HEAD, RWKV_HEAD), F32),
                        bf(), bf(), bf(), bf(), bf(), bf(), bf(),
                        pltpu.VMEM((tb, W), F32), pltpu.VMEM((tb, W), F32)],
        compiler_params=_cparams(("arbitrary",)),
        name="rwkv7",
    )(a_proj, mu, w2, w0, a2, a0, g2, kks, kas, rk, gng, gnb, seg, tri)


def _diff_kernel(q_ref, k_ref, vt_ref, lq1_ref, lk1_ref, lq2_ref, lk2_ref, sg_ref, o_ref,
                 m_ref, l_ref, acc_ref, s_ref, p_ref, a_ref, *, lambda_init):
    tq = q_ref.shape[0]
    nkb, _, tk = vt_ref.shape
    i = pl.program_id(1)
    q = q_ref[...]
    lane = lax.broadcasted_iota(jnp.int32, q.shape, 1)
    qm = (jnp.where(lane < DIFF_HEAD, q, jnp.zeros_like(q)),
          jnp.where(lane >= DIFF_HEAD, q, jnp.zeros_like(q)))
    m_ref[...] = jnp.full(m_ref.shape, -jnp.inf, F32)
    l_ref[...] = jnp.zeros_like(l_ref)
    acc_ref[...] = jnp.zeros_like(acc_ref)
    p_ref[1] = jnp.zeros(p_ref.shape[1:], BF16)
    a_ref[1] = jnp.ones(a_ref.shape[1:], F32)

    def scores(j, slot):
        jc = jnp.minimum(j, nkb - 1)
        kj = k_ref[pl.ds(pl.multiple_of(jc * tk, tk), tk), :]
        for mi in range(2):
            s_ref[slot, mi] = _nt(kj, qm[mi])

    def values(j, slot):
        vj = vt_ref[jnp.clip(j, 0, nkb - 1)]
        for mi in range(2):
            acc_ref[mi] = a_ref[slot, mi] * acc_ref[mi] + _dot(vj, p_ref[slot, mi])

    def softmax(j, slot, masked):
        for mi in range(2):
            s = s_ref[slot, mi]
            if masked:
                d = (lax.broadcasted_iota(jnp.int32, (tk, tq), 0)
                     - lax.broadcasted_iota(jnp.int32, (tk, tq), 1))
                s = jnp.where(d <= i * tq - j * tk, s, -jnp.inf)
            m_old = m_ref[mi]
            m_new = jnp.maximum(m_old, jnp.max(s, axis=0, keepdims=True))
            alpha = jnp.exp2(m_old - m_new)
            p = jnp.exp2(s - m_new)
            l_ref[mi] = alpha * l_ref[mi] + jnp.sum(p, axis=0, keepdims=True)
            m_ref[mi] = m_new
            a_ref[slot, mi] = alpha
            p_ref[slot, mi] = p.astype(BF16)

    def half(j, cur, masked):
        nxt = 1 - cur
        scores(j + 1, nxt)
        values(j - 1, nxt)
        softmax(j, cur, masked)

    def pair(jj, carry):
        half(2 * jj, 0, False)
        half(2 * jj + 1, 1, False)
        return carry

    scores(0, 0)
    n_pairs = ((i + 1) * (tq // tk) + 1) // 2
    lax.fori_loop(0, n_pairs - 1, pair, 0)
    j_last = 2 * (n_pairs - 1)
    half(j_last, 0, True)
    half(j_last + 1, 1, True)
    values(j_last + 1, 1)

    lam = (jnp.exp(jnp.sum(lq1_ref[...] * lk1_ref[...])) - jnp.exp(jnp.sum(lq2_ref[...] * lk2_ref[...]))
           + lambda_init)
    o = acc_ref[0] / l_ref[0] - lam * (acc_ref[1] / l_ref[1])
    o = o * lax.rsqrt(jnp.mean(o * o, axis=0, keepdims=True) + LN_EPS) * (1.0 - lambda_init)
    o = o * sg_ref[...]
    o_ref[...] = o.T.astype(o_ref.dtype)


def _diff_attention(qk, vt, lq1, lk1, lq2, lk2, subln_g, lambda_init, tq):
    T = qk.shape[0]
    tk = vt.shape[2]
    assert tq % tk == 0 and T % tq == 0
    H, E = DIFF_HEADS, 2 * DIFF_HEAD
    vec = lambda: pl.BlockSpec((1, DIFF_HEAD), lambda h, i: (0, 0))
    return pl.pallas_call(
        functools.partial(_diff_kernel, lambda_init=lambda_init),
        grid=(H, T // tq),
        in_specs=[pl.BlockSpec((tq, E), lambda h, i: (i, h)),
                  pl.BlockSpec((T, E), lambda h, i: (0, H + h)),
                  pl.BlockSpec((T // tk, E, tk), lambda h, i: (0, h, 0)),
                  vec(), vec(), vec(), vec(),
                  pl.BlockSpec((E, 1), lambda h, i: (0, 0))],
        out_specs=pl.BlockSpec((tq, E), lambda h, i: (i, h)),
        out_shape=jax.ShapeDtypeStruct((T, DIFF_V), BF16),
        scratch_shapes=[pltpu.VMEM((2, 1, tq), F32), pltpu.VMEM((2, 1, tq), F32),
                        pltpu.VMEM((2, E, tq), F32), pltpu.VMEM((2, 2, tk, tq), F32),
                        pltpu.VMEM((2, 2, tk, tq), BF16), pltpu.VMEM((2, 2, 1, tq), F32)],
        compiler_params=_cparams(("arbitrary", "arbitrary")),
        name="diff_attn",
    )(qk, qk, vt, lq1, lk1, lq2, lk2, subln_g)


def _gla_kernel(c_ref, g2_ref, gb_ref, gng_ref, tri_ref, segv_ref, o_ref,
                st_ref, oacc_ref):
    TB = c_ref.shape[0]
    KW, VW, S = GLA_K_WIDTH, GLA_V_WIDTH, GLA_SUB
    DK, DV = GLA_KEY, GLA_VALUE

    @pl.when(pl.program_id(0) == 0)
    def _():
        st_ref[...] = jnp.zeros_like(st_ref)

    g_lo = c_ref[:, 2 * KW + 2 * VW:]
    gate = _dot(g_lo, g2_ref[...], precision=HIGHEST) + gb_ref[...]
    log_a = jax.nn.log_sigmoid(gate) * (1.0 / GLA_TAU)
    b = _dot(tri_ref[...], log_a, precision=HIGHEST)
    NS = TB // S
    b3 = b.reshape(NS, S, KW)
    q3 = c_ref[:, 0:KW].reshape(NS, S, KW)
    k3 = c_ref[:, KW:2 * KW].reshape(NS, S, KW)
    v = c_ref[:, 2 * KW:2 * KW + VW]
    v3 = v.reshape(NS, S, VW)
    bl3 = b3[:, S - 1:S, :]
    qb = (q3 * jnp.exp(b3)).astype(BF16).reshape(TB, KW)
    kb = (k3 * jnp.exp(bl3 - b3)).astype(BF16).reshape(TB, KW)
    dec = jnp.exp(bl3)
    vb = v.astype(BF16)

    ri = lax.broadcasted_iota(jnp.int32, (NS, S, 1), 1)
    segv = segv_ref[...]
    acc = jnp.zeros((NS, S, VW), F32)
    for j in range(S):
        e = jnp.exp(jnp.minimum(b3 - b3[:, j:j + 1, :], 0.0))
        t = (q3 * e * k3[:, j:j + 1, :]).astype(BF16).reshape(TB, KW)
        att = _dot(t, segv).reshape(NS, S, VW)
        acc = acc + jnp.where(ri >= j, att, 0.0) * v3[:, j:j + 1, :]
    oacc_ref[...] = acc.reshape(TB, VW)

    for h in range(GLA_HEADS):
        lk = slice(h * DK, (h + 1) * DK)
        lv = slice(h * DV, (h + 1) * DV)
        kv = [_tn(vb[s * S:(s + 1) * S, lv], kb[s * S:(s + 1) * S, lk]) for s in range(NS)]
        st = st_ref[h]
        for s in range(NS):
            rows = slice(s * S, (s + 1) * S)
            oacc_ref[rows, lv] += _nt(qb[rows, lk], st.astype(BF16))
            st = st * dec[s][:, lk] + kv[s]
        st_ref[h] = st

    o = oacc_ref[...]
    rgate = c_ref[:, 2 * KW + VW:2 * KW + 2 * VW]
    for h in range(GLA_HEADS):
        lv = slice(h * DV, (h + 1) * DV)
        oh = o[:, lv]
        oh = oh * lax.rsqrt(jnp.mean(oh * oh, axis=1, keepdims=True) + LN_EPS) * gng_ref[:, lv]
        o_ref[:, lv] = (oh * jax.nn.silu(rgate[:, lv])).astype(o_ref.dtype)


def _gla(c_proj, g2, gb, gng, tb):
    T = c_proj.shape[0]
    tb = min(tb, T)
    KW, VW = GLA_K_WIDTH, GLA_V_WIDTH
    t_idx = np.arange(tb)
    tri = jnp.asarray((t_idx[:, None] // GLA_SUB == t_idx[None, :] // GLA_SUB)
                      & (t_idx[None, :] <= t_idx[:, None]), dtype=F32)
    segv = jnp.asarray((np.arange(KW) // GLA_KEY)[:, None] == (np.arange(VW) // GLA_VALUE)[None, :], dtype=BF16)
    full = lambda shp: pl.BlockSpec(shp, lambda i: (0,) * len(shp))
    return pl.pallas_call(
        _gla_kernel,
        grid=(T // tb,),
        in_specs=[pl.BlockSpec((tb, GLA_IN_PAD), lambda i: (i, 0)),
                  full((LANE, KW)), full((1, KW)), full((1, VW)), full((tb, tb)), full((KW, VW))],
        out_specs=pl.BlockSpec((tb, VW), lambda i: (i, 0)),
        out_shape=jax.ShapeDtypeStruct((T, VW), BF16),
        scratch_shapes=[pltpu.VMEM((GLA_HEADS, GLA_VALUE, GLA_KEY), F32),
                        pltpu.VMEM((tb, VW), F32)],
        compiler_params=_cparams(("arbitrary",)),
        name="gla",
    )(c_proj, g2, gb, gng, tri, segv)


def _merge_kernel(oa_ref, ob_ref, oc_ref, g_ref, pa_ref, pb_ref, pc_ref, o_ref):
    D = D_MODEL
    m = g_ref[:, 0:D].astype(F32) * _dot(oa_ref[...], pa_ref[...])
    m = m + g_ref[:, D:2 * D].astype(F32) * _dot(ob_ref[...], pb_ref[...])
    m = m + g_ref[:, 2 * D:3 * D].astype(F32) * _dot(oc_ref[...], pc_ref[...])
    o_ref[...] = m.astype(o_ref.dtype)


def _merge(oa, ob, oc, gates, pa, pb, pc, tm):
    T = oa.shape[0]
    tm = min(tm, T)
    D = D_MODEL
    rowblk = lambda n: pl.BlockSpec((tm, n), lambda i: (i, 0))
    full = lambda shp: pl.BlockSpec(shp, lambda i: (0, 0))
    return pl.pallas_call(
        _merge_kernel,
        grid=(T // tm,),
        in_specs=[rowblk(RWKV_WIDTH), rowblk(DIFF_V), rowblk(GLA_V_WIDTH), rowblk(3 * D),
                  full((RWKV_WIDTH, D)), full((DIFF_V, D)), full((GLA_V_WIDTH, D))],
        out_specs=rowblk(D),
        out_shape=jax.ShapeDtypeStruct((T, D), BF16),
        compiler_params=_cparams(("arbitrary",)),
        name="merge",
    )(oa, ob, oc, gates, pa, pb, pc)


def _layer_norm(z, g, b):
    mu = jnp.mean(z, axis=1, keepdims=True)
    zc = z - mu
    var = jnp.mean(zc * zc, axis=1, keepdims=True)
    return zc * lax.rsqrt(var + LN_EPS) * g + b


def _mm_ln_kernel(a_ref, w_ref, res_ref, g_ref, b_ref, wrh_ref, wrl_ref, br_ref, o_ref):
    z = DEEPNORM_ALPHA * res_ref[...] + _dot(a_ref[...], w_ref[...])
    out = _layer_norm(z, g_ref[...], b_ref[...])
    o_ref[:, 0:D_MODEL] = out
    o_ref[:, D_MODEL:] = _route(out, wrh_ref[...], wrl_ref[...], br_ref[...])


def _mm_ln(a, w, res, g, b, wr, br, tm):
    wr_hi = wr.astype(BF16)
    wr_lo = (wr - wr_hi.astype(F32)).astype(BF16)
    T, K = a.shape
    D = D_MODEL
    tm = min(tm, T)
    rowblk = lambda n: pl.BlockSpec((tm, n), lambda i: (i, 0))
    full = lambda shp: pl.BlockSpec(shp, lambda i: (0, 0))
    return pl.pallas_call(
        _mm_ln_kernel,
        grid=(T // tm,),
        in_specs=[rowblk(K), full((K, D)), rowblk(D), full((1, D)), full((1, D)), full((D, LANE)), full((D, LANE)),
                  full((1, LANE))],
        out_specs=rowblk(D + LANE),
        out_shape=jax.ShapeDtypeStruct((T, D + LANE), F32),
        compiler_params=_cparams(("arbitrary",)),
        name="wo_ln_route",
    )(a, w, res, g, b, wr_hi, wr_lo, br)


def _route(x, w_hi, w_lo, b):
    G, E = MOE_GROUPS, MOE_EXPERTS
    x_hi = x.astype(BF16)
    x_lo = (x - x_hi.astype(F32)).astype(BF16)
    logits = _dot(x_hi, w_hi) + (_dot(x_lo, w_hi) + _dot(x_hi, w_lo)) + b
    lane = lax.broadcasted_iota(jnp.int32, logits.shape, 1)
    neg = -jnp.inf
    big = jnp.int32(1 << 20)
    is_g = lane < G
    lg = jnp.where(is_g, logits, neg)
    gmax = jnp.max(lg, axis=1, keepdims=True)
    gidx = jnp.min(jnp.where(is_g & (lg == gmax), lane, big), axis=1, keepdims=True)
    g_p = 1.0 / jnp.sum(jnp.exp(lg - gmax), axis=1, keepdims=True)
    lo = G + gidx * E
    in_grp = (lane >= lo) & (lane < lo + E)
    le = jnp.where(in_grp, logits, neg)
    m1 = jnp.max(le, axis=1, keepdims=True)
    i1 = jnp.min(jnp.where(in_grp & (le == m1), lane, big), axis=1, keepdims=True)
    le2 = jnp.where(lane == i1, neg, le)
    m2 = jnp.max(le2, axis=1, keepdims=True)
    i2 = jnp.min(jnp.where(in_grp & (le2 == m2), lane, big), axis=1, keepdims=True)
    e2 = jnp.exp(m2 - m1)
    p1 = 1.0 / (1.0 + e2)
    p2 = e2 / (1.0 + e2)
    comb = jnp.where(lane == i1, g_p * p1, 0.0) + jnp.where(lane == i2, g_p * p2, 0.0)
    return jnp.where(lane == 0, gidx.astype(F32), comb)


def _moe_tables(gid_f, tm):
    T = gid_f.shape[0]
    G = MOE_GROUPS
    nt = T // tm + G
    gid = gid_f.astype(jnp.int32)
    order = jnp.argsort(gid).astype(jnp.int32)
    counts = jnp.sum((gid[:, None] == jnp.arange(G)[None, :]).astype(jnp.int32), axis=0)
    starts = jnp.cumsum(counts) - counts
    ntile = (counts + tm - 1) // tm
    tstart = jnp.cumsum(ntile) - ntile
    n = jnp.arange(nt, dtype=jnp.int32)
    tg = jnp.sum((n[:, None] >= tstart[None, 1:]).astype(jnp.int32), axis=1)
    k = n - tstart[tg]
    row0 = starts[tg] + k * tm
    nvalid = jnp.clip(counts[tg] - k * tm, 0, tm).astype(jnp.int32)
    rows = jnp.clip(row0[:, None] + jnp.arange(tm, dtype=jnp.int32)[None, :], 0, T - 1)
    idx = jnp.take(order, rows).reshape(nt, 1, tm)
    return tg.astype(jnp.int32), nvalid, idx


def _moe_kernel(tg_ref, nv_ref, idx_ref, idxn_ref, x_hbm, wg_ref, wu_ref, wd_ref, g_ref, b_ref, y_hbm,
                xs_ref, xb_ref, h_ref, out_ref, pend_ref, gx_sem, sc_sem):
    n = pl.program_id(0)
    e = pl.program_id(1)
    nt = pl.num_programs(0)
    nv = nv_ref[n]
    slot = n % 2
    other = 1 - slot
    tm = out_ref.shape[0]
    part = tm // MOE_EXPERTS
    F = MOE_HIDDEN

    def start_row(idx, dst_slot, r, prio):
        tok = idx[0, 0, r]
        pltpu.make_async_copy(x_hbm.at[pl.ds(tok, 1)], xs_ref.at[dst_slot, pl.ds(r, 1)],
                              gx_sem.at[dst_slot]).start(priority=prio)

    def start_next_part():
        for r in range(part):
            start_row(idxn_ref, other, e * part + r, r % 2)

    def wait_rows(s):
        pltpu.make_async_copy(x_hbm.at[pl.ds(0, tm)], xs_ref.at[s], gx_sem.at[s]).wait()

    def drain():
        @pl.when(pend_ref[0] == tm)
        def _():
            pltpu.make_async_copy(out_ref, y_hbm.at[pl.ds(0, tm)], sc_sem.at[0]).wait()

        @pl.when(pend_ref[0] < tm)
        def _():
            def body(r, c):
                pltpu.make_async_copy(out_ref.at[pl.ds(0, 1)], y_hbm.at[pl.ds(0, 1)], sc_sem.at[0]).wait()
                return c
            lax.fori_loop(0, pend_ref[0], body, 0)
        pend_ref[0] = 0

    @pl.when((n == 0) & (e == 0))
    def _():
        pend_ref[0] = 0

        def body(r, c):
            start_row(idx_ref, 0, 2 * r, 0)
            start_row(idx_ref, 0, 2 * r + 1, 1)
            return c
        lax.fori_loop(0, tm // 2, body, 0)

    @pl.when(e == 0)
    def _():
        wait_rows(slot)
        xb_ref[...] = xs_ref[slot, :, 0:D_MODEL].astype(BF16)

    @pl.when(nv > 0)
    def _():
        start_next_part()
        xb = xb_ref[...]
        hg = _dot(xb, wg_ref[0])
        hu = _dot(xb, wu_ref[0])
        comb = xs_ref[slot, :, D_MODEL:]
        lane = lax.broadcasted_iota(jnp.int32, comb.shape, 1)
        col = MOE_GROUPS + tg_ref[n] * MOE_EXPERTS + e
        cw = jnp.sum(jnp.where(lane == col, comb, 0.0), axis=1, keepdims=True)
        h_ref[e] = (jax.nn.silu(hg) * hu * cw).astype(BF16)

    @pl.when(nv == 0)
    def _():
        start_next_part()

    @pl.when(e == MOE_EXPERTS - 1)
    def _():
        drain()

        @pl.when(nv > 0)
        def _():
            y = _dot(h_ref[0], wd_ref[0, 0:F, :])
            for j in range(1, MOE_EXPERTS):
                y = y + _dot(h_ref[j], wd_ref[0, j * F:(j + 1) * F, :])
            z = DEEPNORM_ALPHA * xs_ref[slot, :, 0:D_MODEL] + y
            out_ref[...] = _layer_norm(z, g_ref[...], b_ref[...])

        def put(r, prio):
            tok = idx_ref[0, 0, r]
            pltpu.make_async_copy(out_ref.at[pl.ds(r, 1)], y_hbm.at[pl.ds(tok, 1)],
                                  sc_sem.at[0]).start(priority=prio)

        def body(r, c):
            put(2 * r, 0)
            put(2 * r + 1, 1)
            return c
        lax.fori_loop(0, nv // 2, body, 0)

        @pl.when(nv % 2 == 1)
        def _():
            put(nv - 1, 0)
        pend_ref[0] = nv

        @pl.when(n == nt - 1)
        def _():
            drain()
            wait_rows(other)


def _moe(xa, wg, wu, wd, l, g, b, tm):
    T = xa.shape[0]
    D = D_MODEL
    F = MOE_HIDDEN
    tm = min(tm, T)
    tg, nvalid, idx = _moe_tables(xa[:, D], tm)
    nt = tg.shape[0]
    wspec = lambda shp: pl.BlockSpec(shp, lambda n, e, tg_r, nv_r: (l * N_EXPERTS + tg_r[n] * MOE_EXPERTS + e, 0, 0))
    ispec = lambda f: pl.BlockSpec((1, 1, tm), f, memory_space=pltpu.SMEM)
    vec = pl.BlockSpec((1, D), lambda n, e, tg_r, nv_r: (0, 0))
    grid_spec = pltpu.PrefetchScalarGridSpec(
        num_scalar_prefetch=2,
        grid=(nt, MOE_EXPERTS),
        in_specs=[ispec(lambda n, e, tg_r, nv_r: (n, 0, 0)),
                  ispec(lambda n, e, tg_r, nv_r: (jnp.minimum(n + 1, nt - 1), 0, 0)),
                  pl.BlockSpec(memory_space=pl.ANY),
                  wspec((1, D, F)), wspec((1, D, F)),
                  pl.BlockSpec((1, MOE_EXPERTS * F, D), lambda n, e, tg_r, nv_r: (l * MOE_GROUPS + tg_r[n], 0, 0),
                               pipeline_mode=pl.Buffered(1)),
                  vec, vec],
        out_specs=pl.BlockSpec(memory_space=pl.ANY),
        scratch_shapes=[pltpu.VMEM((2, tm, D + LANE), F32),
                        pltpu.VMEM((tm, D), BF16), pltpu.VMEM((MOE_EXPERTS, tm, F), BF16),
                        pltpu.VMEM((tm, D), F32), pltpu.SMEM((1,), jnp.int32),
                        pltpu.SemaphoreType.DMA((2,)), pltpu.SemaphoreType.DMA((1,))],
    )
    return pl.pallas_call(
        _moe_kernel,
        grid_spec=grid_spec,
        out_shape=jax.ShapeDtypeStruct((T, D), F32),
        compiler_params=_cparams(("arbitrary", "arbitrary")),
        name="moe_experts_ln",
    )(tg, nvalid, idx, idx, xa, wg, wu, wd, g, b)


def _pad_cols(w, n):
    return jnp.pad(w, ((0, 0), (0, n - w.shape[1])))


def _prep_weights(p):
    W = RWKV_WIDTH
    KW, VW = GLA_K_WIDTH, GLA_V_WIDTH
    w_in = p['w_in']
    o_b = RWKV_IN
    o_c = o_b + DIFF_IN
    o_g = o_c + GLA_IN
    lo1 = 3 * W + RWKV_DECAY_LORA
    lo2 = lo1 + RWKV_AAA_LORA
    padc = lambda w, n: jnp.pad(w, ((0, 0), (0, 0), (0, n - w.shape[2])))
    w_a = jnp.concatenate([padc(w_in[:, :, :lo1], 3 * W + LORA_PAD), padc(w_in[:, :, lo1:lo2], LORA_PAD),
                           w_in[:, :, lo2:o_b]], axis=2).astype(BF16)
    qscale = DIFF_HEAD ** -0.5 * math.log2(math.e)
    w_qk = jnp.concatenate([w_in[:, :, o_b:o_b + DIFF_QK] * qscale, w_in[:, :, o_b + DIFF_QK:o_b + 2 * DIFF_QK]],
                           axis=2).astype(BF16)
    w_vt = jnp.swapaxes(w_in[:, :, o_b + 2 * DIFF_QK:o_c], 1, 2).astype(BF16)
    c0 = o_c
    w_c = jnp.concatenate([w_in[:, :, c0:c0 + KW] * (GLA_KEY ** -0.5), w_in[:, :, c0 + KW:c0 + 2 * KW + VW],
                           w_in[:, :, c0 + 2 * KW + VW + GLA_GATE_LORA:o_g],
                           padc(w_in[:, :, c0 + 2 * KW + VW:c0 + 2 * KW + VW + GLA_GATE_LORA], LANE)],
                          axis=2).astype(BF16)
    w_g = w_in[:, :, o_g:].astype(BF16)
    L = w_in.shape[0]
    F = MOE_HIDDEN
    return dict(
        w_a=w_a, w_qk=w_qk, w_vt=w_vt, w_c=w_c, w_g=w_g,
        wg=p['moe_w_gate'].astype(BF16).reshape(L * N_EXPERTS, D_MODEL, F),
        wu=p['moe_w_up'].astype(BF16).reshape(L * N_EXPERTS, D_MODEL, F),
        wd=p['moe_w_down'].astype(BF16).reshape(L * MOE_GROUPS, MOE_EXPERTS * F, D_MODEL))


def _layer(x, xb, l, p, wts, rope_tabs):
    T = x.shape[0]
    W = RWKV_WIDTH
    lambda_init = 0.8 - 0.6 * math.exp(-0.3 * l)
    lo1 = 3 * W + RWKV_DECAY_LORA
    lo2 = lo1 + RWKV_AAA_LORA
    mu = p['rwkv_mu'][l]
    mu_a = jnp.concatenate([jnp.pad(mu[:lo1], (0, LORA_PAD - RWKV_DECAY_LORA)),
                            jnp.pad(mu[lo1:lo2], (0, LORA_PAD - RWKV_AAA_LORA)), mu[lo2:]])[None, :]

    proj_a = _matmul(xb, wts['w_a'], l, F32, 1024, 1024)
    qk = _matmul_rope(xb, wts['w_qk'], l, rope_tabs, 1024, 1024)
    tk = min(512, T)
    vt = _matmul_vt(wts['w_vt'], l, xb, tk)
    proj_c = _matmul(xb, wts['w_c'], l, F32, 512, GLA_IN_PAD)
    gates = _matmul(xb, wts['w_g'], l, BF16, 1024, 1024, gate=True)

    padr = lambda w: jnp.pad(w, ((0, LORA_PAD - w.shape[0]), (0, 0)))
    r2 = lambda v: v.reshape(1, -1)
    o_a = _rwkv(proj_a, mu_a, padr(p['rwkv_w2'][l]).astype(BF16), r2(p['rwkv_w0'][l]),
                padr(p['rwkv_a2'][l]).astype(BF16), r2(p['rwkv_a0'][l]), p['rwkv_g2'][l].astype(BF16),
                r2(p['rwkv_kk_scale'][l]), r2(p['rwkv_ka_scale'][l]), r2(p['rwkv_rk'][l]),
                r2(p['rwkv_gn_g'][l]), r2(p['rwkv_gn_b'][l]), 256)
    o_bb = _diff_attention(qk, vt, r2(p['diff_lq1'][l]), r2(p['diff_lk1'][l]), r2(p['diff_lq2'][l]),
                           r2(p['diff_lk2'][l]), p['diff_subln_g'][l].reshape(-1, 1), lambda_init, min(2 * tk, T))
    g2p = jnp.pad(p['gla_g2'][l], ((0, LANE - GLA_GATE_LORA), (0, 0)))
    o_c = _gla(proj_c, g2p, r2(p['gla_gb'][l]), r2(p['gla_gn_g'][l]), 256)

    merged = _merge(o_a, o_bb, o_c, gates, p['proj_a'][l].astype(BF16), p['proj_b'][l].astype(BF16),
                    p['proj_c'][l].astype(BF16), 256)
    wr = _pad_cols(jnp.concatenate([p['router_group_w'][l], p['router_expert_w'][l]], axis=1), LANE)
    br = _pad_cols(jnp.concatenate([p['router_group_b'][l], p['router_expert_b'][l]])[None, :], LANE)
    xa = _mm_ln(merged, p['w_o'][l].astype(BF16), x, r2(p['ln1_g'][l]), r2(p['ln1_b'][l]), wr, br, 512)
    return _moe(xa, wts['wg'], wts['wu'], wts['wd'], l, r2(p['ln2_g'][l]), r2(p['ln2_b'][l]), MOE_TILE)


def kernel(x, w_in, rwkv_mu, rwkv_w2, rwkv_w0, rwkv_a2, rwkv_a0, rwkv_g2, rwkv_kk_scale, rwkv_ka_scale, rwkv_rk, rwkv_gn_g, rwkv_gn_b, diff_lq1, diff_lk1, diff_lq2, diff_lk2, diff_subln_g, gla_g2, gla_gb, gla_gn_g, proj_a, proj_b, proj_c, w_o, ln1_g, ln1_b, router_group_w, router_group_b, router_expert_w, router_expert_b, moe_w_gate, moe_w_up, moe_w_down, ln2_g, ln2_b):
    p = dict(w_in=w_in, rwkv_mu=rwkv_mu, rwkv_w2=rwkv_w2, rwkv_w0=rwkv_w0, rwkv_a2=rwkv_a2, rwkv_a0=rwkv_a0,
             rwkv_g2=rwkv_g2, rwkv_kk_scale=rwkv_kk_scale, rwkv_ka_scale=rwkv_ka_scale, rwkv_rk=rwkv_rk,
             rwkv_gn_g=rwkv_gn_g, rwkv_gn_b=rwkv_gn_b, diff_lq1=diff_lq1, diff_lk1=diff_lk1, diff_lq2=diff_lq2,
             diff_lk2=diff_lk2, diff_subln_g=diff_subln_g, gla_g2=gla_g2, gla_gb=gla_gb, gla_gn_g=gla_gn_g,
             proj_a=proj_a, proj_b=proj_b, proj_c=proj_c, w_o=w_o, ln1_g=ln1_g, ln1_b=ln1_b,
             router_group_w=router_group_w, router_group_b=router_group_b, router_expert_w=router_expert_w,
             router_expert_b=router_expert_b, moe_w_gate=moe_w_gate, moe_w_up=moe_w_up, moe_w_down=moe_w_down,
             ln2_g=ln2_g, ln2_b=ln2_b)
    B, T, D = x.shape
    assert B == 1 and D == D_MODEL
    xf = x.reshape(T, D)
    xb = xf.astype(BF16)
    tabs = _rope_tables(T)
    wts = _prep_weights(p)
    for l in range(DEPTH):
        xf = _layer(xf, xb, l, p, wts, tabs)
        xb = xf.astype(BF16)
    return xf.reshape(B, T, D)
```

```python
import functools
import math

import numpy as np
import jax
import jax.numpy as jnp
from jax import lax
from jax.experimental import pallas as pl
from jax.experimental.pallas import tpu as pltpu

F32 = jnp.float32
BF16 = jnp.bfloat16
HIGHEST = lax.Precision.HIGHEST

D_MODEL = 2048
DEPTH = 2
DEEPNORM_ALPHA = (2 * DEPTH) ** 0.25
LN_EPS = 1e-5

RWKV_HEADS = 8
RWKV_HEAD = 64
RWKV_WIDTH = RWKV_HEADS * RWKV_HEAD
RWKV_DECAY_LORA = 96
RWKV_AAA_LORA = 96
RWKV_GATE_LORA = 256
RWKV_GN_EPS = 64e-5
RWKV_IN = 3 * RWKV_WIDTH + RWKV_DECAY_LORA + RWKV_AAA_LORA + RWKV_GATE_LORA
RWKV_CHUNK = 64
LORA_PAD = 128
RWKV_IN_PAD = 3 * RWKV_WIDTH + 2 * LORA_PAD + RWKV_GATE_LORA

DIFF_HEADS = 8
DIFF_HEAD = 64
DIFF_QK = DIFF_HEADS * 2 * DIFF_HEAD
DIFF_V = DIFF_HEADS * 2 * DIFF_HEAD
DIFF_IN = 2 * DIFF_QK + DIFF_V
ROPE_THETA = 500000.0
ROPE_DIM = DIFF_HEAD // 4
ROPE_HALF = ROPE_DIM // 2

GLA_HEADS = 4
GLA_KEY = 64
GLA_VALUE = 128
GLA_K_WIDTH = GLA_HEADS * GLA_KEY
GLA_V_WIDTH = GLA_HEADS * GLA_VALUE
GLA_GATE_LORA = 16
GLA_TAU = 16.0
GLA_SUB = 16
GLA_IN = 2 * GLA_K_WIDTH + 2 * GLA_V_WIDTH + GLA_GATE_LORA
GLA_IN_PAD = 2 * GLA_K_WIDTH + 2 * GLA_V_WIDTH + 128

N_BRANCHES = 3
MOE_GROUPS = 4
MOE_EXPERTS = 8
MOE_HIDDEN = 256
N_EXPERTS = MOE_GROUPS * MOE_EXPERTS
MOE_TILE = 1024
LANE = 128
V7X_VMEM_LIMIT = 56 * 1024 * 1024


def _cparams(sem):
    return pltpu.CompilerParams(dimension_semantics=sem, vmem_limit_bytes=V7X_VMEM_LIMIT)


def _nt(a, b, **kw):
    return lax.dot_general(a, b, (((1,), (1,)), ((), ())), preferred_element_type=F32, **kw)


def _tn(a, b, **kw):
    return lax.dot_general(a, b, (((0,), (0,)), ((), ())), preferred_element_type=F32, **kw)


def _dot(a, b, **kw):
    return jnp.dot(a, b, preferred_element_type=F32, **kw)


def _mm_kernel(a_ref, b_ref, o_ref, *, gate):
    acc = _nt(a_ref[...], b_ref[0])
    if gate:
        acc = jax.nn.sigmoid(acc)
    o_ref[...] = acc.astype(o_ref.dtype)


def _matmul(a, b, l, out_dtype, tm, tn, gate=False):
    M, K = a.shape
    N = b.shape[1]
    tm, tn = min(tm, M), min(tn, N)
    return pl.pallas_call(
        functools.partial(_mm_kernel, gate=gate),
        grid=(N // tn, M // tm),
        in_specs=[pl.BlockSpec((tm, K), lambda j, i: (i, 0)),
                  pl.BlockSpec((1, tn, K), lambda j, i: (l, j, 0))],
        out_specs=pl.BlockSpec((tm, tn), lambda j, i: (i, j)),
        out_shape=jax.ShapeDtypeStruct((M, N), out_dtype),
        compiler_params=_cparams(("arbitrary", "arbitrary")),
        name="proj_mm",
    )(a, b)


def _mm_rope_kernel(a_ref, b_ref, c_ref, sa_ref, sb_ref, o_ref):
    acc = _nt(a_ref[...], b_ref[0])
    c, sa, sb = c_ref[...], sa_ref[...], sb_ref[...]
    for h in range(acc.shape[1] // LANE):
        blk = acc[:, h * LANE:(h + 1) * LANE]
        out = blk * c + pltpu.roll(blk, LANE - ROPE_HALF, 1) * sa + pltpu.roll(blk, ROPE_HALF, 1) * sb
        o_ref[:, h * LANE:(h + 1) * LANE] = out.astype(o_ref.dtype)


def _matmul_rope(a, b, l, tabs, tm, tn):
    M, K = a.shape
    N = b.shape[1]
    tm, tn = min(tm, M), min(tn, N)
    tab_spec = pl.BlockSpec((tm, LANE), lambda j, i: (i, 0))
    return pl.pallas_call(
        _mm_rope_kernel,
        grid=(N // tn, M // tm),
        in_specs=[pl.BlockSpec((tm, K), lambda j, i: (i, 0)),
                  pl.BlockSpec((1, tn, K), lambda j, i: (l, j, 0)),
                  tab_spec, tab_spec, tab_spec],
        out_specs=pl.BlockSpec((tm, tn), lambda j, i: (i, j)),
        out_shape=jax.ShapeDtypeStruct((M, N), BF16),
        compiler_params=_cparams(("arbitrary", "arbitrary")),
        name="proj_qk_rope",
    )(a, b, *tabs)


def _mm_vt_kernel(wt_ref, x_ref, o_ref):
    o_ref[0] = _nt(wt_ref[0], x_ref[...]).astype(o_ref.dtype)


def _matmul_vt(wt, l, x, tk):
    _, N, K = wt.shape
    T = x.shape[0]
    return pl.pallas_call(
        _mm_vt_kernel,
        grid=(T // tk,),
        in_specs=[pl.BlockSpec((1, N, K), lambda i: (l, 0, 0)),
                  pl.BlockSpec((tk, K), lambda i: (i, 0))],
        out_specs=pl.BlockSpec((1, N, tk), lambda i: (i, 0, 0)),
        out_shape=jax.ShapeDtypeStruct((T // tk, N, tk), BF16),
        compiler_params=_cparams(("arbitrary",)),
        name="proj_vt",
    )(wt, x)


def _rope_tables(T):
    pos = jnp.arange(T, dtype=F32)[:, None]
    inv_freq = ROPE_THETA ** (-jnp.arange(ROPE_HALF, dtype=F32) / ROPE_HALF)
    lane = np.arange(LANE) % DIFF_HEAD
    ang = pos * inv_freq[None, :]
    cos8, sin8 = jnp.cos(ang), jnp.sin(ang)
    idx = jnp.asarray(lane % ROPE_HALF)
    cos_l, sin_l = cos8[:, idx], sin8[:, idx]
    first = jnp.asarray(lane < ROPE_HALF)[None, :]
    second = jnp.asarray((lane >= ROPE_HALF) & (lane < ROPE_DIM))[None, :]
    c = jnp.where(first | second, cos_l, 1.0)
    sa = jnp.where(first, -sin_l, 0.0)
    sb = jnp.where(second, sin_l, 0.0)
    return c, sa, sb


def _rwkv_kernel(a_ref, mu_ref, w2_ref, w0_ref, a2_ref, a0_ref, g2_ref, kks_ref, kas_ref, rk_ref,
                 gng_ref, gnb_ref, seg_ref, tri_ref, o_ref,
                 prev_ref, h_ref, at_ref, rt_ref, bt_ref, kt_ref, bp_ref, kp_ref, v_ref, gc_ref, y_ref):
    TB = a_ref.shape[0]
    C, N, W = RWKV_CHUNK, RWKV_HEAD, RWKV_WIDTH

    @pl.when(pl.program_id(0) == 0)
    def _():
        prev_ref[...] = jnp.zeros_like(prev_ref)
        h_ref[...] = jnp.zeros_like(h_ref)

    a = a_ref[...]
    row = lax.broadcasted_iota(jnp.int32, a.shape, 0)
    shifted = jnp.where(row == 0, prev_ref[...], pltpu.roll(a, 1, 0))
    prev_ref[...] = a[TB - 1:TB, :]
    xs = a + (shifted - a) * mu_ref[...]

    r = xs[:, 0:W]
    k = xs[:, W:2 * W]
    v = xs[:, 2 * W:3 * W]
    w_lo = xs[:, 3 * W:3 * W + LORA_PAD]
    a_lo = xs[:, 3 * W + LORA_PAD:3 * W + 2 * LORA_PAD]
    g_lo = xs[:, 3 * W + 2 * LORA_PAD:]

    z = w0_ref[...] + _dot(jnp.tanh(w_lo).astype(BF16), w2_ref[...])
    w = -jax.nn.softplus(-z) - 0.5
    lw = -jnp.exp(w)
    aa = jax.nn.sigmoid(a0_ref[...] + _dot(a_lo.astype(BF16), a2_ref[...]))
    g = _dot(jax.nn.sigmoid(g_lo).astype(BF16), g2_ref[...])

    seg = seg_ref[...]
    kk = k * kks_ref[...]
    ss = _dot((kk * kk).astype(BF16), seg)
    kk = kk * lax.rsqrt(jnp.maximum(ss, 1e-24))
    km = k * (1.0 + (aa - 1.0) * kas_ref[...])
    beta = kk * aa

    tri = tri_ref[...]
    lw_hi = lw.astype(BF16)
    lw_r = lw - lw_hi.astype(F32)
    lw_mid = lw_r.astype(BF16)
    lw_lo = (lw_r - lw_mid.astype(F32)).astype(BF16)
    cum = _dot(tri, lw_hi) + (_dot(tri, lw_mid) + _dot(tri, lw_lo))
    cum3 = cum.reshape(TB // C, C, W)
    cum_c = jnp.broadcast_to(cum3[:, C - 1:C, :], cum3.shape).reshape(TB, W)

    e_in = jnp.exp(cum)
    e_neg = jnp.exp(-cum)
    e_end = jnp.exp(cum_c - cum)
    at_ref[...] = (-kk * jnp.exp(cum - lw)).astype(BF16)
    rt_ref[...] = (r * e_in).astype(BF16)
    bt_ref[...] = (beta * e_neg).astype(BF16)
    kt_ref[...] = (km * e_neg).astype(BF16)
    bp_ref[...] = (beta * e_end).astype(BF16)
    kp_ref[...] = (km * e_end).astype(BF16)
    v_ref[...] = v.astype(BF16)
    gc_ref[...] = jnp.exp(cum_c)

    ri = lax.broadcasted_iota(jnp.int32, (C, C), 0)
    ci = lax.broadcasted_iota(jnp.int32, (C, C), 1)
    strict = ci < ri
    incl = ci <= ri
    eye = ci == ri

    def chunk(c, carry):
        NC = TB // C
        rows = [pl.ds(pl.multiple_of((c * NC + ci) * C, C), C) for ci in range(NC)]
        HS = range(NC * RWKV_HEADS)
        rw = [rows[i // RWKV_HEADS] for i in HS]
        sl = [slice((i % RWKV_HEADS) * N, (i % RWKV_HEADS + 1) * N) for i in HS]
        cat = lambda a, b: jnp.concatenate([a, b], axis=0)
        at = [at_ref[rw[i], sl[i]] for i in HS]
        rt = [rt_ref[rw[i], sl[i]] for i in HS]
        bt = [bt_ref[rw[i], sl[i]] for i in HS]
        kt = [kt_ref[rw[i], sl[i]] for i in HS]
        bp = [bp_ref[rw[i], sl[i]] for i in HS]
        kp = [kp_ref[rw[i], sl[i]] for i in HS]
        vh = [v_ref[rw[i], sl[i]] for i in HS]
        ar = [cat(at[h], rt[h]) for h in HS]
        xb = [_nt(ar[h], bt[h]) for h in HS]
        xk = [_nt(ar[h], kt[h]) for h in HS]
        a_ab = [jnp.where(strict, xb[h][:C], 0.0) for h in HS]
        a_rb = [jnp.where(incl, xb[h][C:], 0.0).astype(BF16) for h in HS]
        a_ak = [jnp.where(strict, xk[h][:C], 0.0).astype(BF16) for h in HS]
        a_rk = [jnp.where(incl, xk[h][C:], 0.0).astype(BF16) for h in HS]
        ident = jnp.where(eye, 1.0, 0.0)
        t_inv = [ident + a_ab[h] for h in HS]
        ap = a_ab
        for _ in range(5):
            apb = [ap[h].astype(BF16) for h in HS]
            ap = [_dot(apb[h], apb[h]) for h in HS]
            t_inv = [t_inv[h] + _dot(t_inv[h].astype(BF16), ap[h].astype(BF16)) for h in HS]
        tb = [t_inv[h].astype(BF16) for h in HS]
        wv = [_dot(a_ak[h], vh[h]).astype(BF16) for h in HS]
        pb = [_dot(tb[h], at[h]).astype(BF16) for h in HS]
        u0b = [_dot(tb[h], wv[h]).astype(BF16) for h in HS]
        qm = [rt[h].astype(F32) + _dot(a_rb[h], pb[h]) for h in HS]
        y0 = [_dot(a_rb[h], u0b[h]) + _dot(a_rk[h], vh[h]) for h in HS]
        gm = [jnp.where(eye, gc_ref[rw[h], sl[h]][0:1, :], 0.0) + _tn(bp[h], pb[h]) for h in HS]
        hadd = [_tn(cat(bp[h], kp[h]), cat(u0b[h], vh[h])) for h in HS]
        qg = [cat(qm[h], gm[h]).astype(BF16) for h in HS]
        for i in HS:
            hd = i % RWKV_HEADS
            z = _dot(qg[i], h_ref[hd].astype(BF16))
            y_ref[rw[i], sl[i]] = z[:C] + y0[i]
            h_ref[hd] = z[C:] + hadd[i]
        return carry

    chunk(0, 0)

    y = y_ref[...]
    segm = seg_ref[...]
    mu = _dot(y.astype(BF16), segm) * (1.0 / N)
    yc = y - mu
    var = _dot((yc * yc).astype(BF16), segm) * (1.0 / N)
    yn = yc * lax.rsqrt(var + RWKV_GN_EPS) * gng_ref[...] + gnb_ref[...]
    bonus = _dot((r * km * rk_ref[...]).astype(BF16), segm) * v
    o_ref[...] = ((yn + bonus) * g).astype(o_ref.dtype)


def _rwkv(a_proj, mu, w2, w0, a2, a0, g2, kks, kas, rk, gng, gnb, tb):
    T = a_proj.shape[0]
    tb = min(tb, T)
    W = RWKV_WIDTH
    hid = np.arange(W) // RWKV_HEAD
    seg = jnp.asarray(hid[:, None] == hid[None, :], dtype=BF16)
    t_idx = np.arange(tb)
    tri = jnp.asarray((t_idx[:, None] // RWKV_CHUNK == t_idx[None, :] // RWKV_CHUNK)
                      & (t_idx[None, :] <= t_idx[:, None]), dtype=BF16)
    full = lambda shp: pl.BlockSpec(shp, lambda i: (0,) * len(shp))
    row = lambda n: full((1, n))
    bf = lambda: pltpu.VMEM((tb, W), BF16)
    return pl.pallas_call(
        _rwkv_kernel,
        grid=(T // tb,),
        in_specs=[pl.BlockSpec((tb, RWKV_IN_PAD), lambda i: (i, 0)),
                  row(RWKV_IN_PAD), full((LORA_PAD, W)), row(W), full((LORA_PAD, W)), row(W),
                  full((RWKV_GATE_LORA, W)), row(W), row(W), row(W), row(W), row(W),
                  full((W, W)), full((tb, tb))],
        out_specs=pl.BlockSpec((tb, W), lambda i: (i, 0)),
        out_shape=jax.ShapeDtypeStruct((T, W), BF16),
        scratch_shapes=[pltpu.VMEM((1, RWKV_IN_PAD), F32),
                        pltpu.VMEM((RWKV_HEADS, RWKV_HEAD, RWKV_HEAD), F32),
                        bf(), bf(), bf(), bf(), bf(), bf(), bf(),
                        pltpu.VMEM((tb, W), F32), pltpu.VMEM((tb, W), F32)],
        compiler_params=_cparams(("arbitrary",)),
        name="rwkv7",
    )(a_proj, mu, w2, w0, a2, a0, g2, kks, kas, rk, gng, gnb, seg, tri)


def _diff_kernel(q_ref, k_ref, vt_ref, lq1_ref, lk1_ref, lq2_ref, lk2_ref, sg_ref, o_ref,
                 m_ref, l_ref, acc_ref, s_ref, p_ref, a_ref, *, lambda_init):
    tq = q_ref.shape[0]
    nkb, _, tk = vt_ref.shape
    i = pl.program_id(1)
    q = q_ref[...]
    lane = lax.broadcasted_iota(jnp.int32, q.shape, 1)
    qm = (jnp.where(lane < DIFF_HEAD, q, jnp.zeros_like(q)),
          jnp.where(lane >= DIFF_HEAD, q, jnp.zeros_like(q)))
    m_ref[...] = jnp.full(m_ref.shape, -jnp.inf, F32)
    l_ref[...] = jnp.zeros_like(l_ref)
    acc_ref[...] = jnp.zeros_like(acc_ref)
    p_ref[1] = jnp.zeros(p_ref.shape[1:], BF16)
    a_ref[1] = jnp.ones(a_ref.shape[1:], F32)

    def scores(j, slot):
        jc = jnp.minimum(j, nkb - 1)
        kj = k_ref[pl.ds(pl.multiple_of(jc * tk, tk), tk), :]
        for mi in range(2):
            s_ref[slot, mi] = _nt(kj, qm[mi])

    def values(j, slot):
        vj = vt_ref[jnp.clip(j, 0, nkb - 1)]
        for mi in range(2):
            acc_ref[mi] = a_ref[slot, mi] * acc_ref[mi] + _dot(vj, p_ref[slot, mi])

    def softmax(j, slot, masked):
        for mi in range(2):
            s = s_ref[slot, mi]
            if masked:
                d = (lax.broadcasted_iota(jnp.int32, (tk, tq), 0)
                     - lax.broadcasted_iota(jnp.int32, (tk, tq), 1))
                s = jnp.where(d <= i * tq - j * tk, s, -jnp.inf)
            m_old = m_ref[mi]
            m_new = jnp.maximum(m_old, jnp.max(s, axis=0, keepdims=True))
            alpha = jnp.exp2(m_old - m_new)
            p = jnp.exp2(s - m_new)
            l_ref[mi] = alpha * l_ref[mi] + jnp.sum(p, axis=0, keepdims=True)
            m_ref[mi] = m_new
            a_ref[slot, mi] = alpha
            p_ref[slot, mi] = p.astype(BF16)

    def half(j, cur, masked):
        nxt = 1 - cur
        scores(j + 1, nxt)
        values(j - 1, nxt)
        softmax(j, cur, masked)

    def pair(jj, carry):
        half(2 * jj, 0, False)
        half(2 * jj + 1, 1, False)
        return carry

    scores(0, 0)
    n_pairs = ((i + 1) * (tq // tk) + 1) // 2
    lax.fori_loop(0, n_pairs - 1, pair, 0)
    j_last = 2 * (n_pairs - 1)
    half(j_last, 0, True)
    half(j_last + 1, 1, True)
    values(j_last + 1, 1)

    lam = (jnp.exp(jnp.sum(lq1_ref[...] * lk1_ref[...])) - jnp.exp(jnp.sum(lq2_ref[...] * lk2_ref[...]))
           + lambda_init)
    o = acc_ref[0] / l_ref[0] - lam * (acc_ref[1] / l_ref[1])
    o = o * lax.rsqrt(jnp.mean(o * o, axis=0, keepdims=True) + LN_EPS) * (1.0 - lambda_init)
    o = o * sg_ref[...]
    o_ref[...] = o.T.astype(o_ref.dtype)


def _diff_attention(qk, vt, lq1, lk1, lq2, lk2, subln_g, lambda_init, tq):
    T = qk.shape[0]
    tk = vt.shape[2]
    assert tq % tk == 0 and T % tq == 0
    H, E = DIFF_HEADS, 2 * DIFF_HEAD
    vec = lambda: pl.BlockSpec((1, DIFF_HEAD), lambda h, i: (0, 0))
    return pl.pallas_call(
        functools.partial(_diff_kernel, lambda_init=lambda_init),
        grid=(H, T // tq),
        in_specs=[pl.BlockSpec((tq, E), lambda h, i: (i, h)),
                  pl.BlockSpec((T, E), lambda h, i: (0, H + h)),
                  pl.BlockSpec((T // tk, E, tk), lambda h, i: (0, h, 0)),
                  vec(), vec(), vec(), vec(),
                  pl.BlockSpec((E, 1), lambda h, i: (0, 0))],
        out_specs=pl.BlockSpec((tq, E), lambda h, i: (i, h)),
        out_shape=jax.ShapeDtypeStruct((T, DIFF_V), BF16),
        scratch_shapes=[pltpu.VMEM((2, 1, tq), F32), pltpu.VMEM((2, 1, tq), F32),
                        pltpu.VMEM((2, E, tq), F32), pltpu.VMEM((2, 2, tk, tq), F32),
                        pltpu.VMEM((2, 2, tk, tq), BF16), pltpu.VMEM((2, 2, 1, tq), F32)],
        compiler_params=_cparams(("arbitrary", "arbitrary")),
        name="diff_attn",
    )(qk, qk, vt, lq1, lk1, lq2, lk2, subln_g)


def _gla_kernel(c_ref, g2_ref, gb_ref, gng_ref, tri_ref, segv_ref, o_ref,
                st_ref, oacc_ref):
    TB = c_ref.shape[0]
    KW, VW, S = GLA_K_WIDTH, GLA_V_WIDTH, GLA_SUB
    DK, DV = GLA_KEY, GLA_VALUE

    @pl.when(pl.program_id(0) == 0)
    def _():
        st_ref[...] = jnp.zeros_like(st_ref)

    g_lo = c_ref[:, 2 * KW + 2 * VW:]
    gate = _dot(g_lo, g2_ref[...], precision=HIGHEST) + gb_ref[...]
    log_a = jax.nn.log_sigmoid(gate) * (1.0 / GLA_TAU)
    b = _dot(tri_ref[...], log_a, precision=HIGHEST)
    NS = TB // S
    b3 = b.reshape(NS, S, KW)
    q3 = c_ref[:, 0:KW].reshape(NS, S, KW)
    k3 = c_ref[:, KW:2 * KW].reshape(NS, S, KW)
    v = c_ref[:, 2 * KW:2 * KW + VW]
    v3 = v.reshape(NS, S, VW)
    bl3 = b3[:, S - 1:S, :]
    qb = (q3 * jnp.exp(b3)).astype(BF16).reshape(TB, KW)
    kb = (k3 * jnp.exp(bl3 - b3)).astype(BF16).reshape(TB, KW)
    dec = jnp.exp(bl3)
    vb = v.astype(BF16)

    ri = lax.broadcasted_iota(jnp.int32, (NS, S, 1), 1)
    segv = segv_ref[...]
    acc = jnp.zeros((NS, S, VW), F32)
    for j in range(S):
        e = jnp.exp(jnp.minimum(b3 - b3[:, j:j + 1, :], 0.0))
        t = (q3 * e * k3[:, j:j + 1, :]).astype(BF16).reshape(TB, KW)
        att = _dot(t, segv).reshape(NS, S, VW)
        acc = acc + jnp.where(ri >= j, att, 0.0) * v3[:, j:j + 1, :]
    oacc_ref[...] = acc.reshape(TB, VW)

    for h in range(GLA_HEADS):
        lk = slice(h * DK, (h + 1) * DK)
        lv = slice(h * DV, (h + 1) * DV)
        kv = [_tn(vb[s * S:(s + 1) * S, lv], kb[s * S:(s + 1) * S, lk]) for s in range(NS)]
        st = st_ref[h]
        for s in range(NS):
            rows = slice(s * S, (s + 1) * S)
            oacc_ref[rows, lv] += _nt(qb[rows, lk], st.astype(BF16))
            st = st * dec[s][:, lk] + kv[s]
        st_ref[h] = st

    o = oacc_ref[...]
    rgate = c_ref[:, 2 * KW + VW:2 * KW + 2 * VW]
    for h in range(GLA_HEADS):
        lv = slice(h * DV, (h + 1) * DV)
        oh = o[:, lv]
        oh = oh * lax.rsqrt(jnp.mean(oh * oh, axis=1, keepdims=True) + LN_EPS) * gng_ref[:, lv]
        o_ref[:, lv] = (oh * jax.nn.silu(rgate[:, lv])).astype(o_ref.dtype)


def _gla(c_proj, g2, gb, gng, tb):
    T = c_proj.shape[0]
    tb = min(tb, T)
    KW, VW = GLA_K_WIDTH, GLA_V_WIDTH
    t_idx = np.arange(tb)
    tri = jnp.asarray((t_idx[:, None] // GLA_SUB == t_idx[None, :] // GLA_SUB)
                      & (t_idx[None, :] <= t_idx[:, None]), dtype=F32)
    segv = jnp.asarray((np.arange(KW) // GLA_KEY)[:, None] == (np.arange(VW) // GLA_VALUE)[None, :], dtype=BF16)
    full = lambda shp: pl.BlockSpec(shp, lambda i: (0,) * len(shp))
    return pl.pallas_call(
        _gla_kernel,
        grid=(T // tb,),
        in_specs=[pl.BlockSpec((tb, GLA_IN_PAD), lambda i: (i, 0)),
                  full((LANE, KW)), full((1, KW)), full((1, VW)), full((tb, tb)), full((KW, VW))],
        out_specs=pl.BlockSpec((tb, VW), lambda i: (i, 0)),
        out_shape=jax.ShapeDtypeStruct((T, VW), BF16),
        scratch_shapes=[pltpu.VMEM((GLA_HEADS, GLA_VALUE, GLA_KEY), F32),
                        pltpu.VMEM((tb, VW), F32)],
        compiler_params=_cparams(("arbitrary",)),
        name="gla",
    )(c_proj, g2, gb, gng, tri, segv)


def _merge_kernel(oa_ref, ob_ref, oc_ref, g_ref, pa_ref, pb_ref, pc_ref, o_ref):
    D = D_MODEL
    m = g_ref[:, 0:D].astype(F32) * _dot(oa_ref[...], pa_ref[...])
    m = m + g_ref[:, D:2 * D].astype(F32) * _dot(ob_ref[...], pb_ref[...])
    m = m + g_ref[:, 2 * D:3 * D].astype(F32) * _dot(oc_ref[...], pc_ref[...])
    o_ref[...] = m.astype(o_ref.dtype)


def _merge(oa, ob, oc, gates, pa, pb, pc, tm):
    T = oa.shape[0]
    tm = min(tm, T)
    D = D_MODEL
    rowblk = lambda n: pl.BlockSpec((tm, n), lambda i: (i, 0))
    full = lambda shp: pl.BlockSpec(shp, lambda i: (0, 0))
    return pl.pallas_call(
        _merge_kernel,
        grid=(T // tm,),
        in_specs=[rowblk(RWKV_WIDTH), rowblk(DIFF_V), rowblk(GLA_V_WIDTH), rowblk(3 * D),
                  full((RWKV_WIDTH, D)), full((DIFF_V, D)), full((GLA_V_WIDTH, D))],
        out_specs=rowblk(D),
        out_shape=jax.ShapeDtypeStruct((T, D), BF16),
        compiler_params=_cparams(("arbitrary",)),
        name="merge",
    )(oa, ob, oc, gates, pa, pb, pc)


def _layer_norm(z, g, b):
    mu = jnp.mean(z, axis=1, keepdims=True)
    zc = z - mu
    var = jnp.mean(zc * zc, axis=1, keepdims=True)
    return zc * lax.rsqrt(var + LN_EPS) * g + b


def _mm_ln_kernel(a_ref, w_ref, res_ref, g_ref, b_ref, wrh_ref, wrl_ref, br_ref, o_ref):
    z = DEEPNORM_ALPHA * res_ref[...] + _dot(a_ref[...], w_ref[...])
    out = _layer_norm(z, g_ref[...], b_ref[...])
    o_ref[:, 0:D_MODEL] = out
    o_ref[:, D_MODEL:] = _route(out, wrh_ref[...], wrl_ref[...], br_ref[...])


def _mm_ln(a, w, res, g, b, wr, br, tm):
    wr_hi = wr.astype(BF16)
    wr_lo = (wr - wr_hi.astype(F32)).astype(BF16)
    T, K = a.shape
    D = D_MODEL
    tm = min(tm, T)
    rowblk = lambda n: pl.BlockSpec((tm, n), lambda i: (i, 0))
    full = lambda shp: pl.BlockSpec(shp, lambda i: (0, 0))
    return pl.pallas_call(
        _mm_ln_kernel,
        grid=(T // tm,),
        in_specs=[rowblk(K), full((K, D)), rowblk(D), full((1, D)), full((1, D)), full((D, LANE)), full((D, LANE)),
                  full((1, LANE))],
        out_specs=rowblk(D + LANE),
        out_shape=jax.ShapeDtypeStruct((T, D + LANE), F32),
        compiler_params=_cparams(("arbitrary",)),
        name="wo_ln_route",
    )(a, w, res, g, b, wr_hi, wr_lo, br)


def _route(x, w_hi, w_lo, b):
    G, E = MOE_GROUPS, MOE_EXPERTS
    x_hi = x.astype(BF16)
    x_lo = (x - x_hi.astype(F32)).astype(BF16)
    logits = _dot(x_hi, w_hi) + (_dot(x_lo, w_hi) + _dot(x_hi, w_lo)) + b
    lane = lax.broadcasted_iota(jnp.int32, logits.shape, 1)
    neg = -jnp.inf
    big = jnp.int32(1 << 20)
    is_g = lane < G
    lg = jnp.where(is_g, logits, neg)
    gmax = jnp.max(lg, axis=1, keepdims=True)
    gidx = jnp.min(jnp.where(is_g & (lg == gmax), lane, big), axis=1, keepdims=True)
    g_p = 1.0 / jnp.sum(jnp.exp(lg - gmax), axis=1, keepdims=True)
    lo = G + gidx * E
    in_grp = (lane >= lo) & (lane < lo + E)
    le = jnp.where(in_grp, logits, neg)
    m1 = jnp.max(le, axis=1, keepdims=True)
    i1 = jnp.min(jnp.where(in_grp & (le == m1), lane, big), axis=1, keepdims=True)
    le2 = jnp.where(lane == i1, neg, le)
    m2 = jnp.max(le2, axis=1, keepdims=True)
    i2 = jnp.min(jnp.where(in_grp & (le2 == m2), lane, big), axis=1, keepdims=True)
    e2 = jnp.exp(m2 - m1)
    p1 = 1.0 / (1.0 + e2)
    p2 = e2 / (1.0 + e2)
    comb = jnp.where(lane == i1, g_p * p1, 0.0) + jnp.where(lane == i2, g_p * p2, 0.0)
    return jnp.where(lane == 0, gidx.astype(F32), comb)


def _moe_tables(gid_f, tm):
    T = gid_f.shape[0]
    G = MOE_GROUPS
    nt = T // tm + G
    gid = gid_f.astype(jnp.int32)
    order = jnp.argsort(gid).astype(jnp.int32)
    counts = jnp.sum((gid[:, None] == jnp.arange(G)[None, :]).astype(jnp.int32), axis=0)
    starts = jnp.cumsum(counts) - counts
    ntile = (counts + tm - 1) // tm
    tstart = jnp.cumsum(ntile) - ntile
    n = jnp.arange(nt, dtype=jnp.int32)
    tg = jnp.sum((n[:, None] >= tstart[None, 1:]).astype(jnp.int32), axis=1)
    k = n - tstart[tg]
    row0 = starts[tg] + k * tm
    nvalid = jnp.clip(counts[tg] - k * tm, 0, tm).astype(jnp.int32)
    rows = jnp.clip(row0[:, None] + jnp.arange(tm, dtype=jnp.int32)[None, :], 0, T - 1)
    idx = jnp.take(order, rows).reshape(nt, 1, tm)
    return tg.astype(jnp.int32), nvalid, idx


def _moe_kernel(tg_ref, nv_ref, idx_ref, idxn_ref, x_hbm, wg_ref, wu_ref, wd_ref, g_ref, b_ref, y_hbm,
                xs_ref, xb_ref, h_ref, out_ref, pend_ref, gx_sem, sc_sem):
    n = pl.program_id(0)
    e = pl.program_id(1)
    nt = pl.num_programs(0)
    nv = nv_ref[n]
    slot = n % 2
    other = 1 - slot
    tm = out_ref.shape[0]
    part = tm // MOE_EXPERTS
    F = MOE_HIDDEN

    def start_row(idx, dst_slot, r, prio):
        tok = idx[0, 0, r]
        pltpu.make_async_copy(x_hbm.at[pl.ds(tok, 1)], xs_ref.at[dst_slot, pl.ds(r, 1)],
                              gx_sem.at[dst_slot]).start(priority=prio)

    def start_next_part():
        for r in range(part):
            start_row(idxn_ref, other, e * part + r, r % 2)

    def wait_rows(s):
        pltpu.make_async_copy(x_hbm.at[pl.ds(0, tm)], xs_ref.at[s], gx_sem.at[s]).wait()

    def drain():
        @pl.when(pend_ref[0] == tm)
        def _():
            pltpu.make_async_copy(out_ref, y_hbm.at[pl.ds(0, tm)], sc_sem.at[0]).wait()

        @pl.when(pend_ref[0] < tm)
        def _():
            def body(r, c):
                pltpu.make_async_copy(out_ref.at[pl.ds(0, 1)], y_hbm.at[pl.ds(0, 1)], sc_sem.at[0]).wait()
                return c
            lax.fori_loop(0, pend_ref[0], body, 0)
        pend_ref[0] = 0

    @pl.when((n == 0) & (e == 0))
    def _():
        pend_ref[0] = 0

        def body(r, c):
            start_row(idx_ref, 0, 2 * r, 0)
            start_row(idx_ref, 0, 2 * r + 1, 1)
            return c
        lax.fori_loop(0, tm // 2, body, 0)

    @pl.when(e == 0)
    def _():
        wait_rows(slot)
        xb_ref[...] = xs_ref[slot, :, 0:D_MODEL].astype(BF16)

    @pl.when(nv > 0)
    def _():
        start_next_part()
        xb = xb_ref[...]
        hg = _dot(xb, wg_ref[0])
        hu = _dot(xb, wu_ref[0])
        comb = xs_ref[slot, :, D_MODEL:]
        lane = lax.broadcasted_iota(jnp.int32, comb.shape, 1)
        col = MOE_GROUPS + tg_ref[n] * MOE_EXPERTS + e
        cw = jnp.sum(jnp.where(lane == col, comb, 0.0), axis=1, keepdims=True)
        h_ref[e] = (jax.nn.silu(hg) * hu * cw).astype(BF16)

    @pl.when(nv == 0)
    def _():
        start_next_part()

    @pl.when(e == MOE_EXPERTS - 1)
    def _():
        drain()

        @pl.when(nv > 0)
        def _():
            y = _dot(h_ref[0], wd_ref[0, 0:F, :])
            for j in range(1, MOE_EXPERTS):
                y = y + _dot(h_ref[j], wd_ref[0, j * F:(j + 1) * F, :])
            z = DEEPNORM_ALPHA * xs_ref[slot, :, 0:D_MODEL] + y
            out_ref[...] = _layer_norm(z, g_ref[...], b_ref[...])

        def put(r, prio):
            tok = idx_ref[0, 0, r]
            pltpu.make_async_copy(out_ref.at[pl.ds(r, 1)], y_hbm.at[pl.ds(tok, 1)],
                                  sc_sem.at[0]).start(priority=prio)

        def body(r, c):
            put(2 * r, 0)
            put(2 * r + 1, 1)
            return c
        lax.fori_loop(0, nv // 2, body, 0)

        @pl.when(nv % 2 == 1)
        def _():
            put(nv - 1, 0)
        pend_ref[0] = nv

        @pl.when(n == nt - 1)
        def _():
            drain()
            wait_rows(other)


def _moe(xa, wg, wu, wd, l, g, b, tm):
    T = xa.shape[0]
    D = D_MODEL
    F = MOE_HIDDEN
    tm = min(tm, T)
    tg, nvalid, idx = _moe_tables(xa[:, D], tm)
    nt = tg.shape[0]
    wspec = lambda shp: pl.BlockSpec(shp, lambda n, e, tg_r, nv_r: (l * N_EXPERTS + tg_r[n] * MOE_EXPERTS + e, 0, 0))
    ispec = lambda f: pl.BlockSpec((1, 1, tm), f, memory_space=pltpu.SMEM)
    vec = pl.BlockSpec((1, D), lambda n, e, tg_r, nv_r: (0, 0))
    grid_spec = pltpu.PrefetchScalarGridSpec(
        num_scalar_prefetch=2,
        grid=(nt, MOE_EXPERTS),
        in_specs=[ispec(lambda n, e, tg_r, nv_r: (n, 0, 0)),
                  ispec(lambda n, e, tg_r, nv_r: (jnp.minimum(n + 1, nt - 1), 0, 0)),
                  pl.BlockSpec(memory_space=pl.ANY),
                  wspec((1, D, F)), wspec((1, D, F)),
                  pl.BlockSpec((1, MOE_EXPERTS * F, D), lambda n, e, tg_r, nv_r: (l * MOE_GROUPS + tg_r[n], 0, 0),
                               pipeline_mode=pl.Buffered(1)),
                  vec, vec],
        out_specs=pl.BlockSpec(memory_space=pl.ANY),
        scratch_shapes=[pltpu.VMEM((2, tm, D + LANE), F32),
                        pltpu.VMEM((tm, D), BF16), pltpu.VMEM((MOE_EXPERTS, tm, F), BF16),
                        pltpu.VMEM((tm, D), F32), pltpu.SMEM((1,), jnp.int32),
                        pltpu.SemaphoreType.DMA((2,)), pltpu.SemaphoreType.DMA((1,))],
    )
    return pl.pallas_call(
        _moe_kernel,
        grid_spec=grid_spec,
        out_shape=jax.ShapeDtypeStruct((T, D), F32),
        compiler_params=_cparams(("arbitrary", "arbitrary")),
        name="moe_experts_ln",
    )(tg, nvalid, idx, idx, xa, wg, wu, wd, g, b)


def _pad_cols(w, n):
    return jnp.pad(w, ((0, 0), (0, n - w.shape[1])))


def _prep_weights(p):
    W = RWKV_WIDTH
    KW, VW = GLA_K_WIDTH, GLA_V_WIDTH
    wt = jnp.swapaxes(p['w_in'], 1, 2)
    o_b = RWKV_IN
    o_c = o_b + DIFF_IN
    o_g = o_c + GLA_IN
    lo1 = 3 * W + RWKV_DECAY_LORA
    lo2 = lo1 + RWKV_AAA_LORA
    padr = lambda w, n: jnp.pad(w, ((0, 0), (0, n - w.shape[1]), (0, 0)))
    w_a = jnp.concatenate([padr(wt[:, :lo1], 3 * W + LORA_PAD), padr(wt[:, lo1:lo2], LORA_PAD),
                           wt[:, lo2:o_b]], axis=1).astype(BF16)
    qscale = DIFF_HEAD ** -0.5 * math.log2(math.e)
    w_qk = jnp.concatenate([wt[:, o_b:o_b + DIFF_QK] * qscale, wt[:, o_b + DIFF_QK:o_b + 2 * DIFF_QK]],
                           axis=1).astype(BF16)
    w_vt = wt[:, o_b + 2 * DIFF_QK:o_c].astype(BF16)
    c0 = o_c
    w_c = jnp.concatenate([wt[:, c0:c0 + KW] * (GLA_KEY ** -0.5), wt[:, c0 + KW:c0 + 2 * KW + VW],
                           wt[:, c0 + 2 * KW + VW + GLA_GATE_LORA:o_g],
                           padr(wt[:, c0 + 2 * KW + VW:c0 + 2 * KW + VW + GLA_GATE_LORA], LANE)],
                          axis=1).astype(BF16)
    w_g = wt[:, o_g:].astype(BF16)
    w_in = p['w_in']
    L = w_in.shape[0]
    F = MOE_HIDDEN
    return dict(
        w_a=w_a, w_qk=w_qk, w_vt=w_vt, w_c=w_c, w_g=w_g,
        wg=p['moe_w_gate'].astype(BF16).reshape(L * N_EXPERTS, D_MODEL, F),
        wu=p['moe_w_up'].astype(BF16).reshape(L * N_EXPERTS, D_MODEL, F),
        wd=p['moe_w_down'].astype(BF16).reshape(L * MOE_GROUPS, MOE_EXPERTS * F, D_MODEL))


def _layer(x, xb, l, p, wts, rope_tabs):
    T = x.shape[0]
    W = RWKV_WIDTH
    lambda_init = 0.8 - 0.6 * math.exp(-0.3 * l)
    lo1 = 3 * W + RWKV_DECAY_LORA
    lo2 = lo1 + RWKV_AAA_LORA
    mu = p['rwkv_mu'][l]
    mu_a = jnp.concatenate([jnp.pad(mu[:lo1], (0, LORA_PAD - RWKV_DECAY_LORA)),
                            jnp.pad(mu[lo1:lo2], (0, LORA_PAD - RWKV_AAA_LORA)), mu[lo2:]])[None, :]

    proj_a = _matmul(xb, wts['w_a'], l, F32, 1024, 1024)
    qk = _matmul_rope(xb, wts['w_qk'], l, rope_tabs, 1024, 1024)
    tk = min(512, T)
    vt = _matmul_vt(wts['w_vt'], l, xb, tk)
    proj_c = _matmul(xb, wts['w_c'], l, F32, 512, GLA_IN_PAD)
    gates = _matmul(xb, wts['w_g'], l, BF16, 1024, 1024, gate=True)

    padr = lambda w: jnp.pad(w, ((0, LORA_PAD - w.shape[0]), (0, 0)))
    r2 = lambda v: v.reshape(1, -1)
    o_a = _rwkv(proj_a, mu_a, padr(p['rwkv_w2'][l]).astype(BF16), r2(p['rwkv_w0'][l]),
                padr(p['rwkv_a2'][l]).astype(BF16), r2(p['rwkv_a0'][l]), p['rwkv_g2'][l].astype(BF16),
                r2(p['rwkv_kk_scale'][l]), r2(p['rwkv_ka_scale'][l]), r2(p['rwkv_rk'][l]),
                r2(p['rwkv_gn_g'][l]), r2(p['rwkv_gn_b'][l]), 256)
    o_bb = _diff_attention(qk, vt, r2(p['diff_lq1'][l]), r2(p['diff_lk1'][l]), r2(p['diff_lq2'][l]),
                           r2(p['diff_lk2'][l]), p['diff_subln_g'][l].reshape(-1, 1), lambda_init, min(2 * tk, T))
    g2p = jnp.pad(p['gla_g2'][l], ((0, LANE - GLA_GATE_LORA), (0, 0)))
    o_c = _gla(proj_c, g2p, r2(p['gla_gb'][l]), r2(p['gla_gn_g'][l]), 256)

    merged = _merge(o_a, o_bb, o_c, gates, p['proj_a'][l].astype(BF16), p['proj_b'][l].astype(BF16),
                    p['proj_c'][l].astype(BF16), 256)
    wr = _pad_cols(jnp.concatenate([p['router_group_w'][l], p['router_expert_w'][l]], axis=1), LANE)
    br = _pad_cols(jnp.concatenate([p['router_group_b'][l], p['router_expert_b'][l]])[None, :], LANE)
    xa = _mm_ln(merged, p['w_o'][l].astype(BF16), x, r2(p['ln1_g'][l]), r2(p['ln1_b'][l]), wr, br, 512)
    return _moe(xa, wts['wg'], wts['wu'], wts['wd'], l, r2(p['ln2_g'][l]), r2(p['ln2_b'][l]), MOE_TILE)


def kernel(x, w_in, rwkv_mu, rwkv_w2, rwkv_w0, rwkv_a2, rwkv_a0, rwkv_g2, rwkv_kk_scale, rwkv_ka_scale, rwkv_rk, rwkv_gn_g, rwkv_gn_b, diff_lq1, diff_lk1, diff_lq2, diff_lk2, diff_subln_g, gla_g2, gla_gb, gla_gn_g, proj_a, proj_b, proj_c, w_o, ln1_g, ln1_b, router_group_w, router_group_b, router_expert_w, router_expert_b, moe_w_gate, moe_w_up, moe_w_down, ln2_g, ln2_b):
    p = dict(w_in=w_in, rwkv_mu=rwkv_mu, rwkv_w2=rwkv_w2, rwkv_w0=rwkv_w0, rwkv_a2=rwkv_a2, rwkv_a0=rwkv_a0,
             rwkv_g2=rwkv_g2, rwkv_kk_scale=rwkv_kk_scale, rwkv_ka_scale=rwkv_ka_scale, rwkv_rk=rwkv_rk,
             rwkv_gn_g=rwkv_gn_g, rwkv_gn_b=rwkv_gn_b, diff_lq1=diff_lq1, diff_lk1=diff_lk1, diff_lq2=diff_lq2,
             diff_lk2=diff_lk2, diff_subln_g=diff_subln_g, gla_g2=gla_g2, gla_gb=gla_gb, gla_gn_g=gla_gn_g,
             proj_a=proj_a, proj_b=proj_b, proj_c=proj_c, w_o=w_o, ln1_g=ln1_g, ln1_b=ln1_b,
             router_group_w=router_group_w, router_group_b=router_group_b, router_expert_w=router_expert_w,
             router_expert_b=router_expert_b, moe_w_gate=moe_w_gate, moe_w_up=moe_w_up, moe_w_down=moe_w_down,
             ln2_g=ln2_g, ln2_b=ln2_b)
    B, T, D = x.shape
    assert B == 1 and D == D_MODEL
    xf = x.reshape(T, D)
    xb = xf.astype(BF16)
    tabs = _rope_tables(T)
    wts = _prep_weights(p)
    for l in range(DEPTH):
        xf = _layer(xf, xb, l, p, wts, tabs)
        xb = xf.astype(BF16)
    return xf.reshape(B, T, D)
```
